```python
import math
import jax, jax.numpy as jnp
from jax import lax
import numpy as np

D_MODEL = 1024
BATCH = 8
SEQ = 2048
DEPTH = 1
DEC_BATCH = 128
DEC_SEQ = 1
PAST_LEN = 16384
PAGE_SIZE = 128

C_CONF = D_MODEL
CONF_GROUPS = 16
K_CONF = 31
D_INNER = D_MODEL
SSM_HEAD_DIM = 64
N_SSM_HEADS = D_INNER // SSM_HEAD_DIM
N_SSM_GROUPS = 4
D_STATE = 128
K_SSM = 4
CHUNK = 128
CONV_DIM = D_INNER + 2 * N_SSM_GROUPS * D_STATE
D_MIX = C_CONF + D_INNER
IN_COLS = 2 * C_CONF + D_INNER + CONV_DIM + N_SSM_HEADS
N_EXPERT_GROUPS = 4
EXPERTS_PER_GROUP = 4
N_EXPERTS = N_EXPERT_GROUPS * EXPERTS_PER_GROUP
TOP_K = 2
D_FF_EXPERT = D_MODEL // 2
PLE_DIM = 256
EPS = 1e-6

kernel_name = 'hybrid_conv_ssd_hmoe_step'


def rmsnorm(x, g):
    xf = x.astype(jnp.float32)
    xf = xf * lax.rsqrt(jnp.mean(xf * xf, axis=-1, keepdims=True) + EPS)
    return (xf * g.astype(jnp.float32)).astype(x.dtype)


def layernorm(x, g, b):
    xf = x.astype(jnp.float32)
    xc = xf - jnp.mean(xf, axis=-1, keepdims=True)
    var = jnp.mean(xc * xc, axis=-1, keepdims=True)
    return (xc * lax.rsqrt(var + EPS) * g.astype(jnp.float32) + b.astype(jnp.float32)).astype(x.dtype)


def causal_dwconv(u, buf, w, b):
    k = w.shape[0]
    full = jnp.concatenate([buf.astype(u.dtype), u], axis=1)
    out = lax.conv_general_dilated(full, w.astype(u.dtype)[:, None, :], window_strides=(1,),
                                   padding='VALID', dimension_numbers=('NWC', 'WIO', 'NWC'),
                                   feature_group_count=u.shape[-1])
    return out + b.astype(u.dtype), full[:, full.shape[1] - (k - 1):]


def segsum(a):
    q = a.shape[-1]
    cs = jnp.cumsum(a, axis=-1)
    diff = cs[..., :, None] - cs[..., None, :]
    return jnp.where(jnp.tril(jnp.ones((q, q), dtype=bool)), diff, -jnp.inf)


def ssd_scan(xs, dt, a_neg, bm, cm, h0):
    bsz, L, H, P = xs.shape
    G, N = bm.shape[2], bm.shape[3]
    R = H // G
    q = min(CHUNK, L)
    n_chunks = -(-L // q)
    pad = n_chunks * q - L
    padw = lambda t: jnp.pad(t, [(0, 0), (0, pad)] + [(0, 0)] * (t.ndim - 2))
    x = padw(xs).reshape(bsz, n_chunks, q, G, R, P)
    d = padw(dt).reshape(bsz, n_chunks, q, G, R)
    bb = padw(bm).reshape(bsz, n_chunks, q, G, N)
    cc = padw(cm).reshape(bsz, n_chunks, q, G, N)
    at = jnp.moveaxis(d * a_neg.reshape(G, R), 2, -1)
    acs = jnp.cumsum(at, axis=-1)
    lmat = jnp.exp(segsum(at))
    xdt = x * d[..., None]
    cb = jnp.einsum('bclgn,bcsgn->bcgls', cc, bb)
    y_diag = jnp.einsum('bcgls,bcgrls,bcsgrp->bclgrp', cb, lmat, xdt)
    decay_states = jnp.exp(acs[..., -1:] - acs)
    states = jnp.einsum('bclgn,bcgrl,bclgrp->bcgrpn', bb, decay_states, xdt)
    chunk_decay = jnp.exp(acs[..., -1])

    def step(h, inp):
        dec, st = inp
        return dec[..., None, None] * h + st, h

    h_last, h_prev = lax.scan(step, h0.reshape(bsz, G, R, P, N),
                              (jnp.moveaxis(chunk_decay, 1, 0), jnp.moveaxis(states, 1, 0)))
    h_prev = jnp.moveaxis(h_prev, 0, 1)
    y_off = jnp.einsum('bclgn,bcgrpn,bcgrl->bclgrp', cc, h_prev, jnp.exp(acs))
    y = (y_diag + y_off).reshape(bsz, n_chunks * q, H, P)[:, :L]
    return y, h_last.reshape(bsz, H, P, N)


def mixer_block(u, cbuf, mbuf, sh, w_in, conf_dw_w, conf_dw_b, conf_ln_g, conf_ln_b,
                ssm_conv_w, ssm_conv_b, dt_bias, a_log, d_skip, ssm_norm_g, w_out):
    bsz, L, _ = u.shape
    proj = u @ w_in
    o1 = 2 * C_CONF
    o2 = o1 + D_INNER
    o3 = o2 + CONV_DIM
    glu = proj[..., :C_CONF] * jax.nn.sigmoid(proj[..., C_CONF:o1])
    cconv, new_cbuf = causal_dwconv(glu, cbuf, conf_dw_w, conf_dw_b)
    a_out = jax.nn.silu(layernorm(cconv, conf_ln_g, conf_ln_b))
    z = proj[..., o1:o2]
    xbc, new_mbuf = causal_dwconv(proj[..., o2:o3], mbuf, ssm_conv_w, ssm_conv_b)
    xbc = jax.nn.silu(xbc).astype(jnp.float32)
    gn = N_SSM_GROUPS * D_STATE
    xs = xbc[..., :D_INNER].reshape(bsz, L, N_SSM_HEADS, SSM_HEAD_DIM)
    bm = xbc[..., D_INNER:D_INNER + gn].reshape(bsz, L, N_SSM_GROUPS, D_STATE)
    cm = xbc[..., D_INNER + gn:].reshape(bsz, L, N_SSM_GROUPS, D_STATE)
    dt = jax.nn.softplus(proj[..., o3:].astype(jnp.float32) + dt_bias.astype(jnp.float32))
    a_neg = -jnp.exp(a_log.astype(jnp.float32))
    y, new_sh = ssd_scan(xs, dt, a_neg, bm, cm, sh.astype(jnp.float32))
    y = y + d_skip.astype(jnp.float32)[:, None] * xs
    y = y.reshape(bsz, L, D_INNER).astype(u.dtype) * jax.nn.silu(z)
    m_out = rmsnorm(y.reshape(bsz, L, N_SSM_GROUPS, D_INNER // N_SSM_GROUPS),
                    ssm_norm_g.reshape(N_SSM_GROUPS, D_INNER // N_SSM_GROUPS)).reshape(bsz, L, D_INNER)
    out = jnp.concatenate([a_out, m_out], axis=-1) @ w_out
    return out, new_cbuf, new_mbuf, new_sh.astype(sh.dtype)


def hier_moe(u, w_rg, b_rg, w_re, b_re, w_gate_e, w_up_e, w_down_e):
    shp = u.shape
    t = u.reshape(-1, shp[-1])
    p_grp = jax.nn.softmax((t @ w_rg + b_rg).astype(jnp.float32), axis=-1)
    g_val, g_idx = lax.top_k(p_grp, 1)
    le = (t @ w_re + b_re).astype(jnp.float32).reshape(-1, N_EXPERT_GROUPS, EXPERTS_PER_GROUP)
    sel = jax.nn.one_hot(g_idx[:, 0], N_EXPERT_GROUPS, dtype=jnp.float32)
    le_sel = jnp.einsum('tge,tg->te', le, sel)
    e_val, e_idx = lax.top_k(jax.nn.softmax(le_sel, axis=-1), TOP_K)
    wts = g_val * e_val / jnp.sum(e_val, axis=-1, keepdims=True)
    eid = g_idx * EXPERTS_PER_GROUP + e_idx
    combine = jnp.sum(jax.nn.one_hot(eid, N_EXPERTS, dtype=jnp.float32) * wts[..., None], axis=1).astype(u.dtype)
    out = jnp.zeros_like(t)
    for e in range(N_EXPERTS):
        hdn = jax.nn.silu(t @ w_gate_e[e]) * (t @ w_up_e[e])
        out = out + combine[:, e:e + 1] * (hdn @ w_down_e[e])
    return out.reshape(shp)


def layer(h, p, cbuf, mbuf, sh, g_mix, w_in, conf_dw_w, conf_dw_b, conf_ln_g, conf_ln_b,
          ssm_conv_w, ssm_conv_b, dt_bias, a_log, d_skip, ssm_norm_g, w_out, g_ffn,
          w_rg, b_rg, w_re, b_re, w_gate_e, w_up_e, w_down_e, g_ple, w_ple_gate, w_ple):
    mix, cbuf, mbuf, sh = mixer_block(rmsnorm(h, g_mix), cbuf, mbuf, sh, w_in, conf_dw_w, conf_dw_b,
                                      conf_ln_g, conf_ln_b, ssm_conv_w, ssm_conv_b, dt_bias, a_log,
                                      d_skip, ssm_norm_g, w_out)
    h = h + mix
    h = h + hier_moe(rmsnorm(h, g_ffn), w_rg, b_rg, w_re, b_re, w_gate_e, w_up_e, w_down_e)
    h = h + (p.astype(h.dtype) @ w_ple) * jax.nn.sigmoid(rmsnorm(h, g_ple) @ w_ple_gate)
    return h, cbuf, mbuf, sh


def setup_inputs(seed: int = 0) -> dict:
    key = jax.random.key(seed)
    ks = iter(jax.random.split(key, 48))
    f32 = jnp.float32

    def nrm(shape, scale):
        return jax.random.normal(next(ks), shape, f32) * scale

    def gain(shape):
        return 1.0 + 0.05 * jax.random.normal(next(ks), shape, f32)

    dt0 = jnp.exp(jax.random.uniform(next(ks), (DEPTH, N_SSM_HEADS), f32,
                                     minval=math.log(1e-3), maxval=math.log(1e-1)))
    dt_bias = dt0 + jnp.log(-jnp.expm1(-dt0))
    a_log = jnp.log(jax.random.uniform(next(ks), (DEPTH, N_SSM_HEADS), f32, minval=1.0, maxval=16.0))
    return {
        'x_prompt': nrm((BATCH, SEQ, D_MODEL), 1.0),
        'x_sample': nrm((DEC_BATCH, DEC_SEQ, D_MODEL), 1.0),
        'p_prompt': nrm((DEPTH, BATCH, SEQ, PLE_DIM), 1.0),
        'p_sample': nrm((DEPTH, DEC_BATCH, DEC_SEQ, PLE_DIM), 1.0),
        'state_conf_conv': nrm((DEPTH, DEC_BATCH, K_CONF - 1, C_CONF), 0.5),
        'state_ssm_conv': nrm((DEPTH, DEC_BATCH, K_SSM - 1, CONV_DIM), 1.0),
        'state_ssm': nrm((DEPTH, DEC_BATCH, N_SSM_HEADS, SSM_HEAD_DIM, D_STATE), 0.3),
        'g_mix': gain((DEPTH, D_MODEL)),
        'w_in': nrm((DEPTH, D_MODEL, IN_COLS), D_MODEL ** -0.5),
        'conf_dw_w': nrm((DEPTH, K_CONF, C_CONF), K_CONF ** -0.5),
        'conf_dw_b': nrm((DEPTH, C_CONF), 0.02),
        'conf_ln_g': gain((DEPTH, C_CONF)),
        'conf_ln_b': nrm((DEPTH, C_CONF), 0.02),
        'ssm_conv_w': nrm((DEPTH, K_SSM, CONV_DIM), K_SSM ** -0.5),
        'ssm_conv_b': nrm((DEPTH, CONV_DIM), 0.02),
        'dt_bias': dt_bias,
        'a_log': a_log,
        'd_skip': gain((DEPTH, N_SSM_HEADS)),
        'ssm_norm_g': gain((DEPTH, D_INNER)),
        'w_out': nrm((DEPTH, D_MIX, D_MODEL), D_MIX ** -0.5),
        'g_ffn': gain((DEPTH, D_MODEL)),
        'w_rg': nrm((DEPTH, D_MODEL, N_EXPERT_GROUPS), D_MODEL ** -0.5),
        'b_rg': nrm((DEPTH, N_EXPERT_GROUPS), 0.01),
        'w_re': nrm((DEPTH, D_MODEL, N_EXPERTS), D_MODEL ** -0.5),
        'b_re': nrm((DEPTH, N_EXPERTS), 0.01),
        'w_gate_e': nrm((DEPTH, N_EXPERTS, D_MODEL, D_FF_EXPERT), D_MODEL ** -0.5),
        'w_up_e': nrm((DEPTH, N_EXPERTS, D_MODEL, D_FF_EXPERT), D_MODEL ** -0.5),
        'w_down_e': nrm((DEPTH, N_EXPERTS, D_FF_EXPERT, D_MODEL), D_FF_EXPERT ** -0.5),
        'g_ple': gain((DEPTH, D_MODEL)),
        'w_ple_gate': nrm((DEPTH, D_MODEL, D_MODEL), D_MODEL ** -0.5),
        'w_ple': nrm((DEPTH, PLE_DIM, D_MODEL), PLE_DIM ** -0.5),
        'g_final': gain((D_MODEL,)),
    }


def reference(x_prompt, x_sample, p_prompt, p_sample, state_conf_conv, state_ssm_conv, state_ssm,
              g_mix, w_in, conf_dw_w, conf_dw_b, conf_ln_g, conf_ln_b, ssm_conv_w, ssm_conv_b,
              dt_bias, a_log, d_skip, ssm_norm_g, w_out, g_ffn, w_rg, b_rg, w_re, b_re,
              w_gate_e, w_up_e, w_down_e, g_ple, w_ple_gate, w_ple, g_final):
    hp, hs = x_prompt, x_sample
    bp = x_prompt.shape[0]
    cp_l, mp_l, sp_l, cs_l, ms_l, ss_l = [], [], [], [], [], []
    for i in range(DEPTH):
        lw = (g_mix[i], w_in[i], conf_dw_w[i], conf_dw_b[i], conf_ln_g[i], conf_ln_b[i],
              ssm_conv_w[i], ssm_conv_b[i], dt_bias[i], a_log[i], d_skip[i], ssm_norm_g[i], w_out[i],
              g_ffn[i], w_rg[i], b_rg[i], w_re[i], b_re[i], w_gate_e[i], w_up_e[i], w_down_e[i],
              g_ple[i], w_ple_gate[i], w_ple[i])
        hp, c, m, s = layer(hp, p_prompt[i],
                            jnp.zeros((bp, K_CONF - 1, C_CONF), hp.dtype),
                            jnp.zeros((bp, K_SSM - 1, CONV_DIM), hp.dtype),
                            jnp.zeros((bp, N_SSM_HEADS, SSM_HEAD_DIM, D_STATE), jnp.float32), *lw)
        cp_l.append(c)
        mp_l.append(m)
        sp_l.append(s)
        hs, c, m, s = layer(hs, p_sample[i], state_conf_conv[i], state_ssm_conv[i], state_ssm[i], *lw)
        cs_l.append(c)
        ms_l.append(m)
        ss_l.append(s)
    y_prompt = rmsnorm(hp, g_final)
    y_sample = rmsnorm(hs, g_final)
    return (y_prompt, y_sample, jnp.stack(cp_l), jnp.stack(mp_l), jnp.stack(sp_l),
            jnp.stack(cs_l), jnp.stack(ms_l), jnp.stack(ss_l))
```

```python
import functools

import jax
import jax.numpy as jnp
from jax import lax
from jax.experimental import pallas as pl
from jax.experimental.pallas import tpu as pltpu

F32 = jnp.float32
BF16 = jnp.bfloat16

D_MODEL = 1024
C_CONF = 1024
K_CONF = 31
D_INNER = 1024
HEAD_DIM = 64
N_HEADS = 16
N_GROUPS = 4
HEADS_PER_GROUP = N_HEADS // N_GROUPS
GROUP_W = HEADS_PER_GROUP * HEAD_DIM
D_STATE = 128
K_SSM = 4
CHUNK = 128
CONV_DIM = D_INNER + 2 * N_GROUPS * D_STATE
IN_COLS = 2 * C_CONF + D_INNER + CONV_DIM + N_HEADS
O_GATE = C_CONF
O_Z = 2 * C_CONF
O_XBC = O_Z + D_INNER
O_DT = O_XBC + CONV_DIM
N_EXPERT_GROUPS = 4
EXPERTS_PER_GROUP = 4
N_EXPERTS = 16
D_FF = 512
PLE_DIM = 256
EPS = 1e-6

LANES = 128
SUBLANES = 8
N_LANE_TILES = C_CONF // LANES
CONF_PAD = 32
SSM_PAD = 8
VMEM_LIMIT = 56 * 1024 * 1024


def _dot(a, b):
    return jnp.dot(a, b, preferred_element_type=F32)


def _dot_nt(a, b):
    return lax.dot_general(a, b, (((1,), (1,)), ((), ())), preferred_element_type=F32)


def _dot_tn(a, b):
    return lax.dot_general(a, b, (((0,), (0,)), ((), ())), preferred_element_type=F32)


def _split3(v):
    hi = v.astype(BF16)
    r = v - hi.astype(F32)
    mid = r.astype(BF16)
    lo = (r - mid.astype(F32)).astype(BF16)
    return hi, mid, lo


def _rms(x, g):
    return x * lax.rsqrt(jnp.mean(x * x, axis=-1, keepdims=True) + EPS) * g


def _sigmoid(x):
    return jax.nn.sigmoid(x)


def _silu(x):
    return x * jax.nn.sigmoid(x)


def _softplus(x):
    return jax.nn.softplus(x)


def _mixer_prompt_kernel(x_ref, gmix_ref, win_ref, wdtT_ref, cw_ref, cb_ref, lng_ref, lnb_ref,
                         sw_ref, sb_ref, dtb_ref, dtbT_ref, alog_ref, alogT_ref, dskip_ref, sng_ref,
                         wout_ref,
                         h1_ref, ncc_ref, nsc_ref, nss_ref,
                         cscr, cout, mscr, st_scr, *, tl):
    t = pl.program_id(1)
    nt = pl.num_programs(1)

    @pl.when(t == 0)
    def _():
        cscr[:, 0:CONF_PAD, :] = jnp.zeros((N_LANE_TILES, CONF_PAD, LANES), F32)
        mscr[0:SSM_PAD, :] = jnp.zeros((SSM_PAD, CONV_DIM), F32)
        st_scr[...] = jnp.zeros(st_scr.shape, F32)

    x = x_ref[0]
    u = _rms(x, gmix_ref[...]).astype(BF16)

    glu = _dot(u, win_ref[:, 0:O_GATE]) * _sigmoid(_dot(u, win_ref[:, O_GATE:O_Z]))
    for lc in range(N_LANE_TILES):
        cscr[lc, CONF_PAD:CONF_PAD + tl, :] = glu[:, lc * LANES:(lc + 1) * LANES]

    rc = 64

    def conv_lane_tile(lc, carry):
        bias = cb_ref[lc]
        for r0 in range(0, tl, rc):
            acc = jnp.broadcast_to(bias, (rc, LANES))
            for k in range(K_CONF):
                acc = acc + cw_ref[lc, pl.ds(k, 1), :] * cscr[lc, pl.ds(r0 + k + CONF_PAD - (K_CONF - 1), rc), :]
            cout[lc, pl.ds(r0, rc), :] = acc
        return carry

    lax.fori_loop(0, N_LANE_TILES, conv_lane_tile, 0)

    @pl.when(t == nt - 1)
    def _():
        for lc in range(N_LANE_TILES):
            ncc_ref[0, :, lc * LANES:(lc + 1) * LANES] = cscr[lc, pl.ds(CONF_PAD + tl - (K_CONF - 1), K_CONF - 1), :]

    for lc in range(N_LANE_TILES):
        cscr[lc, 0:CONF_PAD, :] = cscr[lc, tl:tl + CONF_PAD, :]

    cc = [cout[lc] for lc in range(N_LANE_TILES)]
    tot = cc[0]
    for lc in range(1, N_LANE_TILES):
        tot = tot + cc[lc]
    mean = jnp.sum(tot, axis=-1, keepdims=True) * (1.0 / C_CONF)
    xc = [c - mean for c in cc]
    sq = xc[0] * xc[0]
    for lc in range(1, N_LANE_TILES):
        sq = sq + xc[lc] * xc[lc]
    rstd = lax.rsqrt(jnp.sum(sq, axis=-1, keepdims=True) * (1.0 / C_CONF) + EPS)
    a_out = jnp.concatenate(
        [_silu(xc[lc] * rstd * lng_ref[:, lc * LANES:(lc + 1) * LANES] + lnb_ref[:, lc * LANES:(lc + 1) * LANES])
         for lc in range(N_LANE_TILES)], axis=-1).astype(BF16)

    z = _dot(u, win_ref[:, O_Z:O_XBC])
    mscr[SSM_PAD:SSM_PAD + tl, :] = _dot(u, win_ref[:, O_XBC:O_DT])

    @pl.when(t == nt - 1)
    def _():
        nsc_ref[0] = mscr[pl.ds(SSM_PAD + tl - (K_SSM - 1), K_SSM - 1), :]

    acc = jnp.broadcast_to(sb_ref[...], (tl, CONV_DIM))
    for k in range(K_SSM):
        acc = acc + sw_ref[pl.ds(k, 1), :] * mscr[pl.ds(SSM_PAD - (K_SSM - 1) + k, tl), :]
    xbc = _silu(acc)
    mscr[0:SSM_PAD, :] = mscr[tl:tl + SSM_PAD, :]

    xs = xbc[:, 0:D_INNER]
    bm = xbc[:, D_INNER:D_INNER + N_GROUPS * D_STATE]
    cm = xbc[:, D_INNER + N_GROUPS * D_STATE:]

    dt = _softplus(_dot(u, win_ref[:, O_DT:IN_COLS]) + dtb_ref[...])
    dtT = _softplus(_dot_nt(wdtT_ref[...], u) + dtbT_ref[...])
    a = dt * (-jnp.exp(alog_ref[...]))
    aT = dtT * (-jnp.exp(alogT_ref[...]))

    row = lax.broadcasted_iota(jnp.int32, (CHUNK, CHUNK), 0)
    col = lax.broadcasted_iota(jnp.int32, (CHUNK, CHUNK), 1)
    lower = row >= col
    tri = lower.astype(BF16)
    triT = (row <= col).astype(BF16)

    y_chunks = []
    for c in range(tl // CHUNK):
        r0 = c * CHUNK
        a_c = a[r0:r0 + CHUNK]
        aT_c = aT[:, r0:r0 + CHUNK]
        ah, am, al = _split3(a_c)
        cs = _dot(tri, ah) + _dot(tri, am) + _dot(tri, al)
        th, tm, tlo = _split3(aT_c)
        csT = _dot(th, triT) + _dot(tm, triT) + _dot(tlo, triT)
        cs_last = cs[CHUNK - 1:CHUNK, :]
        ecs = jnp.exp(cs)
        dst = jnp.exp(cs_last - cs)
        cdec = jnp.exp(cs_last)
        dt_c = dt[r0:r0 + CHUNK]
        xs_c = xs[r0:r0 + CHUNK]
        y_heads = []
        for g in range(N_GROUPS):
            cg = cm[r0:r0 + CHUNK, g * D_STATE:(g + 1) * D_STATE].astype(BF16)
            bg = bm[r0:r0 + CHUNK, g * D_STATE:(g + 1) * D_STATE].astype(BF16)
            cb = _dot_nt(cg, bg)
            y_off = _dot(cg, st_scr[g].astype(BF16))
            xdd = []
            dec_row = []
            for hh in range(HEADS_PER_GROUP):
                h = g * HEADS_PER_GROUP + hh
                xdt = xs_c[:, h * HEAD_DIM:(h + 1) * HEAD_DIM] * dt_c[:, h:h + 1]
                lmat = jnp.where(lower, jnp.exp(cs[:, h:h + 1] - csT[h:h + 1, :]), 0.0)
                y_diag = _dot((cb * lmat).astype(BF16), xdt.astype(BF16))
                y_heads.append(y_diag + y_off[:, hh * HEAD_DIM:(hh + 1) * HEAD_DIM] * ecs[:, h:h + 1])
                xdd.append((xdt * dst[:, h:h + 1]).astype(BF16))
                dec_row.append(jnp.broadcast_to(cdec[:, h:h + 1], (1, HEAD_DIM)))
            contrib = _dot_tn(bg, jnp.concatenate(xdd, axis=-1))
            st_scr[g] = st_scr[g] * jnp.concatenate(dec_row, axis=-1) + contrib
        y_chunks.append(jnp.concatenate(y_heads, axis=-1))
    y = y_chunks[0] if len(y_chunks) == 1 else jnp.concatenate(y_chunks, axis=0)
    y = (y + dskip_ref[...] * xs) * _silu(z)
    m_parts = []
    for g in range(N_GROUPS):
        yg = y[:, g * GROUP_W:(g + 1) * GROUP_W]
        m_parts.append(_rms(yg, sng_ref[:, g * GROUP_W:(g + 1) * GROUP_W]))
    m_out = jnp.concatenate(m_parts, axis=-1).astype(BF16)

    @pl.when(t == nt - 1)
    def _():
        for g in range(N_GROUPS):
            nss_ref[0, g * HEADS_PER_GROUP:(g + 1) * HEADS_PER_GROUP] = (
                st_scr[g].T.reshape(HEADS_PER_GROUP, HEAD_DIM, D_STATE))

    h1_ref[0] = x + _dot(a_out, wout_ref[0:C_CONF, :]) + _dot(m_out, wout_ref[C_CONF:, :])


def _const_spec(shape):
    nd = len(shape)
    return pl.BlockSpec(shape, lambda *_: (0,) * nd, pipeline_mode=pl.Buffered(1))


def _mixer_prompt(x, w, tl=256):
    bsz, seq, _ = x.shape
    kern = functools.partial(_mixer_prompt_kernel, tl=tl)
    consts = [w['g_mix'], w['w_in'], w['w_dtT'], w['conf_w'], w['conf_b'], w['ln_g'], w['ln_b'],
              w['ssm_w'], w['ssm_b'], w['dt_bias'], w['dt_biasT'], w['a_log'], w['a_logT'], w['d_skip'],
              w['ssm_norm_g'], w['w_out']]
    return pl.pallas_call(
        kern,
        grid=(bsz, seq // tl),
        in_specs=[pl.BlockSpec((1, tl, D_MODEL), lambda b, t: (b, t, 0))] + [_const_spec(c.shape) for c in consts],
        out_specs=[
            pl.BlockSpec((1, tl, D_MODEL), lambda b, t: (b, t, 0)),
            pl.BlockSpec((1, K_CONF - 1, C_CONF), lambda b, t: (b, 0, 0)),
            pl.BlockSpec((1, K_SSM - 1, CONV_DIM), lambda b, t: (b, 0, 0)),
            pl.BlockSpec((1, N_HEADS, HEAD_DIM, D_STATE), lambda b, t: (b, 0, 0, 0)),
        ],
        out_shape=[
            jax.ShapeDtypeStruct((bsz, seq, D_MODEL), F32),
            jax.ShapeDtypeStruct((bsz, K_CONF - 1, C_CONF), F32),
            jax.ShapeDtypeStruct((bsz, K_SSM - 1, CONV_DIM), F32),
            jax.ShapeDtypeStruct((bsz, N_HEADS, HEAD_DIM, D_STATE), F32),
        ],
        scratch_shapes=[
            pltpu.VMEM((N_LANE_TILES, CONF_PAD + tl, LANES), F32),
            pltpu.VMEM((N_LANE_TILES, tl, LANES), F32),
            pltpu.VMEM((SSM_PAD + tl, CONV_DIM), F32),
            pltpu.VMEM((N_GROUPS, D_STATE, GROUP_W), F32),
        ],
        compiler_params=pltpu.CompilerParams(
            dimension_semantics=("arbitrary", "arbitrary"), vmem_limit_bytes=VMEM_LIMIT),
        name="mixer_prompt",
    )(x, *consts)


def _mixer_sample_kernel(x_ref, cst_ref, mst_ref, sst_ref, gmix_ref, win_ref, cw_ref, cb_ref, lng_ref, lnb_ref,
                         sw_ref, sb_ref, dtb_ref, alog_ref, dskip_ref, sng_ref, wout_ref, hexp_ref,
                         h1_ref, ncc_ref, nsc_ref, nss_ref,
                         proj_scr, mix_scr, *, bb):
    i = pl.program_id(0)
    n = pl.num_programs(0)

    @pl.when(i == 0)
    def _():
        u = _rms(x_ref[...], gmix_ref[...]).astype(BF16)
        proj_scr[...] = _dot(u, win_ref[...])

    r0 = pl.multiple_of(i * bb, bb)
    proj = proj_scr[pl.ds(r0, bb), :]

    glu = proj[:, 0:O_GATE] * _sigmoid(proj[:, O_GATE:O_Z])
    acc = cb_ref[...] + cw_ref[pl.ds(K_CONF - 1, 1), :] * glu
    for k in range(K_CONF - 1):
        row_k = cst_ref[:, k, :]
        acc = acc + cw_ref[pl.ds(k, 1), :] * row_k
        if k >= 1:
            ncc_ref[:, k - 1, :] = row_k
    ncc_ref[:, K_CONF - 2, :] = glu
    mean = jnp.mean(acc, axis=-1, keepdims=True)
    xc = acc - mean
    rstd = lax.rsqrt(jnp.mean(xc * xc, axis=-1, keepdims=True) + EPS)
    a_out = _silu(xc * rstd * lng_ref[...] + lnb_ref[...])

    z = proj[:, O_Z:O_XBC]
    xbc_raw = proj[:, O_XBC:O_DT]
    acc = sb_ref[...] + sw_ref[pl.ds(K_SSM - 1, 1), :] * xbc_raw
    for k in range(K_SSM - 1):
        row_k = mst_ref[:, k, :]
        acc = acc + sw_ref[pl.ds(k, 1), :] * row_k
        if k >= 1:
            nsc_ref[:, k - 1, :] = row_k
    nsc_ref[:, K_SSM - 2, :] = xbc_raw
    xbc = _silu(acc)
    xs = xbc[:, 0:D_INNER]
    bm = xbc[:, D_INNER:D_INNER + N_GROUPS * D_STATE]
    cm = xbc[:, D_INNER + N_GROUPS * D_STATE:]
    dt = _softplus(proj[:, O_DT:IN_COLS] + dtb_ref[...])
    dec = jnp.exp(dt * (-jnp.exp(alog_ref[...])))
    hexp = hexp_ref[...]
    d_h, d_m, d_l = _split3(dec)
    dec_e = _dot(d_h, hexp) + _dot(d_m, hexp) + _dot(d_l, hexp)
    t_h, t_m, t_l = _split3(dt)
    dt_e = _dot(t_h, hexp) + _dot(t_m, hexp) + _dot(t_l, hexp)
    xdt = xs * dt_e

    lane = lax.broadcasted_iota(jnp.int32, (SUBLANES, D_INNER), 1)
    sub = lax.broadcasted_iota(jnp.int32, (SUBLANES, D_INNER), 0)
    gmask = (lane // GROUP_W) == sub
    ones_row = (lax.broadcasted_iota(jnp.int32, (SUBLANES, D_STATE), 0) < 3).astype(BF16)

    y_rows = []
    for b in range(bb):
        xrow = jnp.where(gmask, jnp.broadcast_to(xdt[b:b + 1, :], (SUBLANES, D_INNER)), 0.0)
        x_h, x_m, x_l = [v.astype(F32) for v in _split3(xrow)]
        bmat = jnp.concatenate([bm[b:b + 1, g * D_STATE:(g + 1) * D_STATE] for g in range(N_GROUPS)]
                               + [jnp.zeros((SUBLANES - N_GROUPS, D_STATE), F32)], axis=0)
        b_h, b_m, b_l = [v.astype(F32) for v in _split3(bmat)]
        lhs = jnp.concatenate([x_h, x_h, x_m, x_h, x_m, x_l], axis=0).astype(BF16)
        rhs = jnp.concatenate([b_h, b_m, b_h, b_l, b_m, b_h], axis=0).astype(BF16)
        upd = _dot_tn(lhs, rhs)
        dbc = jnp.broadcast_to(dec_e[b:b + 1, :], (SUBLANES, D_INNER))
        q_h = dbc.astype(BF16).astype(F32)
        q_m = (dbc - q_h).astype(BF16).astype(F32)
        q_l = dbc - q_h - q_m
        dlhs = jnp.where(sub == 0, q_h, jnp.where(sub == 1, q_m, jnp.where(sub == 2, q_l, 0.0))).astype(BF16)
        dfull = _dot_tn(dlhs, ones_row)
        hnew = dfull * sst_ref[b].reshape(D_INNER, D_STATE) + upd
        nss_ref[b] = hnew.reshape(N_HEADS, HEAD_DIM, D_STATE)
        cmat = jnp.concatenate([cm[b:b + 1, g * D_STATE:(g + 1) * D_STATE] for g in range(N_GROUPS)]
                               + [jnp.zeros((SUBLANES - N_GROUPS, D_STATE), F32)], axis=0)
        yg = _dot_nt(cmat.astype(BF16), hnew.astype(BF16))
        y_rows.append(jnp.sum(jnp.where(gmask, yg, 0.0), axis=0, keepdims=True))
    y = jnp.concatenate(y_rows, axis=0)
    y = (y + dskip_ref[...] * xs) * _silu(z)
    m_parts = []
    for g in range(N_GROUPS):
        m_parts.append(_rms(y[:, g * GROUP_W:(g + 1) * GROUP_W], sng_ref[:, g * GROUP_W:(g + 1) * GROUP_W]))
    mix_scr[pl.ds(r0, bb), :] = jnp.concatenate([a_out] + m_parts, axis=-1)

    @pl.when(i == n - 1)
    def _():
        h1_ref[...] = x_ref[...] + _dot(mix_scr[...].astype(BF16), wout_ref[...])


def _mixer_sample(x, cst, mst, sst, w, bb=8):
    nb = x.shape[0]
    kern = functools.partial(_mixer_sample_kernel, bb=bb)
    consts = [w['g_mix'], w['w_in'], w['conf_w2'], w['conf_b2'], w['ln_g'], w['ln_b'],
              w['ssm_w'], w['ssm_b'], w['dt_bias'], w['a_log'], w['d_skip'], w['ssm_norm_g'], w['w_out'],
              w['head_expand']]
    return pl.pallas_call(
        kern,
        grid=(nb // bb,),
        in_specs=[
            _const_spec((nb, D_MODEL)),
            pl.BlockSpec((bb, K_CONF - 1, C_CONF), lambda i: (i, 0, 0)),
            pl.BlockSpec((bb, K_SSM - 1, CONV_DIM), lambda i: (i, 0, 0)),
            pl.BlockSpec((bb, N_HEADS, HEAD_DIM, D_STATE), lambda i: (i, 0, 0, 0)),
        ] + [_const_spec(c.shape) for c in consts],
        out_specs=[
            pl.BlockSpec((nb, D_MODEL), lambda i: (0, 0)),
            pl.BlockSpec((bb, K_CONF - 1, C_CONF), lambda i: (i, 0, 0)),
            pl.BlockSpec((bb, K_SSM - 1, CONV_DIM), lambda i: (i, 0, 0)),
            pl.BlockSpec((bb, N_HEADS, HEAD_DIM, D_STATE), lambda i: (i, 0, 0, 0)),
        ],
        out_shape=[
            jax.ShapeDtypeStruct((nb, D_MODEL), F32),
            jax.ShapeDtypeStruct((nb, K_CONF - 1, C_CONF), F32),
            jax.ShapeDtypeStruct((nb, K_SSM - 1, CONV_DIM), F32),
            jax.ShapeDtypeStruct((nb, N_HEADS, HEAD_DIM, D_STATE), F32),
        ],
        scratch_shapes=[
            pltpu.VMEM((nb, IN_COLS), F32),
            pltpu.VMEM((nb, D_MODEL + D_INNER), F32),
        ],
        compiler_params=pltpu.CompilerParams(dimension_semantics=("arbitrary",), vmem_limit_bytes=VMEM_LIMIT),
        name="mixer_sample",
    )(x, cst, mst, sst, *consts)


def _first_argmax(v, width):
    lane = lax.broadcasted_iota(jnp.int32, v.shape, 1)
    m = jnp.max(v, axis=-1, keepdims=True)
    idx = jnp.min(jnp.where(v == m, lane, width), axis=-1, keepdims=True)
    return m, idx


def _ffn_kernel(h_ref, p_ref, gffn_ref, wr_ref, br_ref, wg_ref, wu_ref, wd_ref, gple_ref, wpg_ref, wp_ref,
                gfin_ref, y_ref, t_scr, comb_scr, acc_scr):
    e = pl.program_id(1)
    ne = pl.num_programs(1)

    @pl.when(e == 0)
    def _():
        tf = _rms(h_ref[...], gffn_ref[...])
        t_scr[...] = tf.astype(BF16)
        t_h, t_m, t_l = _split3(tf)
        w_h = wr_ref[0]
        w_m = wr_ref[1]
        logits = (_dot(t_h, w_h) + _dot(t_m, w_h) + _dot(t_h, w_m) + _dot(t_l, w_h) + _dot(t_m, w_m)) + br_ref[...]
        lg = logits[:, 0:N_EXPERT_GROUPS]
        le = logits[:, N_EXPERTS:2 * N_EXPERTS]
        eg = jnp.exp(lg - jnp.max(lg, axis=-1, keepdims=True))
        pg = eg / jnp.sum(eg, axis=-1, keepdims=True)
        g_val, g_idx = _first_argmax(pg, N_EXPERT_GROUPS)
        lane16 = lax.broadcasted_iota(jnp.int32, le.shape, 1)
        in_grp = (lane16 // EXPERTS_PER_GROUP) == g_idx
        neg = jnp.float32(-jnp.inf)
        le_m = jnp.where(in_grp, le, neg)
        ee = jnp.where(in_grp, jnp.exp(le - jnp.max(le_m, axis=-1, keepdims=True)), 0.0)
        pe = ee / jnp.sum(ee, axis=-1, keepdims=True)
        pe_m = jnp.where(in_grp, pe, -1.0)
        v1, i1 = _first_argmax(pe_m, N_EXPERTS)
        pe_m2 = jnp.where(lane16 == i1, -1.0, pe_m)
        v2, i2 = _first_argmax(pe_m2, N_EXPERTS)
        den = v1 + v2
        comb = jnp.where(lane16 == i1, g_val * v1 / den, 0.0) + jnp.where(lane16 == i2, g_val * v2 / den, 0.0)
        comb_scr[...] = comb
        acc_scr[...] = jnp.zeros(acc_scr.shape, F32)

    t = t_scr[...]
    hdn = _silu(_dot(t, wg_ref[0])) * _dot(t, wu_ref[0])
    out_e = _dot(hdn.astype(BF16), wd_ref[0])
    lane16 = lax.broadcasted_iota(jnp.int32, comb_scr.shape, 1)
    c_e = jnp.sum(jnp.where(lane16 == e, comb_scr[...], 0.0), axis=-1, keepdims=True)
    acc_scr[...] += c_e * out_e

    @pl.when(e == ne - 1)
    def _():
        h2 = h_ref[...] + acc_scr[...]
        gate = _sigmoid(_dot(_rms(h2, gple_ref[...]).astype(BF16), wpg_ref[...]))
        h3 = h2 + _dot(p_ref[...].astype(BF16), wp_ref[...]) * gate
        y_ref[...] = _rms(h3, gfin_ref[...])


def _ffn(h, p, w, tb):
    nt = h.shape[0]
    return pl.pallas_call(
        _ffn_kernel,
        grid=(nt // tb, N_EXPERTS),
        in_specs=[
            pl.BlockSpec((tb, D_MODEL), lambda i, e: (i, 0)),
            pl.BlockSpec((tb, PLE_DIM), lambda i, e: (i, 0)),
            _const_spec((1, D_MODEL)),
            _const_spec((2, D_MODEL, 2 * N_EXPERTS)),
            _const_spec((1, 2 * N_EXPERTS)),
            pl.BlockSpec((1, D_MODEL, D_FF), lambda i, e: (e, 0, 0)),
            pl.BlockSpec((1, D_MODEL, D_FF), lambda i, e: (e, 0, 0)),
            pl.BlockSpec((1, D_FF, D_MODEL), lambda i, e: (e, 0, 0)),
            _const_spec((1, D_MODEL)),
            _const_spec((D_MODEL, D_MODEL)),
            _const_spec((PLE_DIM, D_MODEL)),
            _const_spec((1, D_MODEL)),
        ],
        out_specs=pl.BlockSpec((tb, D_MODEL), lambda i, e: (i, 0)),
        out_shape=jax.ShapeDtypeStruct((nt, D_MODEL), F32),
        scratch_shapes=[
            pltpu.VMEM((tb, D_MODEL), BF16),
            pltpu.VMEM((tb, N_EXPERTS), F32),
            pltpu.VMEM((tb, D_MODEL), F32),
        ],
        compiler_params=pltpu.CompilerParams(
            dimension_semantics=("arbitrary", "arbitrary"), vmem_limit_bytes=VMEM_LIMIT),
        name="ffn",
    )(h, p, w['g_ffn'], w['w_router'], w['b_router'], w['w_gate_e'], w['w_up_e'], w['w_down_e'],
      w['g_ple'], w['w_ple_gate'], w['w_ple'], w['g_final'])


def _prep_weights(g_mix, w_in, conf_dw_w, conf_dw_b, conf_ln_g, conf_ln_b, ssm_conv_w, ssm_conv_b,
                  dt_bias, a_log, d_skip, ssm_norm_g, w_out, g_ffn, w_rg, b_rg, w_re, b_re,
                  w_gate_e, w_up_e, w_down_e, g_ple, w_ple_gate, w_ple, g_final):
    row = lambda v: v.reshape(1, -1)
    w_in_b = w_in.astype(BF16)
    conf_w_pad = jnp.concatenate([conf_dw_w, jnp.zeros((CONF_PAD - K_CONF, C_CONF), F32)], axis=0)
    w_router = jnp.concatenate(
        [w_rg, jnp.zeros((D_MODEL, N_EXPERTS - N_EXPERT_GROUPS), F32), w_re], axis=1)
    wr_hi = w_router.astype(BF16)
    wr_mid = (w_router - wr_hi.astype(F32)).astype(BF16)
    b_router = jnp.concatenate([b_rg, jnp.zeros((N_EXPERTS - N_EXPERT_GROUPS,), F32), b_re]).reshape(1, -1)
    head_of_lane = jnp.arange(D_INNER) // HEAD_DIM
    head_expand = (head_of_lane[None, :] == jnp.arange(N_HEADS)[:, None]).astype(BF16)
    return dict(
        g_mix=row(g_mix), w_in=w_in_b, w_dtT=w_in_b[:, O_DT:].T,
        conf_w=conf_w_pad.reshape(CONF_PAD, N_LANE_TILES, LANES).transpose(1, 0, 2),
        conf_b=conf_dw_b.reshape(N_LANE_TILES, 1, LANES),
        conf_w2=conf_dw_w, conf_b2=row(conf_dw_b),
        ln_g=row(conf_ln_g), ln_b=row(conf_ln_b), ssm_w=ssm_conv_w, ssm_b=row(ssm_conv_b),
        dt_bias=row(dt_bias), dt_biasT=dt_bias.reshape(-1, 1), a_log=row(a_log), a_logT=a_log.reshape(-1, 1),
        d_skip=row(jnp.repeat(d_skip, HEAD_DIM)), ssm_norm_g=row(ssm_norm_g), w_out=w_out.astype(BF16),
        head_expand=head_expand,
        g_ffn=row(g_ffn), w_router=jnp.stack([wr_hi, wr_mid]), b_router=b_router,
        w_gate_e=w_gate_e.astype(BF16), w_up_e=w_up_e.astype(BF16), w_down_e=w_down_e.astype(BF16),
        g_ple=row(g_ple), w_ple_gate=w_ple_gate.astype(BF16), w_ple=w_ple.astype(BF16), g_final=row(g_final),
    )


def kernel(x_prompt, x_sample, p_prompt, p_sample, state_conf_conv, state_ssm_conv, state_ssm, g_mix, w_in, conf_dw_w, conf_dw_b, conf_ln_g, conf_ln_b, ssm_conv_w, ssm_conv_b, dt_bias, a_log, d_skip, ssm_norm_g, w_out, g_ffn, w_rg, b_rg, w_re, b_re, w_gate_e, w_up_e, w_down_e, g_ple, w_ple_gate, w_ple, g_final):
    depth = g_mix.shape[0]
    bsz, seq, _ = x_prompt.shape
    nb = x_sample.shape[0]
    hp = x_prompt
    hs = x_sample.reshape(nb, D_MODEL)
    cp_l, mp_l, sp_l, cs_l, ms_l, ss_l = [], [], [], [], [], []
    for i in range(depth):
        w = _prep_weights(g_mix[i], w_in[i], conf_dw_w[i], conf_dw_b[i], conf_ln_g[i], conf_ln_b[i],
                          ssm_conv_w[i], ssm_conv_b[i], dt_bias[i], a_log[i], d_skip[i], ssm_norm_g[i], w_out[i],
                          g_ffn[i], w_rg[i], b_rg[i], w_re[i], b_re[i], w_gate_e[i], w_up_e[i], w_down_e[i],
                          g_ple[i], w_ple_gate[i], w_ple[i], g_final)
        h1p, c, m, s = _mixer_prompt(hp, w)
        cp_l.append(c)
        mp_l.append(m)
        sp_l.append(s)
        h1s, c, m, s = _mixer_sample(hs, state_conf_conv[i], state_ssm_conv[i], state_ssm[i], w)
        cs_l.append(c)
        ms_l.append(m)
        ss_l.append(s)
        assert depth == 1
        hp = _ffn(h1p.reshape(bsz * seq, D_MODEL), p_prompt[i].reshape(bsz * seq, PLE_DIM), w, tb=1024)
        hp = hp.reshape(bsz, seq, D_MODEL)
        hs = _ffn(h1s, p_sample[i].reshape(nb, PLE_DIM), w, tb=nb)
    return (hp, hs.reshape(nb, 1, D_MODEL), jnp.stack(cp_l), jnp.stack(mp_l), jnp.stack(sp_l),
            jnp.stack(cs_l), jnp.stack(ms_l), jnp.stack(ss_l))
```

```python
import functools

import jax
import jax.numpy as jnp
from jax import lax
from jax.experimental import pallas as pl
from jax.experimental.pallas import tpu as pltpu
from jax.experimental.pallas import tpu_sc as plsc

F32 = jnp.float32
BF16 = jnp.bfloat16

D_MODEL = 1024
C_CONF = 1024
K_CONF = 31
D_INNER = 1024
HEAD_DIM = 64
N_HEADS = 16
N_GROUPS = 4
HEADS_PER_GROUP = N_HEADS // N_GROUPS
GROUP_W = HEADS_PER_GROUP * HEAD_DIM
D_STATE = 128
K_SSM = 4
CHUNK = 128
CONV_DIM = D_INNER + 2 * N_GROUPS * D_STATE
IN_COLS = 2 * C_CONF + D_INNER + CONV_DIM + N_HEADS
O_GATE = C_CONF
O_Z = 2 * C_CONF
O_XBC = O_Z + D_INNER
O_DT = O_XBC + CONV_DIM
N_EXPERT_GROUPS = 4
EXPERTS_PER_GROUP = 4
N_EXPERTS = 16
D_FF = 512
PLE_DIM = 256
EPS = 1e-6

LANES = 128
SUBLANES = 8
N_LANE_TILES = C_CONF // LANES
CONF_PAD = 32
SSM_PAD = 8
VMEM_LIMIT = 56 * 1024 * 1024


def _dot(a, b):
    return jnp.dot(a, b, preferred_element_type=F32)


def _dot_nt(a, b):
    return lax.dot_general(a, b, (((1,), (1,)), ((), ())), preferred_element_type=F32)


def _dot_tn(a, b):
    return lax.dot_general(a, b, (((0,), (0,)), ((), ())), preferred_element_type=F32)


def _split3(v):
    hi = v.astype(BF16)
    r = v - hi.astype(F32)
    mid = r.astype(BF16)
    lo = (r - mid.astype(F32)).astype(BF16)
    return hi, mid, lo


def _rms(x, g):
    return x * lax.rsqrt(jnp.mean(x * x, axis=-1, keepdims=True) + EPS) * g


def _sigmoid(x):
    return jax.nn.sigmoid(x)


def _silu(x):
    return x * jax.nn.sigmoid(x)


def _softplus(x):
    return jax.nn.softplus(x)


def _mixer_prompt_kernel(x_ref, gmix_ref, win_ref, wdtT_ref, cw_ref, cb_ref, lng_ref, lnb_ref,
                         sw_ref, sb_ref, dtb_ref, dtbT_ref, alog_ref, alogT_ref, dskip_ref, sng_ref,
                         wout_ref,
                         h1_ref, ncc_ref, nsc_ref, nss_ref,
                         cscr, cout, mscr, st_scr, *, tl):
    t = pl.program_id(1)
    nt = pl.num_programs(1)

    @pl.when(t == 0)
    def _():
        cscr[:, 0:CONF_PAD, :] = jnp.zeros((N_LANE_TILES, CONF_PAD, LANES), F32)
        mscr[0:SSM_PAD, :] = jnp.zeros((SSM_PAD, CONV_DIM), F32)
        st_scr[...] = jnp.zeros(st_scr.shape, F32)

    x = x_ref[0]
    u = _rms(x, gmix_ref[...]).astype(BF16)

    glu = _dot(u, win_ref[:, 0:O_GATE]) * _sigmoid(_dot(u, win_ref[:, O_GATE:O_Z]))
    for lc in range(N_LANE_TILES):
        cscr[lc, CONF_PAD:CONF_PAD + tl, :] = glu[:, lc * LANES:(lc + 1) * LANES]

    rc = 64

    def conv_lane_tile(lc, carry):
        bias = cb_ref[lc]
        for r0 in range(0, tl, rc):
            acc = jnp.broadcast_to(bias, (rc, LANES))
            for k in range(K_CONF):
                acc = acc + cw_ref[lc, pl.ds(k, 1), :] * cscr[lc, pl.ds(r0 + k + CONF_PAD - (K_CONF - 1), rc), :]
            cout[lc, pl.ds(r0, rc), :] = acc
        return carry

    lax.fori_loop(0, N_LANE_TILES, conv_lane_tile, 0)

    @pl.when(t == nt - 1)
    def _():
        for lc in range(N_LANE_TILES):
            ncc_ref[0, :, lc * LANES:(lc + 1) * LANES] = cscr[lc, pl.ds(CONF_PAD + tl - (K_CONF - 1), K_CONF - 1), :]

    for lc in range(N_LANE_TILES):
        cscr[lc, 0:CONF_PAD, :] = cscr[lc, tl:tl + CONF_PAD, :]

    cc = [cout[lc] for lc in range(N_LANE_TILES)]
    tot = cc[0]
    for lc in range(1, N_LANE_TILES):
        tot = tot + cc[lc]
    mean = jnp.sum(tot, axis=-1, keepdims=True) * (1.0 / C_CONF)
    xc = [c - mean for c in cc]
    sq = xc[0] * xc[0]
    for lc in range(1, N_LANE_TILES):
        sq = sq + xc[lc] * xc[lc]
    rstd = lax.rsqrt(jnp.sum(sq, axis=-1, keepdims=True) * (1.0 / C_CONF) + EPS)
    a_out = jnp.concatenate(
        [_silu(xc[lc] * rstd * lng_ref[:, lc * LANES:(lc + 1) * LANES] + lnb_ref[:, lc * LANES:(lc + 1) * LANES])
         for lc in range(N_LANE_TILES)], axis=-1).astype(BF16)

    z = _dot(u, win_ref[:, O_Z:O_XBC])
    mscr[SSM_PAD:SSM_PAD + tl, :] = _dot(u, win_ref[:, O_XBC:O_DT])

    @pl.when(t == nt - 1)
    def _():
        nsc_ref[0] = mscr[pl.ds(SSM_PAD + tl - (K_SSM - 1), K_SSM - 1), :]

    acc = jnp.broadcast_to(sb_ref[...], (tl, CONV_DIM))
    for k in range(K_SSM):
        acc = acc + sw_ref[pl.ds(k, 1), :] * mscr[pl.ds(SSM_PAD - (K_SSM - 1) + k, tl), :]
    xbc = _silu(acc)
    mscr[0:SSM_PAD, :] = mscr[tl:tl + SSM_PAD, :]

    xs = xbc[:, 0:D_INNER]
    bm = xbc[:, D_INNER:D_INNER + N_GROUPS * D_STATE]
    cm = xbc[:, D_INNER + N_GROUPS * D_STATE:]

    dt = _softplus(_dot(u, win_ref[:, O_DT:IN_COLS]) + dtb_ref[...])
    dtT = _softplus(_dot_nt(wdtT_ref[...], u) + dtbT_ref[...])
    a = dt * (-jnp.exp(alog_ref[...]))
    aT = dtT * (-jnp.exp(alogT_ref[...]))

    row = lax.broadcasted_iota(jnp.int32, (CHUNK, CHUNK), 0)
    col = lax.broadcasted_iota(jnp.int32, (CHUNK, CHUNK), 1)
    lower = row >= col
    tri = lower.astype(BF16)
    triT = (row <= col).astype(BF16)

    y_chunks = []
    for c in range(tl // CHUNK):
        r0 = c * CHUNK
        a_c = a[r0:r0 + CHUNK]
        aT_c = aT[:, r0:r0 + CHUNK]
        ah, am, al = _split3(a_c)
        cs = _dot(tri, ah) + _dot(tri, am) + _dot(tri, al)
        th, tm, tlo = _split3(aT_c)
        csT = _dot(th, triT) + _dot(tm, triT) + _dot(tlo, triT)
        cs_last = cs[CHUNK - 1:CHUNK, :]
        ecs = jnp.exp(cs)
        dst = jnp.exp(cs_last - cs)
        cdec = jnp.exp(cs_last)
        dt_c = dt[r0:r0 + CHUNK]
        xs_c = xs[r0:r0 + CHUNK]
        y_heads = []
        for g in range(N_GROUPS):
            cg = cm[r0:r0 + CHUNK, g * D_STATE:(g + 1) * D_STATE].astype(BF16)
            bg = bm[r0:r0 + CHUNK, g * D_STATE:(g + 1) * D_STATE].astype(BF16)
            cb = _dot_nt(cg, bg)
            y_off = _dot(cg, st_scr[g].astype(BF16))
            xdd = []
            dec_row = []
            for hh in range(HEADS_PER_GROUP):
                h = g * HEADS_PER_GROUP + hh
                xdt = xs_c[:, h * HEAD_DIM:(h + 1) * HEAD_DIM] * dt_c[:, h:h + 1]
                lmat = jnp.where(lower, jnp.exp(cs[:, h:h + 1] - csT[h:h + 1, :]), 0.0)
                y_diag = _dot((cb * lmat).astype(BF16), xdt.astype(BF16))
                y_heads.append(y_diag + y_off[:, hh * HEAD_DIM:(hh + 1) * HEAD_DIM] * ecs[:, h:h + 1])
                xdd.append((xdt * dst[:, h:h + 1]).astype(BF16))
                dec_row.append(jnp.broadcast_to(cdec[:, h:h + 1], (1, HEAD_DIM)))
            contrib = _dot_tn(bg, jnp.concatenate(xdd, axis=-1))
            st_scr[g] = st_scr[g] * jnp.concatenate(dec_row, axis=-1) + contrib
        y_chunks.append(jnp.concatenate(y_heads, axis=-1))
    y = y_chunks[0] if len(y_chunks) == 1 else jnp.concatenate(y_chunks, axis=0)
    y = (y + dskip_ref[...] * xs) * _silu(z)
    m_parts = []
    for g in range(N_GROUPS):
        yg = y[:, g * GROUP_W:(g + 1) * GROUP_W]
        m_parts.append(_rms(yg, sng_ref[:, g * GROUP_W:(g + 1) * GROUP_W]))
    m_out = jnp.concatenate(m_parts, axis=-1).astype(BF16)

    @pl.when(t == nt - 1)
    def _():
        for g in range(N_GROUPS):
            nss_ref[0, g * HEADS_PER_GROUP:(g + 1) * HEADS_PER_GROUP] = (
                st_scr[g].T.reshape(HEADS_PER_GROUP, HEAD_DIM, D_STATE))

    h1_ref[0] = x + _dot(a_out, wout_ref[0:C_CONF, :]) + _dot(m_out, wout_ref[C_CONF:, :])


def _const_spec(shape):
    nd = len(shape)
    return pl.BlockSpec(shape, lambda *_: (0,) * nd, pipeline_mode=pl.Buffered(1))


def _mixer_prompt(x, w, tl=256):
    bsz, seq, _ = x.shape
    kern = functools.partial(_mixer_prompt_kernel, tl=tl)
    consts = [w['g_mix'], w['w_in'], w['w_dtT'], w['conf_w'], w['conf_b'], w['ln_g'], w['ln_b'],
              w['ssm_w'], w['ssm_b'], w['dt_bias'], w['dt_biasT'], w['a_log'], w['a_logT'], w['d_skip'],
              w['ssm_norm_g'], w['w_out']]
    return pl.pallas_call(
        kern,
        grid=(bsz, seq // tl),
        in_specs=[pl.BlockSpec((1, tl, D_MODEL), lambda b, t: (b, t, 0))] + [_const_spec(c.shape) for c in consts],
        out_specs=[
            pl.BlockSpec((1, tl, D_MODEL), lambda b, t: (b, t, 0)),
            pl.BlockSpec((1, K_CONF - 1, C_CONF), lambda b, t: (b, 0, 0)),
            pl.BlockSpec((1, K_SSM - 1, CONV_DIM), lambda b, t: (b, 0, 0)),
            pl.BlockSpec((1, N_HEADS, HEAD_DIM, D_STATE), lambda b, t: (b, 0, 0, 0)),
        ],
        out_shape=[
            jax.ShapeDtypeStruct((bsz, seq, D_MODEL), F32),
            jax.ShapeDtypeStruct((bsz, K_CONF - 1, C_CONF), F32),
            jax.ShapeDtypeStruct((bsz, K_SSM - 1, CONV_DIM), F32),
            jax.ShapeDtypeStruct((bsz, N_HEADS, HEAD_DIM, D_STATE), F32),
        ],
        scratch_shapes=[
            pltpu.VMEM((N_LANE_TILES, CONF_PAD + tl, LANES), F32),
            pltpu.VMEM((N_LANE_TILES, tl, LANES), F32),
            pltpu.VMEM((SSM_PAD + tl, CONV_DIM), F32),
            pltpu.VMEM((N_GROUPS, D_STATE, GROUP_W), F32),
        ],
        compiler_params=pltpu.CompilerParams(
            dimension_semantics=("arbitrary", "arbitrary"), vmem_limit_bytes=VMEM_LIMIT),
        name="mixer_prompt",
    )(x, *consts)


def _mixer_sample_kernel(x_ref, cst_ref, mst_ref, sst_ref, gmix_ref, win_ref, cw_ref, cb_ref, lng_ref, lnb_ref,
                         sw_ref, sb_ref, dtb_ref, alog_ref, dskip_ref, sng_ref, wout_ref, hexp_ref,
                         h1_ref, ncc_ref, nsc_ref, nss_ref,
                         proj_scr, mix_scr, *, bb):
    i = pl.program_id(0)
    n = pl.num_programs(0)

    @pl.when(i == 0)
    def _():
        u = _rms(x_ref[...], gmix_ref[...]).astype(BF16)
        proj_scr[...] = _dot(u, win_ref[...])

    r0 = pl.multiple_of(i * bb, bb)
    proj = proj_scr[pl.ds(r0, bb), :]

    glu = proj[:, 0:O_GATE] * _sigmoid(proj[:, O_GATE:O_Z])
    acc = cb_ref[...] + cw_ref[pl.ds(K_CONF - 1, 1), :] * glu
    for k in range(K_CONF - 1):
        row_k = cst_ref[:, k, :]
        acc = acc + cw_ref[pl.ds(k, 1), :] * row_k
        if k >= 1:
            ncc_ref[:, k - 1, :] = row_k
    ncc_ref[:, K_CONF - 2, :] = glu
    mean = jnp.mean(acc, axis=-1, keepdims=True)
    xc = acc - mean
    rstd = lax.rsqrt(jnp.mean(xc * xc, axis=-1, keepdims=True) + EPS)
    a_out = _silu(xc * rstd * lng_ref[...] + lnb_ref[...])

    z = proj[:, O_Z:O_XBC]
    xbc_raw = proj[:, O_XBC:O_DT]
    acc = sb_ref[...] + sw_ref[pl.ds(K_SSM - 1, 1), :] * xbc_raw
    for k in range(K_SSM - 1):
        row_k = mst_ref[:, k, :]
        acc = acc + sw_ref[pl.ds(k, 1), :] * row_k
        if k >= 1:
            nsc_ref[:, k - 1, :] = row_k
    nsc_ref[:, K_SSM - 2, :] = xbc_raw
    xbc = _silu(acc)
    xs = xbc[:, 0:D_INNER]
    bm = xbc[:, D_INNER:D_INNER + N_GROUPS * D_STATE]
    cm = xbc[:, D_INNER + N_GROUPS * D_STATE:]
    dt = _softplus(proj[:, O_DT:IN_COLS] + dtb_ref[...])
    dec = jnp.exp(dt * (-jnp.exp(alog_ref[...])))
    hexp = hexp_ref[...]
    d_h, d_m, d_l = _split3(dec)
    dec_e = _dot(d_h, hexp) + _dot(d_m, hexp) + _dot(d_l, hexp)
    t_h, t_m, t_l = _split3(dt)
    dt_e = _dot(t_h, hexp) + _dot(t_m, hexp) + _dot(t_l, hexp)
    xdt = xs * dt_e

    lane = lax.broadcasted_iota(jnp.int32, (SUBLANES, D_INNER), 1)
    sub = lax.broadcasted_iota(jnp.int32, (SUBLANES, D_INNER), 0)
    gmask = (lane // GROUP_W) == sub
    ones_row = (lax.broadcasted_iota(jnp.int32, (SUBLANES, D_STATE), 0) < 3).astype(BF16)

    y_rows = []
    for b in range(bb):
        xrow = jnp.where(gmask, jnp.broadcast_to(xdt[b:b + 1, :], (SUBLANES, D_INNER)), 0.0)
        x_h, x_m, x_l = [v.astype(F32) for v in _split3(xrow)]
        bmat = jnp.concatenate([bm[b:b + 1, g * D_STATE:(g + 1) * D_STATE] for g in range(N_GROUPS)]
                               + [jnp.zeros((SUBLANES - N_GROUPS, D_STATE), F32)], axis=0)
        b_h, b_m, b_l = [v.astype(F32) for v in _split3(bmat)]
        lhs = jnp.concatenate([x_h, x_h, x_m, x_h, x_m, x_l], axis=0).astype(BF16)
        rhs = jnp.concatenate([b_h, b_m, b_h, b_l, b_m, b_h], axis=0).astype(BF16)
        upd = _dot_tn(lhs, rhs)
        dbc = jnp.broadcast_to(dec_e[b:b + 1, :], (SUBLANES, D_INNER))
        q_h = dbc.astype(BF16).astype(F32)
        q_m = (dbc - q_h).astype(BF16).astype(F32)
        q_l = dbc - q_h - q_m
        dlhs = jnp.where(sub == 0, q_h, jnp.where(sub == 1, q_m, jnp.where(sub == 2, q_l, 0.0))).astype(BF16)
        dfull = _dot_tn(dlhs, ones_row)
        hnew = dfull * sst_ref[b].reshape(D_INNER, D_STATE) + upd
        nss_ref[b] = hnew.reshape(N_HEADS, HEAD_DIM, D_STATE)
        cmat = jnp.concatenate([cm[b:b + 1, g * D_STATE:(g + 1) * D_STATE] for g in range(N_GROUPS)]
                               + [jnp.zeros((SUBLANES - N_GROUPS, D_STATE), F32)], axis=0)
        yg = _dot_nt(cmat.astype(BF16), hnew.astype(BF16))
        y_rows.append(jnp.sum(jnp.where(gmask, yg, 0.0), axis=0, keepdims=True))
    y = jnp.concatenate(y_rows, axis=0)
    y = (y + dskip_ref[...] * xs) * _silu(z)
    m_parts = []
    for g in range(N_GROUPS):
        m_parts.append(_rms(y[:, g * GROUP_W:(g + 1) * GROUP_W], sng_ref[:, g * GROUP_W:(g + 1) * GROUP_W]))
    mix_scr[pl.ds(r0, bb), :] = jnp.concatenate([a_out] + m_parts, axis=-1)

    @pl.when(i == n - 1)
    def _():
        h1_ref[...] = x_ref[...] + _dot(mix_scr[...].astype(BF16), wout_ref[...])


def _mixer_sample(x, cst, mst, sst, w, bb=8):
    nb = x.shape[0]
    kern = functools.partial(_mixer_sample_kernel, bb=bb)
    consts = [w['g_mix'], w['w_in'], w['conf_w2'], w['conf_b2'], w['ln_g'], w['ln_b'],
              w['ssm_w'], w['ssm_b'], w['dt_bias'], w['a_log'], w['d_skip'], w['ssm_norm_g'], w['w_out'],
              w['head_expand']]
    return pl.pallas_call(
        kern,
        grid=(nb // bb,),
        in_specs=[
            _const_spec((nb, D_MODEL)),
            pl.BlockSpec((bb, K_CONF - 1, C_CONF), lambda i: (i, 0, 0)),
            pl.BlockSpec((bb, K_SSM - 1, CONV_DIM), lambda i: (i, 0, 0)),
            pl.BlockSpec((bb, N_HEADS, HEAD_DIM, D_STATE), lambda i: (i, 0, 0, 0)),
        ] + [_const_spec(c.shape) for c in consts],
        out_specs=[
            pl.BlockSpec((nb, D_MODEL), lambda i: (0, 0)),
            pl.BlockSpec((bb, K_CONF - 1, C_CONF), lambda i: (i, 0, 0)),
            pl.BlockSpec((bb, K_SSM - 1, CONV_DIM), lambda i: (i, 0, 0)),
            pl.BlockSpec((bb, N_HEADS, HEAD_DIM, D_STATE), lambda i: (i, 0, 0, 0)),
        ],
        out_shape=[
            jax.ShapeDtypeStruct((nb, D_MODEL), F32),
            jax.ShapeDtypeStruct((nb, K_CONF - 1, C_CONF), F32),
            jax.ShapeDtypeStruct((nb, K_SSM - 1, CONV_DIM), F32),
            jax.ShapeDtypeStruct((nb, N_HEADS, HEAD_DIM, D_STATE), F32),
        ],
        scratch_shapes=[
            pltpu.VMEM((nb, IN_COLS), F32),
            pltpu.VMEM((nb, D_MODEL + D_INNER), F32),
        ],
        compiler_params=pltpu.CompilerParams(dimension_semantics=("arbitrary",), vmem_limit_bytes=VMEM_LIMIT),
        name="mixer_sample",
    )(x, cst, mst, sst, *consts)


def _first_argmax(v, width):
    lane = lax.broadcasted_iota(jnp.int32, v.shape, 1)
    m = jnp.max(v, axis=-1, keepdims=True)
    idx = jnp.min(jnp.where(v == m, lane, width), axis=-1, keepdims=True)
    return m, idx


def _ffn_kernel(h_ref, p_ref, gffn_ref, wr_ref, br_ref, wg_ref, wu_ref, wd_ref, gple_ref, wpg_ref, wp_ref,
                gfin_ref, y_ref, t_scr, comb_scr, acc_scr):
    e = pl.program_id(1)
    ne = pl.num_programs(1)

    @pl.when(e == 0)
    def _():
        tf = _rms(h_ref[...], gffn_ref[...])
        t_scr[...] = tf.astype(BF16)
        t_h, t_m, t_l = _split3(tf)
        w_h = wr_ref[0]
        w_m = wr_ref[1]
        logits = (_dot(t_h, w_h) + _dot(t_m, w_h) + _dot(t_h, w_m) + _dot(t_l, w_h) + _dot(t_m, w_m)) + br_ref[...]
        lg = logits[:, 0:N_EXPERT_GROUPS]
        le = logits[:, N_EXPERTS:2 * N_EXPERTS]
        eg = jnp.exp(lg - jnp.max(lg, axis=-1, keepdims=True))
        pg = eg / jnp.sum(eg, axis=-1, keepdims=True)
        g_val, g_idx = _first_argmax(pg, N_EXPERT_GROUPS)
        lane16 = lax.broadcasted_iota(jnp.int32, le.shape, 1)
        in_grp = (lane16 // EXPERTS_PER_GROUP) == g_idx
        neg = jnp.float32(-jnp.inf)
        le_m = jnp.where(in_grp, le, neg)
        ee = jnp.where(in_grp, jnp.exp(le - jnp.max(le_m, axis=-1, keepdims=True)), 0.0)
        pe = ee / jnp.sum(ee, axis=-1, keepdims=True)
        pe_m = jnp.where(in_grp, pe, -1.0)
        v1, i1 = _first_argmax(pe_m, N_EXPERTS)
        pe_m2 = jnp.where(lane16 == i1, -1.0, pe_m)
        v2, i2 = _first_argmax(pe_m2, N_EXPERTS)
        den = v1 + v2
        comb = jnp.where(lane16 == i1, g_val * v1 / den, 0.0) + jnp.where(lane16 == i2, g_val * v2 / den, 0.0)
        comb_scr[...] = comb
        acc_scr[...] = jnp.zeros(acc_scr.shape, F32)

    t = t_scr[...]
    hdn = _silu(_dot(t, wg_ref[0])) * _dot(t, wu_ref[0])
    out_e = _dot(hdn.astype(BF16), wd_ref[0])
    lane16 = lax.broadcasted_iota(jnp.int32, comb_scr.shape, 1)
    c_e = jnp.sum(jnp.where(lane16 == e, comb_scr[...], 0.0), axis=-1, keepdims=True)
    acc_scr[...] += c_e * out_e

    @pl.when(e == ne - 1)
    def _():
        h2 = h_ref[...] + acc_scr[...]
        gate = _sigmoid(_dot(_rms(h2, gple_ref[...]).astype(BF16), wpg_ref[...]))
        h3 = h2 + _dot(p_ref[...].astype(BF16), wp_ref[...]) * gate
        y_ref[...] = _rms(h3, gfin_ref[...])


def _ffn(h, p, w, tb):
    nt = h.shape[0]
    return pl.pallas_call(
        _ffn_kernel,
        grid=(nt // tb, N_EXPERTS),
        in_specs=[
            pl.BlockSpec((tb, D_MODEL), lambda i, e: (i, 0)),
            pl.BlockSpec((tb, PLE_DIM), lambda i, e: (i, 0)),
            _const_spec((1, D_MODEL)),
            _const_spec((2, D_MODEL, 2 * N_EXPERTS)),
            _const_spec((1, 2 * N_EXPERTS)),
            pl.BlockSpec((1, D_MODEL, D_FF), lambda i, e: (e, 0, 0)),
            pl.BlockSpec((1, D_MODEL, D_FF), lambda i, e: (e, 0, 0)),
            pl.BlockSpec((1, D_FF, D_MODEL), lambda i, e: (e, 0, 0)),
            _const_spec((1, D_MODEL)),
            _const_spec((D_MODEL, D_MODEL)),
            _const_spec((PLE_DIM, D_MODEL)),
            _const_spec((1, D_MODEL)),
        ],
        out_specs=pl.BlockSpec((tb, D_MODEL), lambda i, e: (i, 0)),
        out_shape=jax.ShapeDtypeStruct((nt, D_MODEL), F32),
        scratch_shapes=[
            pltpu.VMEM((tb, D_MODEL), BF16),
            pltpu.VMEM((tb, N_EXPERTS), F32),
            pltpu.VMEM((tb, D_MODEL), F32),
        ],
        compiler_params=pltpu.CompilerParams(
            dimension_semantics=("arbitrary", "arbitrary"), vmem_limit_bytes=VMEM_LIMIT),
        name="ffn",
    )(h, p, w['g_ffn'], w['w_router'], w['b_router'], w['w_gate_e'], w['w_up_e'], w['w_down_e'],
      w['g_ple'], w['w_ple_gate'], w['w_ple'], w['g_final'])


ROUTE_TB = 1024
EXPERT_TM = 512
SC_WINDOW = 128
SC_CORES = 2
SC_SUBCORES = 16
SC_WORKERS = SC_CORES * SC_SUBCORES
HALF = D_MODEL // 2


def _pack_bf16_pairs(v):
    bits = pltpu.bitcast(v.astype(BF16).astype(F32), jnp.uint32)
    packed = bits[:, HALF:] | (bits[:, :HALF] >> 16)
    return pltpu.bitcast(packed, jnp.int32)


def _unpack_bf16_pairs(p):
    u = pltpu.bitcast(p, jnp.uint32)
    lo = pltpu.bitcast(u << 16, F32)
    hi = pltpu.bitcast(u & jnp.uint32(0xFFFF0000), F32)
    return jnp.concatenate([lo, hi], axis=-1)


def _route_kernel(h_ref, gffn_ref, wr_ref, br_ref, tp_ref, meta_ref, wts_ref, cnt_ref, lower_scr, carry_scr):
    i = pl.program_id(0)
    tb = h_ref.shape[0]

    @pl.when(i == 0)
    def _():
        r = lax.broadcasted_iota(jnp.int32, (tb, tb), 0)
        c = lax.broadcasted_iota(jnp.int32, (tb, tb), 1)
        lower_scr[...] = (r > c).astype(BF16)
        carry_scr[...] = jnp.zeros(carry_scr.shape, F32)

    tf = _rms(h_ref[...], gffn_ref[...])
    tp_ref[...] = _pack_bf16_pairs(tf)
    t_h, t_m, t_l = _split3(tf)
    w_h = wr_ref[0]
    w_m = wr_ref[1]
    logits = (_dot(t_h, w_h) + _dot(t_m, w_h) + _dot(t_h, w_m) + _dot(t_l, w_h) + _dot(t_m, w_m)) + br_ref[...]
    lg = logits[:, 0:N_EXPERT_GROUPS]
    le = logits[:, N_EXPERTS:2 * N_EXPERTS]
    eg = jnp.exp(lg - jnp.max(lg, axis=-1, keepdims=True))
    pg = eg / jnp.sum(eg, axis=-1, keepdims=True)
    g_val, g_idx = _first_argmax(pg, N_EXPERT_GROUPS)
    lane16 = lax.broadcasted_iota(jnp.int32, le.shape, 1)
    in_grp = (lane16 // EXPERTS_PER_GROUP) == g_idx
    le_m = jnp.where(in_grp, le, -jnp.inf)
    ee = jnp.where(in_grp, jnp.exp(le - jnp.max(le_m, axis=-1, keepdims=True)), 0.0)
    pe = ee / jnp.sum(ee, axis=-1, keepdims=True)
    pe_m = jnp.where(in_grp, pe, -1.0)
    v1, i1 = _first_argmax(pe_m, N_EXPERTS)
    v2, i2 = _first_argmax(jnp.where(lane16 == i1, -1.0, pe_m), N_EXPERTS)
    den = v1 + v2
    w0 = g_val * v1 / den
    w1 = g_val * v2 / den

    sel0 = lane16 == i1
    sel1 = lane16 == i2
    hot = jnp.where(sel0 | sel1, 1.0, 0.0)
    rank = _dot(lower_scr[...], hot.astype(BF16)) + carry_scr[...]
    r0 = jnp.sum(jnp.where(sel0, rank, 0.0), axis=-1, keepdims=True)
    r1 = jnp.sum(jnp.where(sel1, rank, 0.0), axis=-1, keepdims=True)
    carry_scr[...] += jnp.sum(hot, axis=0, keepdims=True)
    cnt_ref[...] = carry_scr[...]

    lane = lax.broadcasted_iota(jnp.int32, (tb, LANES), 1)
    metaf = jnp.where(lane == 0, i1.astype(F32), jnp.where(lane == 1, i2.astype(F32),
                      jnp.where(lane == 2, r0, jnp.where(lane == 3, r1, 0.0))))
    meta_ref[...] = metaf.T[0:SUBLANES, :].astype(jnp.int32)
    wts_ref[...] = jnp.where(lane == 0, w0, jnp.where(lane == 1, w1, 0.0))


def _route(h, w, tb=ROUTE_TB):
    nt = h.shape[0]
    return pl.pallas_call(
        _route_kernel,
        grid=(nt // tb,),
        in_specs=[
            pl.BlockSpec((tb, D_MODEL), lambda i: (i, 0)),
            _const_spec((1, D_MODEL)),
            _const_spec((2, D_MODEL, 2 * N_EXPERTS)),
            _const_spec((1, 2 * N_EXPERTS)),
        ],
        out_specs=[
            pl.BlockSpec((tb, HALF), lambda i: (i, 0)),
            pl.BlockSpec((SUBLANES, tb), lambda i: (0, i)),
            pl.BlockSpec((tb, LANES), lambda i: (i, 0)),
            pl.BlockSpec((1, N_EXPERTS), lambda i: (0, 0)),
        ],
        out_shape=[
            jax.ShapeDtypeStruct((nt, HALF), jnp.int32),
            jax.ShapeDtypeStruct((SUBLANES, nt), jnp.int32),
            jax.ShapeDtypeStruct((nt, LANES), F32),
            jax.ShapeDtypeStruct((1, N_EXPERTS), F32),
        ],
        scratch_shapes=[pltpu.VMEM((tb, tb), BF16), pltpu.VMEM((1, N_EXPERTS), F32)],
        compiler_params=pltpu.CompilerParams(dimension_semantics=("arbitrary",), vmem_limit_bytes=VMEM_LIMIT),
        name="moe_route",
    )(h, w['g_ffn'], w['w_router'], w['b_router'])


def _sc_mesh():
    return plsc.VectorSubcoreMesh(core_axis_name="c", subcore_axis_name="s")


def _sc_dispatch(tp, slot0, slot1, n_slots):
    nt = tp.shape[0]

    per_worker = nt // (SC_WINDOW * SC_WORKERS)

    @pl.kernel(out_type=jax.ShapeDtypeStruct((n_slots, HALF), tp.dtype), mesh=_sc_mesh(), name="moe_dispatch",
               scratch_types=[pltpu.VMEM((1, SC_WINDOW), jnp.int32), pltpu.VMEM((1, SC_WINDOW), jnp.int32),
                              pltpu.VMEM((SC_WINDOW, HALF), tp.dtype)])
    def run(x_hbm, i0_hbm, i1_hbm, o_hbm, i0_v, i1_v, buf_v):
        worker = lax.axis_index("c") * SC_SUBCORES + lax.axis_index("s")

        @pl.loop(0, per_worker)
        def _(k):
            base = (worker * per_worker + k) * SC_WINDOW
            pltpu.sync_copy(i0_hbm.at[:, pl.ds(base, SC_WINDOW)], i0_v)
            pltpu.sync_copy(i1_hbm.at[:, pl.ds(base, SC_WINDOW)], i1_v)
            pltpu.sync_copy(x_hbm.at[pl.ds(base, SC_WINDOW)], buf_v)
            pltpu.sync_copy(buf_v, o_hbm.at[i0_v.at[0]])
            pltpu.sync_copy(buf_v, o_hbm.at[i1_v.at[0]])

    return run(tp, slot0, slot1)


def _sc_combine(ys, slot0, slot1):
    nt = slot0.shape[1]
    out = jax.ShapeDtypeStruct((nt, HALF), ys.dtype)

    per_worker = nt // (SC_WINDOW * SC_WORKERS)

    @pl.kernel(out_type=(out, out), mesh=_sc_mesh(), name="moe_combine",
               scratch_types=[pltpu.VMEM((1, SC_WINDOW), jnp.int32), pltpu.VMEM((1, SC_WINDOW), jnp.int32),
                              pltpu.VMEM((SC_WINDOW, HALF), ys.dtype)])
    def run(y_hbm, i0_hbm, i1_hbm, o0_hbm, o1_hbm, i0_v, i1_v, buf_v):
        worker = lax.axis_index("c") * SC_SUBCORES + lax.axis_index("s")

        @pl.loop(0, per_worker)
        def _(k):
            base = (worker * per_worker + k) * SC_WINDOW
            pltpu.sync_copy(i0_hbm.at[:, pl.ds(base, SC_WINDOW)], i0_v)
            pltpu.sync_copy(i1_hbm.at[:, pl.ds(base, SC_WINDOW)], i1_v)
            pltpu.sync_copy(y_hbm.at[i0_v.at[0]], buf_v)
            pltpu.sync_copy(buf_v, o0_hbm.at[pl.ds(base, SC_WINDOW)])
            pltpu.sync_copy(y_hbm.at[i1_v.at[0]], buf_v)
            pltpu.sync_copy(buf_v, o1_hbm.at[pl.ds(base, SC_WINDOW)])

    return run(ys, slot0, slot1)


def _experts_kernel(te_ref, nv_ref, x_ref, wg_ref, wu_ref, wd_ref, y_ref):
    j = pl.program_id(0)

    @pl.when(j < nv_ref[0])
    def _():
        x = _unpack_bf16_pairs(x_ref[...]).astype(BF16)
        hdn = _silu(_dot(x, wg_ref[0])) * _dot(x, wu_ref[0])
        y_ref[...] = _pack_bf16_pairs(_dot(hdn.astype(BF16), wd_ref[0]))


def _experts(xs, tile_expert, n_valid, w, tm=EXPERT_TM):
    n_slots = xs.shape[0]
    grid_spec = pltpu.PrefetchScalarGridSpec(
        num_scalar_prefetch=2,
        grid=(n_slots // tm,),
        in_specs=[
            pl.BlockSpec((tm, HALF), lambda j, te, nv: (j, 0)),
            pl.BlockSpec((1, D_MODEL, D_FF), lambda j, te, nv: (te[j], 0, 0)),
            pl.BlockSpec((1, D_MODEL, D_FF), lambda j, te, nv: (te[j], 0, 0)),
            pl.BlockSpec((1, D_FF, D_MODEL), lambda j, te, nv: (te[j], 0, 0)),
        ],
        out_specs=pl.BlockSpec((tm, HALF), lambda j, te, nv: (j, 0)),
    )
    return pl.pallas_call(
        _experts_kernel,
        grid_spec=grid_spec,
        out_shape=jax.ShapeDtypeStruct((n_slots, HALF), jnp.int32),
        compiler_params=pltpu.CompilerParams(dimension_semantics=("arbitrary",), vmem_limit_bytes=VMEM_LIMIT),
        name="moe_experts",
    )(tile_expert, n_valid, xs, w['w_gate_e'], w['w_up_e'], w['w_down_e'])


def _finish_kernel(h_ref, y0_ref, y1_ref, wts_ref, p_ref, gple_ref, wpg_ref, wp_ref, gfin_ref, o_ref):
    wts = wts_ref[...]
    moe = wts[:, 0:1] * _unpack_bf16_pairs(y0_ref[...]) + wts[:, 1:2] * _unpack_bf16_pairs(y1_ref[...])
    h2 = h_ref[...] + moe
    gate = _sigmoid(_dot(_rms(h2, gple_ref[...]).astype(BF16), wpg_ref[...]))
    h3 = h2 + _dot(p_ref[...].astype(BF16), wp_ref[...]) * gate
    o_ref[...] = _rms(h3, gfin_ref[...])


def _finish(h, y0, y1, wts, p, w, tb=1024):
    nt = h.shape[0]
    return pl.pallas_call(
        _finish_kernel,
        grid=(nt // tb,),
        in_specs=[
            pl.BlockSpec((tb, D_MODEL), lambda i: (i, 0)),
            pl.BlockSpec((tb, HALF), lambda i: (i, 0)),
            pl.BlockSpec((tb, HALF), lambda i: (i, 0)),
            pl.BlockSpec((tb, LANES), lambda i: (i, 0)),
            pl.BlockSpec((tb, PLE_DIM), lambda i: (i, 0)),
            _const_spec((1, D_MODEL)),
            _const_spec((D_MODEL, D_MODEL)),
            _const_spec((PLE_DIM, D_MODEL)),
            _const_spec((1, D_MODEL)),
        ],
        out_specs=pl.BlockSpec((tb, D_MODEL), lambda i: (i, 0)),
        out_shape=jax.ShapeDtypeStruct((nt, D_MODEL), F32),
        compiler_params=pltpu.CompilerParams(dimension_semantics=("arbitrary",), vmem_limit_bytes=VMEM_LIMIT),
        name="moe_finish",
    )(h, y0, y1, wts, p, w['g_ple'], w['w_ple_gate'], w['w_ple'], w['g_final'])


def _ffn_sparse(h, p, w):
    nt = h.shape[0]
    tm = EXPERT_TM
    n_tiles = (nt * 2) // tm + N_EXPERTS
    tp, meta, wts, counts = _route(h, w)
    cnt = counts.reshape(N_EXPERTS).astype(jnp.int32)
    tiles_e = (cnt + tm - 1) // tm
    tile_end = jnp.cumsum(tiles_e)
    off = (tile_end - tiles_e) * tm
    n_valid = tile_end[-1:]
    tile_ids = jnp.arange(n_tiles, dtype=jnp.int32)
    tile_expert = jnp.sum((tile_ids[:, None] >= tile_end[None, :]).astype(jnp.int32), axis=1)
    last_expert = jnp.sum((n_valid - 1 >= tile_end).astype(jnp.int32))
    tile_expert = jnp.minimum(tile_expert, last_expert).astype(jnp.int32)
    eid = meta[0:2]
    slot = meta[2:4] + jnp.sum(jnp.where(eid[..., None] == jnp.arange(N_EXPERTS), off, 0), axis=-1)
    slot0 = slot[0:1]
    slot1 = slot[1:2]
    xs = _sc_dispatch(tp, slot0, slot1, n_tiles * tm)
    ys = _experts(xs, tile_expert, n_valid.astype(jnp.int32), w)
    y0, y1 = _sc_combine(ys, slot0, slot1)
    return _finish(h, y0, y1, wts, p, w)


def _prep_weights(g_mix, w_in, conf_dw_w, conf_dw_b, conf_ln_g, conf_ln_b, ssm_conv_w, ssm_conv_b,
                  dt_bias, a_log, d_skip, ssm_norm_g, w_out, g_ffn, w_rg, b_rg, w_re, b_re,
                  w_gate_e, w_up_e, w_down_e, g_ple, w_ple_gate, w_ple, g_final):
    row = lambda v: v.reshape(1, -1)
    w_in_b = w_in.astype(BF16)
    conf_w_pad = jnp.concatenate([conf_dw_w, jnp.zeros((CONF_PAD - K_CONF, C_CONF), F32)], axis=0)
    w_router = jnp.concatenate(
        [w_rg, jnp.zeros((D_MODEL, N_EXPERTS - N_EXPERT_GROUPS), F32), w_re], axis=1)
    wr_hi = w_router.astype(BF16)
    wr_mid = (w_router - wr_hi.astype(F32)).astype(BF16)
    b_router = jnp.concatenate([b_rg, jnp.zeros((N_EXPERTS - N_EXPERT_GROUPS,), F32), b_re]).reshape(1, -1)
    head_of_lane = jnp.arange(D_INNER) // HEAD_DIM
    head_expand = (head_of_lane[None, :] == jnp.arange(N_HEADS)[:, None]).astype(BF16)
    return dict(
        g_mix=row(g_mix), w_in=w_in_b, w_dtT=w_in_b[:, O_DT:].T,
        conf_w=conf_w_pad.reshape(CONF_PAD, N_LANE_TILES, LANES).transpose(1, 0, 2),
        conf_b=conf_dw_b.reshape(N_LANE_TILES, 1, LANES),
        conf_w2=conf_dw_w, conf_b2=row(conf_dw_b),
        ln_g=row(conf_ln_g), ln_b=row(conf_ln_b), ssm_w=ssm_conv_w, ssm_b=row(ssm_conv_b),
        dt_bias=row(dt_bias), dt_biasT=dt_bias.reshape(-1, 1), a_log=row(a_log), a_logT=a_log.reshape(-1, 1),
        d_skip=row(jnp.repeat(d_skip, HEAD_DIM)), ssm_norm_g=row(ssm_norm_g), w_out=w_out.astype(BF16),
        head_expand=head_expand,
        g_ffn=row(g_ffn), w_router=jnp.stack([wr_hi, wr_mid]), b_router=b_router,
        w_gate_e=w_gate_e.astype(BF16), w_up_e=w_up_e.astype(BF16), w_down_e=w_down_e.astype(BF16),
        g_ple=row(g_ple), w_ple_gate=w_ple_gate.astype(BF16), w_ple=w_ple.astype(BF16), g_final=row(g_final),
    )


def kernel(x_prompt, x_sample, p_prompt, p_sample, state_conf_conv, state_ssm_conv, state_ssm, g_mix, w_in, conf_dw_w, conf_dw_b, conf_ln_g, conf_ln_b, ssm_conv_w, ssm_conv_b, dt_bias, a_log, d_skip, ssm_norm_g, w_out, g_ffn, w_rg, b_rg, w_re, b_re, w_gate_e, w_up_e, w_down_e, g_ple, w_ple_gate, w_ple, g_final):
    depth = g_mix.shape[0]
    bsz, seq, _ = x_prompt.shape
    nb = x_sample.shape[0]
    hp = x_prompt
    hs = x_sample.reshape(nb, D_MODEL)
    cp_l, mp_l, sp_l, cs_l, ms_l, ss_l = [], [], [], [], [], []
    for i in range(depth):
        w = _prep_weights(g_mix[i], w_in[i], conf_dw_w[i], conf_dw_b[i], conf_ln_g[i], conf_ln_b[i],
                          ssm_conv_w[i], ssm_conv_b[i], dt_bias[i], a_log[i], d_skip[i], ssm_norm_g[i], w_out[i],
                          g_ffn[i], w_rg[i], b_rg[i], w_re[i], b_re[i], w_gate_e[i], w_up_e[i], w_down_e[i],
                          g_ple[i], w_ple_gate[i], w_ple[i], g_final)
        h1p, c, m, s = _mixer_prompt(hp, w)
        cp_l.append(c)
        mp_l.append(m)
        sp_l.append(s)
        h1s, c, m, s = _mixer_sample(hs, state_conf_conv[i], state_ssm_conv[i], state_ssm[i], w)
        cs_l.append(c)
        ms_l.append(m)
        ss_l.append(s)
        assert depth == 1
        hp = _ffn_sparse(h1p.reshape(bsz * seq, D_MODEL), p_prompt[i].reshape(bsz * seq, PLE_DIM), w)
        hp = hp.reshape(bsz, seq, D_MODEL)
        hs = _ffn(h1s, p_sample[i].reshape(nb, PLE_DIM), w, tb=nb)
    return (hp, hs.reshape(nb, 1, D_MODEL), jnp.stack(cp_l), jnp.stack(mp_l), jnp.stack(sp_l),
            jnp.stack(cs_l), jnp.stack(ms_l), jnp.stack(ss_l))
```

```python
import functools

import jax
import jax.numpy as jnp
from jax import lax
from jax.experimental import pallas as pl
from jax.experimental.pallas import tpu as pltpu
from jax.experimental.pallas import tpu_sc as plsc

F32 = jnp.float32
BF16 = jnp.bfloat16

D_MODEL = 1024
C_CONF = 1024
K_CONF = 31
D_INNER = 1024
HEAD_DIM = 64
N_HEADS = 16
N_GROUPS = 4
HEADS_PER_GROUP = N_HEADS // N_GROUPS
GROUP_W = HEADS_PER_GROUP * HEAD_DIM
D_STATE = 128
K_SSM = 4
CHUNK = 128
CONV_DIM = D_INNER + 2 * N_GROUPS * D_STATE
IN_COLS = 2 * C_CONF + D_INNER + CONV_DIM + N_HEADS
O_GATE = C_CONF
O_Z = 2 * C_CONF
O_XBC = O_Z + D_INNER
O_DT = O_XBC + CONV_DIM
N_EXPERT_GROUPS = 4
EXPERTS_PER_GROUP = 4
N_EXPERTS = 16
D_FF = 512
PLE_DIM = 256
EPS = 1e-6

LANES = 128
SUBLANES = 8
N_LANE_TILES = C_CONF // LANES
N_XBC_TILES = CONV_DIM // LANES
CONF_PAD = 32
SSM_PAD = 8
VMEM_LIMIT = 56 * 1024 * 1024


def _dot(a, b):
    return jnp.dot(a, b, preferred_element_type=F32)


def _dot_nt(a, b):
    return lax.dot_general(a, b, (((1,), (1,)), ((), ())), preferred_element_type=F32)


def _dot_tn(a, b):
    return lax.dot_general(a, b, (((0,), (0,)), ((), ())), preferred_element_type=F32)


def _split3(v):
    hi = v.astype(BF16)
    r = v - hi.astype(F32)
    mid = r.astype(BF16)
    lo = (r - mid.astype(F32)).astype(BF16)
    return hi, mid, lo


def _rms(x, g):
    return x * lax.rsqrt(jnp.mean(x * x, axis=-1, keepdims=True) + EPS) * g


def _sigmoid(x):
    return jax.nn.sigmoid(x)


def _silu(x):
    return x * jax.nn.sigmoid(x)


def _softplus(x):
    return jax.nn.softplus(x)


def _mixer_prompt_kernel(x_ref, gmix_ref, win_ref, wdtT_ref, cw_ref, cb_ref, lng_ref, lnb_ref,
                         sw_ref, sb_ref, dtb_ref, dtbT_ref, alog_ref, alogT_ref, dskip_ref, sng_ref,
                         wout_ref, hexp_ref,
                         h1_ref, ncc_ref, nsc_ref, nss_ref,
                         cscr, cout, mscr, xbc_scr, st_scr, *, tl):
    t = pl.program_id(1)
    nt = pl.num_programs(1)

    @pl.when(t == 0)
    def _():
        cscr[:, 0:CONF_PAD, :] = jnp.zeros((N_LANE_TILES, CONF_PAD, LANES), F32)
        mscr[:, 0:SSM_PAD, :] = jnp.zeros((N_XBC_TILES, SSM_PAD, LANES), F32)
        st_scr[...] = jnp.zeros(st_scr.shape, F32)

    x = x_ref[0]
    u = _rms(x, gmix_ref[...]).astype(BF16)

    glu = _dot(u, win_ref[:, 0:O_GATE]) * _sigmoid(_dot(u, win_ref[:, O_GATE:O_Z]))
    for lc in range(N_LANE_TILES):
        cscr[lc, CONF_PAD:CONF_PAD + tl, :] = glu[:, lc * LANES:(lc + 1) * LANES]

    rc = 64

    def conv_lane_tile(lc, carry):
        bias = cb_ref[lc]
        for r0 in range(0, tl, rc):
            acc = jnp.broadcast_to(bias, (rc, LANES))
            for k in range(K_CONF):
                acc = acc + cw_ref[lc, pl.ds(k, 1), :] * cscr[lc, pl.ds(r0 + k + CONF_PAD - (K_CONF - 1), rc), :]
            cout[lc, pl.ds(r0, rc), :] = acc
        return carry

    for lc in range(N_LANE_TILES):
        conv_lane_tile(lc, 0)

    @pl.when(t == nt - 1)
    def _():
        for lc in range(N_LANE_TILES):
            ncc_ref[0, :, lc * LANES:(lc + 1) * LANES] = cscr[lc, pl.ds(CONF_PAD + tl - (K_CONF - 1), K_CONF - 1), :]

    for lc in range(N_LANE_TILES):
        cscr[lc, 0:CONF_PAD, :] = cscr[lc, tl:tl + CONF_PAD, :]

    cc = [cout[lc] for lc in range(N_LANE_TILES)]
    tot = cc[0]
    for lc in range(1, N_LANE_TILES):
        tot = tot + cc[lc]
    mean = jnp.sum(tot, axis=-1, keepdims=True) * (1.0 / C_CONF)
    xc = [c - mean for c in cc]
    sq = xc[0] * xc[0]
    for lc in range(1, N_LANE_TILES):
        sq = sq + xc[lc] * xc[lc]
    rstd = lax.rsqrt(jnp.sum(sq, axis=-1, keepdims=True) * (1.0 / C_CONF) + EPS)
    a_out = jnp.concatenate(
        [_silu(xc[lc] * rstd * lng_ref[:, lc * LANES:(lc + 1) * LANES] + lnb_ref[:, lc * LANES:(lc + 1) * LANES])
         for lc in range(N_LANE_TILES)], axis=-1).astype(BF16)

    z = _dot(u, win_ref[:, O_Z:O_XBC])
    xbc_raw = _dot(u, win_ref[:, O_XBC:O_DT])
    for lt in range(N_XBC_TILES):
        mscr[lt, SSM_PAD:SSM_PAD + tl, :] = xbc_raw[:, lt * LANES:(lt + 1) * LANES]

    @pl.when(t == nt - 1)
    def _():
        for lt in range(N_XBC_TILES):
            nsc_ref[0, :, lt * LANES:(lt + 1) * LANES] = mscr[lt, pl.ds(SSM_PAD + tl - (K_SSM - 1), K_SSM - 1), :]

    def ssm_conv_lane_tile(lt, carry):
        bias = sb_ref[lt]
        for r0 in range(0, tl, rc):
            acc = jnp.broadcast_to(bias, (rc, LANES))
            for k in range(K_SSM):
                acc = acc + sw_ref[lt, pl.ds(k, 1), :] * mscr[lt, pl.ds(r0 + k + SSM_PAD - (K_SSM - 1), rc), :]
            xbc_scr[lt, pl.ds(r0, rc), :] = _silu(acc)
        mscr[lt, 0:SSM_PAD, :] = mscr[lt, tl:tl + SSM_PAD, :]
        return carry

    for lt in range(N_XBC_TILES):
        ssm_conv_lane_tile(lt, 0)

    n_x = D_INNER // LANES
    n_b = N_GROUPS * D_STATE // LANES
    xs = jnp.concatenate([xbc_scr[lt] for lt in range(n_x)], axis=-1)
    bm = jnp.concatenate([xbc_scr[lt] for lt in range(n_x, n_x + n_b)], axis=-1)
    cm = jnp.concatenate([xbc_scr[lt] for lt in range(n_x + n_b, N_XBC_TILES)], axis=-1)

    dt = _softplus(_dot(u, win_ref[:, O_DT:IN_COLS]) + dtb_ref[...])
    dtT = _softplus(_dot_nt(wdtT_ref[...], u) + dtbT_ref[...])
    a = dt * (-jnp.exp(alog_ref[...]))
    aT = dtT * (-jnp.exp(alogT_ref[...]))
    hexp = hexp_ref[...]
    d_h, d_m, d_l = _split3(dt)
    xdt_all = xs * (_dot(d_h, hexp) + _dot(d_m, hexp) + _dot(d_l, hexp))

    row = lax.broadcasted_iota(jnp.int32, (CHUNK, CHUNK), 0)
    col = lax.broadcasted_iota(jnp.int32, (CHUNK, CHUNK), 1)
    lower = row >= col
    tri = lower.astype(BF16)
    triT = (row <= col).astype(BF16)

    y_chunks = []
    for c in range(tl // CHUNK):
        r0 = c * CHUNK
        a_c = a[r0:r0 + CHUNK]
        aT_c = aT[:, r0:r0 + CHUNK]
        ah, am, al = _split3(a_c)
        cs = _dot(tri, ah) + _dot(tri, am) + _dot(tri, al)
        th, tm, tlo = _split3(aT_c)
        csT = _dot(th, triT) + _dot(tm, triT) + _dot(tlo, triT)
        cs_last = cs[CHUNK - 1:CHUNK, :]
        ecs = jnp.exp(cs)
        dst = jnp.exp(cs_last - cs)
        cdec = jnp.exp(cs_last)
        xdt_c = xdt_all[r0:r0 + CHUNK]
        y_heads = []
        for g in range(N_GROUPS):
            cg = cm[r0:r0 + CHUNK, g * D_STATE:(g + 1) * D_STATE].astype(BF16)
            bg = bm[r0:r0 + CHUNK, g * D_STATE:(g + 1) * D_STATE].astype(BF16)
            cb = _dot_nt(cg, bg)
            y_off = _dot(cg, st_scr[g].astype(BF16))
            xdd = []
            dec_row = []
            for hh in range(HEADS_PER_GROUP):
                h = g * HEADS_PER_GROUP + hh
                xdt = xdt_c[:, h * HEAD_DIM:(h + 1) * HEAD_DIM]
                lmat = jnp.where(lower, jnp.exp(cs[:, h:h + 1] - csT[h:h + 1, :]), 0.0)
                y_diag = _dot((cb * lmat).astype(BF16), xdt.astype(BF16))
                y_heads.append(y_diag + y_off[:, hh * HEAD_DIM:(hh + 1) * HEAD_DIM] * ecs[:, h:h + 1])
                xdd.append((xdt * dst[:, h:h + 1]).astype(BF16))
                dec_row.append(jnp.broadcast_to(cdec[:, h:h + 1], (1, HEAD_DIM)))
            contrib = _dot_tn(bg, jnp.concatenate(xdd, axis=-1))
            st_scr[g] = st_scr[g] * jnp.concatenate(dec_row, axis=-1) + contrib
        y_chunks.append(jnp.concatenate(y_heads, axis=-1))
    y = y_chunks[0] if len(y_chunks) == 1 else jnp.concatenate(y_chunks, axis=0)
    y = (y + dskip_ref[...] * xs) * _silu(z)
    m_parts = []
    for g in range(N_GROUPS):
        yg = y[:, g * GROUP_W:(g + 1) * GROUP_W]
        m_parts.append(_rms(yg, sng_ref[:, g * GROUP_W:(g + 1) * GROUP_W]))
    m_out = jnp.concatenate(m_parts, axis=-1).astype(BF16)

    @pl.when(t == nt - 1)
    def _():
        for g in range(N_GROUPS):
            nss_ref[0, g * HEADS_PER_GROUP:(g + 1) * HEADS_PER_GROUP] = (
                st_scr[g].T.reshape(HEADS_PER_GROUP, HEAD_DIM, D_STATE))

    h1_ref[0] = x + _dot(a_out, wout_ref[0:C_CONF, :]) + _dot(m_out, wout_ref[C_CONF:, :])


def _const_spec(shape):
    nd = len(shape)
    return pl.BlockSpec(shape, lambda *_: (0,) * nd, pipeline_mode=pl.Buffered(1))


def _mixer_prompt(x, w, tl=256):
    bsz, seq, _ = x.shape
    kern = functools.partial(_mixer_prompt_kernel, tl=tl)
    consts = [w['g_mix'], w['w_in'], w['w_dtT'], w['conf_w'], w['conf_b'], w['ln_g'], w['ln_b'],
              w['ssm_w3'], w['ssm_b3'], w['dt_bias'], w['dt_biasT'], w['a_log'], w['a_logT'], w['d_skip'],
              w['ssm_norm_g'], w['w_out'], w['head_expand']]
    return pl.pallas_call(
        kern,
        grid=(bsz, seq // tl),
        in_specs=[pl.BlockSpec((1, tl, D_MODEL), lambda b, t: (b, t, 0))] + [_const_spec(c.shape) for c in consts],
        out_specs=[
            pl.BlockSpec((1, tl, D_MODEL), lambda b, t: (b, t, 0)),
            pl.BlockSpec((1, K_CONF - 1, C_CONF), lambda b, t: (b, 0, 0)),
            pl.BlockSpec((1, K_SSM - 1, CONV_DIM), lambda b, t: (b, 0, 0)),
            pl.BlockSpec((1, N_HEADS, HEAD_DIM, D_STATE), lambda b, t: (b, 0, 0, 0)),
        ],
        out_shape=[
            jax.ShapeDtypeStruct((bsz, seq, D_MODEL), F32),
            jax.ShapeDtypeStruct((bsz, K_CONF - 1, C_CONF), F32),
            jax.ShapeDtypeStruct((bsz, K_SSM - 1, CONV_DIM), F32),
            jax.ShapeDtypeStruct((bsz, N_HEADS, HEAD_DIM, D_STATE), F32),
        ],
        scratch_shapes=[
            pltpu.VMEM((N_LANE_TILES, CONF_PAD + tl, LANES), F32),
            pltpu.VMEM((N_LANE_TILES, tl, LANES), F32),
            pltpu.VMEM((N_XBC_TILES, SSM_PAD + tl, LANES), F32),
            pltpu.VMEM((N_XBC_TILES, tl, LANES), F32),
            pltpu.VMEM((N_GROUPS, D_STATE, GROUP_W), F32),
        ],
        compiler_params=pltpu.CompilerParams(
            dimension_semantics=("arbitrary", "arbitrary"), vmem_limit_bytes=VMEM_LIMIT),
        name="mixer_prompt",
    )(x, *consts)


def _mixer_sample_kernel(x_ref, cst_ref, mst_ref, sst_ref, gmix_ref, win_ref, cw_ref, cb_ref, lng_ref, lnb_ref,
                         sw_ref, sb_ref, dtb_ref, alog_ref, dskip_ref, sng_ref, wout_ref, hexp_ref,
                         h1_ref, ncc_ref, nsc_ref, nss_ref,
                         proj_scr, mix_scr, *, bb):
    i = pl.program_id(0)
    n = pl.num_programs(0)

    @pl.when(i == 0)
    def _():
        u = _rms(x_ref[...], gmix_ref[...]).astype(BF16)
        proj_scr[...] = _dot(u, win_ref[...])

    r0 = pl.multiple_of(i * bb, bb)
    proj = proj_scr[pl.ds(r0, bb), :]

    glu = proj[:, 0:O_GATE] * _sigmoid(proj[:, O_GATE:O_Z])
    acc = cb_ref[...] + cw_ref[pl.ds(K_CONF - 1, 1), :] * glu
    for k in range(K_CONF - 1):
        row_k = cst_ref[k]
        acc = acc + cw_ref[pl.ds(k, 1), :] * row_k
        if k >= 1:
            ncc_ref[k - 1] = row_k
    ncc_ref[K_CONF - 2] = glu
    mean = jnp.mean(acc, axis=-1, keepdims=True)
    xc = acc - mean
    rstd = lax.rsqrt(jnp.mean(xc * xc, axis=-1, keepdims=True) + EPS)
    a_out = _silu(xc * rstd * lng_ref[...] + lnb_ref[...])

    z = proj[:, O_Z:O_XBC]
    xbc_raw = proj[:, O_XBC:O_DT]
    acc = sb_ref[...] + sw_ref[pl.ds(K_SSM - 1, 1), :] * xbc_raw
    for k in range(K_SSM - 1):
        row_k = mst_ref[:, k, :]
        acc = acc + sw_ref[pl.ds(k, 1), :] * row_k
        if k >= 1:
            nsc_ref[:, k - 1, :] = row_k
    nsc_ref[:, K_SSM - 2, :] = xbc_raw
    xbc = _silu(acc)
    xs = xbc[:, 0:D_INNER]
    bm = xbc[:, D_INNER:D_INNER + N_GROUPS * D_STATE]
    cm = xbc[:, D_INNER + N_GROUPS * D_STATE:]
    dt = _softplus(proj[:, O_DT:IN_COLS] + dtb_ref[...])
    dec = jnp.exp(dt * (-jnp.exp(alog_ref[...])))
    hexp = hexp_ref[...]
    d_h, d_m, d_l = _split3(dec)
    dec_e = _dot(d_h, hexp) + _dot(d_m, hexp) + _dot(d_l, hexp)
    t_h, t_m, t_l = _split3(dt)
    dt_e = _dot(t_h, hexp) + _dot(t_m, hexp) + _dot(t_l, hexp)
    xdt = xs * dt_e

    lane = lax.broadcasted_iota(jnp.int32, (SUBLANES, D_INNER), 1)
    sub = lax.broadcasted_iota(jnp.int32, (SUBLANES, D_INNER), 0)
    gmask = (lane // GROUP_W) == sub
    ones_row = (lax.broadcasted_iota(jnp.int32, (SUBLANES, D_STATE), 0) < 3).astype(BF16)

    y_rows = []
    for b in range(bb):
        xrow = jnp.where(gmask, jnp.broadcast_to(xdt[b:b + 1, :], (SUBLANES, D_INNER)), 0.0)
        x_h, x_m, x_l = [v.astype(F32) for v in _split3(xrow)]
        bmat = jnp.concatenate([bm[b:b + 1, g * D_STATE:(g + 1) * D_STATE] for g in range(N_GROUPS)]
                               + [jnp.zeros((SUBLANES - N_GROUPS, D_STATE), F32)], axis=0)
        b_h, b_m, b_l = [v.astype(F32) for v in _split3(bmat)]
        lhs = jnp.concatenate([x_h, x_h, x_m, x_h, x_m, x_l], axis=0).astype(BF16)
        rhs = jnp.concatenate([b_h, b_m, b_h, b_l, b_m, b_h], axis=0).astype(BF16)
        upd = _dot_tn(lhs, rhs)
        dbc = jnp.broadcast_to(dec_e[b:b + 1, :], (SUBLANES, D_INNER))
        q_h = dbc.astype(BF16).astype(F32)
        q_m = (dbc - q_h).astype(BF16).astype(F32)
        q_l = dbc - q_h - q_m
        dlhs = jnp.where(sub == 0, q_h, jnp.where(sub == 1, q_m, jnp.where(sub == 2, q_l, 0.0))).astype(BF16)
        dfull = _dot_tn(dlhs, ones_row)
        hnew = dfull * sst_ref[b].reshape(D_INNER, D_STATE) + upd
        nss_ref[b] = hnew.reshape(N_HEADS, HEAD_DIM, D_STATE)
        cmat = jnp.concatenate([cm[b:b + 1, g * D_STATE:(g + 1) * D_STATE] for g in range(N_GROUPS)]
                               + [jnp.zeros((SUBLANES - N_GROUPS, D_STATE), F32)], axis=0)
        yg = _dot_nt(cmat.astype(BF16), hnew.astype(BF16))
        y_rows.append(jnp.sum(jnp.where(gmask, yg, 0.0), axis=0, keepdims=True))
    y = jnp.concatenate(y_rows, axis=0)
    y = (y + dskip_ref[...] * xs) * _silu(z)
    m_parts = []
    for g in range(N_GROUPS):
        m_parts.append(_rms(y[:, g * GROUP_W:(g + 1) * GROUP_W], sng_ref[:, g * GROUP_W:(g + 1) * GROUP_W]))
    mix_scr[pl.ds(r0, bb), :] = jnp.concatenate([a_out] + m_parts, axis=-1)

    @pl.when(i == n - 1)
    def _():
        h1_ref[...] = x_ref[...] + _dot(mix_scr[...].astype(BF16), wout_ref[...])


def _mixer_sample(x, cst, mst, sst, w, bb=8):
    nb = x.shape[0]
    kern = functools.partial(_mixer_sample_kernel, bb=bb)
    consts = [w['g_mix'], w['w_in'], w['conf_w2'], w['conf_b2'], w['ln_g'], w['ln_b'],
              w['ssm_w'], w['ssm_b'], w['dt_bias'], w['a_log'], w['d_skip'], w['ssm_norm_g'], w['w_out'],
              w['head_expand']]
    return pl.pallas_call(
        kern,
        grid=(nb // bb,),
        in_specs=[
            _const_spec((nb, D_MODEL)),
            pl.BlockSpec((K_CONF - 1, bb, C_CONF), lambda i: (0, i, 0)),
            pl.BlockSpec((bb, K_SSM - 1, CONV_DIM), lambda i: (i, 0, 0)),
            pl.BlockSpec((bb, N_HEADS, HEAD_DIM, D_STATE), lambda i: (i, 0, 0, 0)),
        ] + [_const_spec(c.shape) for c in consts],
        out_specs=[
            pl.BlockSpec((nb, D_MODEL), lambda i: (0, 0)),
            pl.BlockSpec((K_CONF - 1, bb, C_CONF), lambda i: (0, i, 0)),
            pl.BlockSpec((bb, K_SSM - 1, CONV_DIM), lambda i: (i, 0, 0)),
            pl.BlockSpec((bb, N_HEADS, HEAD_DIM, D_STATE), lambda i: (i, 0, 0, 0)),
        ],
        out_shape=[
            jax.ShapeDtypeStruct((nb, D_MODEL), F32),
            jax.ShapeDtypeStruct((K_CONF - 1, nb, C_CONF), F32),
            jax.ShapeDtypeStruct((nb, K_SSM - 1, CONV_DIM), F32),
            jax.ShapeDtypeStruct((nb, N_HEADS, HEAD_DIM, D_STATE), F32),
        ],
        scratch_shapes=[
            pltpu.VMEM((nb, IN_COLS), F32),
            pltpu.VMEM((nb, D_MODEL + D_INNER), F32),
        ],
        compiler_params=pltpu.CompilerParams(dimension_semantics=("arbitrary",), vmem_limit_bytes=VMEM_LIMIT),
        name="mixer_sample",
    )(x, cst, mst, sst, *consts)


def _first_argmax(v, width):
    lane = lax.broadcasted_iota(jnp.int32, v.shape, 1)
    m = jnp.max(v, axis=-1, keepdims=True)
    idx = jnp.min(jnp.where(v == m, lane, width), axis=-1, keepdims=True)
    return m, idx


def _ffn_kernel(h_ref, p_ref, gffn_ref, wr_ref, br_ref, wg_ref, wu_ref, wd_ref, gple_ref, wpg_ref, wp_ref,
                gfin_ref, y_ref, t_scr, comb_scr, acc_scr):
    e = pl.program_id(1)
    ne = pl.num_programs(1)

    @pl.when(e == 0)
    def _():
        tf = _rms(h_ref[...], gffn_ref[...])
        t_scr[...] = tf.astype(BF16)
        t_h, t_m, t_l = _split3(tf)
        w_h = wr_ref[0]
        w_m = wr_ref[1]
        logits = (_dot(t_h, w_h) + _dot(t_m, w_h) + _dot(t_h, w_m) + _dot(t_l, w_h) + _dot(t_m, w_m)) + br_ref[...]
        lg = logits[:, 0:N_EXPERT_GROUPS]
        le = logits[:, N_EXPERTS:2 * N_EXPERTS]
        eg = jnp.exp(lg - jnp.max(lg, axis=-1, keepdims=True))
        pg = eg / jnp.sum(eg, axis=-1, keepdims=True)
        g_val, g_idx = _first_argmax(pg, N_EXPERT_GROUPS)
        lane16 = lax.broadcasted_iota(jnp.int32, le.shape, 1)
        in_grp = (lane16 // EXPERTS_PER_GROUP) == g_idx
        neg = jnp.float32(-jnp.inf)
        le_m = jnp.where(in_grp, le, neg)
        ee = jnp.where(in_grp, jnp.exp(le - jnp.max(le_m, axis=-1, keepdims=True)), 0.0)
        pe = ee / jnp.sum(ee, axis=-1, keepdims=True)
        pe_m = jnp.where(in_grp, pe, -1.0)
        v1, i1 = _first_argmax(pe_m, N_EXPERTS)
        pe_m2 = jnp.where(lane16 == i1, -1.0, pe_m)
        v2, i2 = _first_argmax(pe_m2, N_EXPERTS)
        den = v1 + v2
        comb = jnp.where(lane16 == i1, g_val * v1 / den, 0.0) + jnp.where(lane16 == i2, g_val * v2 / den, 0.0)
        comb_scr[...] = comb
        acc_scr[...] = jnp.zeros(acc_scr.shape, F32)

    t = t_scr[...]
    hdn = _silu(_dot(t, wg_ref[0].astype(BF16))) * _dot(t, wu_ref[0].astype(BF16))
    out_e = _dot(hdn.astype(BF16), wd_ref[0].astype(BF16))
    lane16 = lax.broadcasted_iota(jnp.int32, comb_scr.shape, 1)
    c_e = jnp.sum(jnp.where(lane16 == e, comb_scr[...], 0.0), axis=-1, keepdims=True)
    acc_scr[...] += c_e * out_e

    @pl.when(e == ne - 1)
    def _():
        h2 = h_ref[...] + acc_scr[...]
        gate = _sigmoid(_dot(_rms(h2, gple_ref[...]).astype(BF16), wpg_ref[...]))
        h3 = h2 + _dot(p_ref[...].astype(BF16), wp_ref[...]) * gate
        y_ref[...] = _rms(h3, gfin_ref[...])


def _ffn(h, p, w, tb):
    nt = h.shape[0]
    return pl.pallas_call(
        _ffn_kernel,
        grid=(nt // tb, N_EXPERTS),
        in_specs=[
            pl.BlockSpec((tb, D_MODEL), lambda i, e: (i, 0)),
            pl.BlockSpec((tb, PLE_DIM), lambda i, e: (i, 0)),
            _const_spec((1, D_MODEL)),
            _const_spec((2, D_MODEL, 2 * N_EXPERTS)),
            _const_spec((1, 2 * N_EXPERTS)),
            pl.BlockSpec((1, D_MODEL, D_FF), lambda i, e: (e, 0, 0)),
            pl.BlockSpec((1, D_MODEL, D_FF), lambda i, e: (e, 0, 0)),
            pl.BlockSpec((1, D_FF, D_MODEL), lambda i, e: (e, 0, 0)),
            _const_spec((1, D_MODEL)),
            _const_spec((D_MODEL, D_MODEL)),
            _const_spec((PLE_DIM, D_MODEL)),
            _const_spec((1, D_MODEL)),
        ],
        out_specs=pl.BlockSpec((tb, D_MODEL), lambda i, e: (i, 0)),
        out_shape=jax.ShapeDtypeStruct((nt, D_MODEL), F32),
        scratch_shapes=[
            pltpu.VMEM((tb, D_MODEL), BF16),
            pltpu.VMEM((tb, N_EXPERTS), F32),
            pltpu.VMEM((tb, D_MODEL), F32),
        ],
        compiler_params=pltpu.CompilerParams(
            dimension_semantics=("arbitrary", "arbitrary"), vmem_limit_bytes=VMEM_LIMIT),
        name="ffn",
    )(h, p, w['g_ffn'], w['w_router'], w['b_router'], w['w_gate_e'], w['w_up_e'], w['w_down_e'],
      w['g_ple'], w['w_ple_gate'], w['w_ple'], w['g_final'])


ROUTE_TB = 1024
EXPERT_TM = 512
SC_WINDOW = 128
SC_CORES = 2
SC_SUBCORES = 16
SC_WORKERS = SC_CORES * SC_SUBCORES
HALF = D_MODEL // 2


def _pack_bf16_pairs(v):
    bits = pltpu.bitcast(v.astype(BF16).astype(F32), jnp.uint32)
    packed = bits[:, HALF:] | (bits[:, :HALF] >> 16)
    return pltpu.bitcast(packed, jnp.int32)


def _unpack_bf16_pairs(p):
    u = pltpu.bitcast(p, jnp.uint32)
    lo = pltpu.bitcast(u << 16, F32)
    hi = pltpu.bitcast(u & jnp.uint32(0xFFFF0000), F32)
    return jnp.concatenate([lo, hi], axis=-1)


def _route_kernel(h_ref, gffn_ref, wr_ref, br_ref, tp_ref, meta_ref, wts_ref, cnt_ref, lower_scr, carry_scr):
    i = pl.program_id(0)
    tb = h_ref.shape[0]

    @pl.when(i == 0)
    def _():
        r = lax.broadcasted_iota(jnp.int32, (tb, tb), 0)
        c = lax.broadcasted_iota(jnp.int32, (tb, tb), 1)
        lower_scr[...] = (r > c).astype(BF16)
        carry_scr[...] = jnp.zeros(carry_scr.shape, F32)

    tf = _rms(h_ref[...], gffn_ref[...])
    tp_ref[...] = _pack_bf16_pairs(tf)
    t_h = tf.astype(BF16)
    t_m = (tf - t_h.astype(F32)).astype(BF16)
    w_h = wr_ref[0]
    w_m = wr_ref[1]
    logits = (_dot(t_h, w_h) + _dot(t_m, w_h) + _dot(t_h, w_m)) + br_ref[...]
    lg = logits[:, 0:N_EXPERT_GROUPS]
    le = logits[:, N_EXPERTS:2 * N_EXPERTS]
    eg = jnp.exp(lg - jnp.max(lg, axis=-1, keepdims=True))
    pg = eg / jnp.sum(eg, axis=-1, keepdims=True)
    g_val, g_idx = _first_argmax(pg, N_EXPERT_GROUPS)
    lane16 = lax.broadcasted_iota(jnp.int32, le.shape, 1)
    in_grp = (lane16 // EXPERTS_PER_GROUP) == g_idx
    le_m = jnp.where(in_grp, le, -jnp.inf)
    ee = jnp.where(in_grp, jnp.exp(le - jnp.max(le_m, axis=-1, keepdims=True)), 0.0)
    pe = ee / jnp.sum(ee, axis=-1, keepdims=True)
    pe_m = jnp.where(in_grp, pe, -1.0)
    v1, i1 = _first_argmax(pe_m, N_EXPERTS)
    v2, i2 = _first_argmax(jnp.where(lane16 == i1, -1.0, pe_m), N_EXPERTS)
    den = v1 + v2
    w0 = g_val * v1 / den
    w1 = g_val * v2 / den

    sel0 = lane16 == i1
    sel1 = lane16 == i2
    hot = jnp.where(sel0 | sel1, 1.0, 0.0)
    rank = _dot(lower_scr[...], hot.astype(BF16)) + carry_scr[...]
    r0 = jnp.sum(jnp.where(sel0, rank, 0.0), axis=-1, keepdims=True)
    r1 = jnp.sum(jnp.where(sel1, rank, 0.0), axis=-1, keepdims=True)
    carry_scr[...] += jnp.sum(hot, axis=0, keepdims=True)
    cnt_ref[...] = carry_scr[...]

    lane = lax.broadcasted_iota(jnp.int32, (tb, LANES), 1)
    metaf = jnp.where(lane == 0, i1.astype(F32), jnp.where(lane == 1, i2.astype(F32),
                      jnp.where(lane == 2, r0, jnp.where(lane == 3, r1, 0.0))))
    meta_ref[...] = metaf.T[0:SUBLANES, :].astype(jnp.int32)
    wts_ref[...] = jnp.where(lane == 0, w0, jnp.where(lane == 1, w1, 0.0))


def _route(h, w, tb=ROUTE_TB):
    nt = h.shape[0]
    return pl.pallas_call(
        _route_kernel,
        grid=(nt // tb,),
        in_specs=[
            pl.BlockSpec((tb, D_MODEL), lambda i: (i, 0)),
            _const_spec((1, D_MODEL)),
            _const_spec((2, D_MODEL, 2 * N_EXPERTS)),
            _const_spec((1, 2 * N_EXPERTS)),
        ],
        out_specs=[
            pl.BlockSpec((tb, HALF), lambda i: (i, 0)),
            pl.BlockSpec((SUBLANES, tb), lambda i: (0, i)),
            pl.BlockSpec((tb, LANES), lambda i: (i, 0)),
            pl.BlockSpec((1, N_EXPERTS), lambda i: (0, 0)),
        ],
        out_shape=[
            jax.ShapeDtypeStruct((nt, HALF), jnp.int32),
            jax.ShapeDtypeStruct((SUBLANES, nt), jnp.int32),
            jax.ShapeDtypeStruct((nt, LANES), F32),
            jax.ShapeDtypeStruct((1, N_EXPERTS), F32),
        ],
        scratch_shapes=[pltpu.VMEM((tb, tb), BF16), pltpu.VMEM((1, N_EXPERTS), F32)],
        compiler_params=pltpu.CompilerParams(dimension_semantics=("arbitrary",), vmem_limit_bytes=VMEM_LIMIT),
        name="moe_route",
    )(h, w['g_ffn'], w['w_router'], w['b_router'])


def _sc_mesh():
    return plsc.VectorSubcoreMesh(core_axis_name="c", subcore_axis_name="s")


def _sc_dispatch(tp, slot0, slot1, n_slots):
    nt = tp.shape[0]

    per_worker = nt // (SC_WINDOW * SC_WORKERS)

    @pl.kernel(out_type=jax.ShapeDtypeStruct((n_slots, HALF), tp.dtype), mesh=_sc_mesh(), name="moe_dispatch",
               scratch_types=[pltpu.VMEM((1, SC_WINDOW), jnp.int32), pltpu.VMEM((1, SC_WINDOW), jnp.int32),
                              pltpu.VMEM((SC_WINDOW, HALF), tp.dtype)])
    def run(x_hbm, i0_hbm, i1_hbm, o_hbm, i0_v, i1_v, buf_v):
        worker = lax.axis_index("c") * SC_SUBCORES + lax.axis_index("s")

        @pl.loop(0, per_worker)
        def _(k):
            base = (worker * per_worker + k) * SC_WINDOW
            pltpu.sync_copy(i0_hbm.at[:, pl.ds(base, SC_WINDOW)], i0_v)
            pltpu.sync_copy(i1_hbm.at[:, pl.ds(base, SC_WINDOW)], i1_v)
            pltpu.sync_copy(x_hbm.at[pl.ds(base, SC_WINDOW)], buf_v)
            pltpu.sync_copy(buf_v, o_hbm.at[i0_v.at[0]])
            pltpu.sync_copy(buf_v, o_hbm.at[i1_v.at[0]])

    return run(tp, slot0, slot1)


def _sc_combine(ys, slot0, slot1):
    nt = slot0.shape[1]
    out = jax.ShapeDtypeStruct((nt, HALF), ys.dtype)

    per_worker = nt // (SC_WINDOW * SC_WORKERS)

    @pl.kernel(out_type=(out, out), mesh=_sc_mesh(), name="moe_combine",
               scratch_types=[pltpu.VMEM((1, SC_WINDOW), jnp.int32), pltpu.VMEM((1, SC_WINDOW), jnp.int32),
                              pltpu.VMEM((SC_WINDOW, HALF), ys.dtype)])
    def run(y_hbm, i0_hbm, i1_hbm, o0_hbm, o1_hbm, i0_v, i1_v, buf_v):
        worker = lax.axis_index("c") * SC_SUBCORES + lax.axis_index("s")

        @pl.loop(0, per_worker)
        def _(k):
            base = (worker * per_worker + k) * SC_WINDOW
            pltpu.sync_copy(i0_hbm.at[:, pl.ds(base, SC_WINDOW)], i0_v)
            pltpu.sync_copy(i1_hbm.at[:, pl.ds(base, SC_WINDOW)], i1_v)
            pltpu.sync_copy(y_hbm.at[i0_v.at[0]], buf_v)
            pltpu.sync_copy(buf_v, o0_hbm.at[pl.ds(base, SC_WINDOW)])
            pltpu.sync_copy(y_hbm.at[i1_v.at[0]], buf_v)
            pltpu.sync_copy(buf_v, o1_hbm.at[pl.ds(base, SC_WINDOW)])

    return run(ys, slot0, slot1)


def _experts_kernel(te_ref, nv_ref, x_ref, wg_ref, wu_ref, wd_ref, y_ref, wg_s, wu_s, wd_s):
    j = pl.program_id(0)

    @pl.when((j == 0) | (te_ref[j] != te_ref[jnp.maximum(j - 1, 0)]))
    def _():
        wg_s[...] = wg_ref[0].astype(BF16)
        wu_s[...] = wu_ref[0].astype(BF16)
        wd_s[...] = wd_ref[0].astype(BF16)

    @pl.when(j < nv_ref[0])
    def _():
        x = _unpack_bf16_pairs(x_ref[...]).astype(BF16)
        hdn = _silu(_dot(x, wg_s[...])) * _dot(x, wu_s[...])
        y_ref[...] = _pack_bf16_pairs(_dot(hdn.astype(BF16), wd_s[...]))


def _experts(xs, tile_expert, n_valid, w, tm=EXPERT_TM):
    n_slots = xs.shape[0]
    grid_spec = pltpu.PrefetchScalarGridSpec(
        num_scalar_prefetch=2,
        grid=(n_slots // tm,),
        in_specs=[
            pl.BlockSpec((tm, HALF), lambda j, te, nv: (j, 0)),
            pl.BlockSpec((1, D_MODEL, D_FF), lambda j, te, nv: (te[j], 0, 0)),
            pl.BlockSpec((1, D_MODEL, D_FF), lambda j, te, nv: (te[j], 0, 0)),
            pl.BlockSpec((1, D_FF, D_MODEL), lambda j, te, nv: (te[j], 0, 0)),
        ],
        out_specs=pl.BlockSpec((tm, HALF), lambda j, te, nv: (j, 0)),
        scratch_shapes=[pltpu.VMEM((D_MODEL, D_FF), BF16), pltpu.VMEM((D_MODEL, D_FF), BF16),
                        pltpu.VMEM((D_FF, D_MODEL), BF16)],
    )
    return pl.pallas_call(
        _experts_kernel,
        grid_spec=grid_spec,
        out_shape=jax.ShapeDtypeStruct((n_slots, HALF), jnp.int32),
        compiler_params=pltpu.CompilerParams(dimension_semantics=("arbitrary",), vmem_limit_bytes=VMEM_LIMIT),
        name="moe_experts",
    )(tile_expert, n_valid, xs, w['w_gate_e'], w['w_up_e'], w['w_down_e'])


def _finish_kernel(h_ref, y0_ref, y1_ref, wts_ref, p_ref, gple_ref, wpg_ref, wp_ref, gfin_ref, o_ref):
    wts = wts_ref[...]
    moe = wts[:, 0:1] * _unpack_bf16_pairs(y0_ref[...]) + wts[:, 1:2] * _unpack_bf16_pairs(y1_ref[...])
    h2 = h_ref[...] + moe
    gate = _sigmoid(_dot(_rms(h2, gple_ref[...]).astype(BF16), wpg_ref[...]))
    h3 = h2 + _dot(p_ref[...].astype(BF16), wp_ref[...]) * gate
    o_ref[...] = _rms(h3, gfin_ref[...])


def _finish(h, y0, y1, wts, p, w, tb=1024):
    nt = h.shape[0]
    return pl.pallas_call(
        _finish_kernel,
        grid=(nt // tb,),
        in_specs=[
            pl.BlockSpec((tb, D_MODEL), lambda i: (i, 0)),
            pl.BlockSpec((tb, HALF), lambda i: (i, 0)),
            pl.BlockSpec((tb, HALF), lambda i: (i, 0)),
            pl.BlockSpec((tb, LANES), lambda i: (i, 0)),
            pl.BlockSpec((tb, PLE_DIM), lambda i: (i, 0)),
            _const_spec((1, D_MODEL)),
            _const_spec((D_MODEL, D_MODEL)),
            _const_spec((PLE_DIM, D_MODEL)),
            _const_spec((1, D_MODEL)),
        ],
        out_specs=pl.BlockSpec((tb, D_MODEL), lambda i: (i, 0)),
        out_shape=jax.ShapeDtypeStruct((nt, D_MODEL), F32),
        compiler_params=pltpu.CompilerParams(dimension_semantics=("arbitrary",), vmem_limit_bytes=VMEM_LIMIT),
        name="moe_finish",
    )(h, y0, y1, wts, p, w['g_ple'], w['w_ple_gate'], w['w_ple'], w['g_final'])


def _ffn_sparse(h, p, w):
    nt = h.shape[0]
    tm = EXPERT_TM
    n_tiles = (nt * 2) // tm + N_EXPERTS
    tp, meta, wts, counts = _route(h, w)
    cnt = counts.reshape(N_EXPERTS).astype(jnp.int32)
    tiles_e = (cnt + tm - 1) // tm
    tile_end = jnp.cumsum(tiles_e)
    off = (tile_end - tiles_e) * tm
    n_valid = tile_end[-1:]
    tile_ids = jnp.arange(n_tiles, dtype=jnp.int32)
    tile_expert = jnp.sum((tile_ids[:, None] >= tile_end[None, :]).astype(jnp.int32), axis=1)
    last_expert = jnp.sum((n_valid - 1 >= tile_end).astype(jnp.int32))
    tile_expert = jnp.minimum(tile_expert, last_expert).astype(jnp.int32)
    eid = meta[0:2]
    slot = meta[2:4] + jnp.sum(jnp.where(eid[..., None] == jnp.arange(N_EXPERTS), off, 0), axis=-1)
    slot0 = slot[0:1]
    slot1 = slot[1:2]
    xs = _sc_dispatch(tp, slot0, slot1, n_tiles * tm)
    ys = _experts(xs, tile_expert, n_valid.astype(jnp.int32), w)
    y0, y1 = _sc_combine(ys, slot0, slot1)
    return _finish(h, y0, y1, wts, p, w)


def _prep_weights(g_mix, w_in, conf_dw_w, conf_dw_b, conf_ln_g, conf_ln_b, ssm_conv_w, ssm_conv_b,
                  dt_bias, a_log, d_skip, ssm_norm_g, w_out, g_ffn, w_rg, b_rg, w_re, b_re,
                  w_gate_e, w_up_e, w_down_e, g_ple, w_ple_gate, w_ple, g_final):
    row = lambda v: v.reshape(1, -1)
    w_in_b = w_in.astype(BF16)
    conf_w_pad = jnp.concatenate([conf_dw_w, jnp.zeros((CONF_PAD - K_CONF, C_CONF), F32)], axis=0)
    w_router = jnp.concatenate(
        [w_rg, jnp.zeros((D_MODEL, N_EXPERTS - N_EXPERT_GROUPS), F32), w_re], axis=1)
    wr_hi = w_router.astype(BF16)
    wr_mid = (w_router - wr_hi.astype(F32)).astype(BF16)
    b_router = jnp.concatenate([b_rg, jnp.zeros((N_EXPERTS - N_EXPERT_GROUPS,), F32), b_re]).reshape(1, -1)
    head_of_lane = jnp.arange(D_INNER) // HEAD_DIM
    head_expand = (head_of_lane[None, :] == jnp.arange(N_HEADS)[:, None]).astype(BF16)
    return dict(
        g_mix=row(g_mix), w_in=w_in_b, w_dtT=w_in_b[:, O_DT:].T,
        conf_w=conf_w_pad.reshape(CONF_PAD, N_LANE_TILES, LANES).transpose(1, 0, 2),
        conf_b=conf_dw_b.reshape(N_LANE_TILES, 1, LANES),
        conf_w2=conf_dw_w, conf_b2=row(conf_dw_b),
        ssm_w3=ssm_conv_w.reshape(K_SSM, N_XBC_TILES, LANES).transpose(1, 0, 2),
        ssm_b3=ssm_conv_b.reshape(N_XBC_TILES, 1, LANES),
        ln_g=row(conf_ln_g), ln_b=row(conf_ln_b), ssm_w=ssm_conv_w, ssm_b=row(ssm_conv_b),
        dt_bias=row(dt_bias), dt_biasT=dt_bias.reshape(-1, 1), a_log=row(a_log), a_logT=a_log.reshape(-1, 1),
        d_skip=row(jnp.repeat(d_skip, HEAD_DIM)), ssm_norm_g=row(ssm_norm_g), w_out=w_out.astype(BF16),
        head_expand=head_expand,
        g_ffn=row(g_ffn), w_router=jnp.stack([wr_hi, wr_mid]), b_router=b_router,
        w_gate_e=w_gate_e, w_up_e=w_up_e, w_down_e=w_down_e,
        g_ple=row(g_ple), w_ple_gate=w_ple_gate.astype(BF16), w_ple=w_ple.astype(BF16), g_final=row(g_final),
    )


def kernel(x_prompt, x_sample, p_prompt, p_sample, state_conf_conv, state_ssm_conv, state_ssm, g_mix, w_in, conf_dw_w, conf_dw_b, conf_ln_g, conf_ln_b, ssm_conv_w, ssm_conv_b, dt_bias, a_log, d_skip, ssm_norm_g, w_out, g_ffn, w_rg, b_rg, w_re, b_re, w_gate_e, w_up_e, w_down_e, g_ple, w_ple_gate, w_ple, g_final):
    depth = g_mix.shape[0]
    bsz, seq, _ = x_prompt.shape
    nb = x_sample.shape[0]
    hp = x_prompt
    hs = x_sample.reshape(nb, D_MODEL)
    cp_l, mp_l, sp_l, cs_l, ms_l, ss_l = [], [], [], [], [], []
    for i in range(depth):
        w = _prep_weights(g_mix[i], w_in[i], conf_dw_w[i], conf_dw_b[i], conf_ln_g[i], conf_ln_b[i],
                          ssm_conv_w[i], ssm_conv_b[i], dt_bias[i], a_log[i], d_skip[i], ssm_norm_g[i], w_out[i],
                          g_ffn[i], w_rg[i], b_rg[i], w_re[i], b_re[i], w_gate_e[i], w_up_e[i], w_down_e[i],
                          g_ple[i], w_ple_gate[i], w_ple[i], g_final)
        h1p, c, m, s = _mixer_prompt(hp, w)
        cp_l.append(c)
        mp_l.append(m)
        sp_l.append(s)
        h1s, c, m, s = _mixer_sample(hs, jnp.transpose(state_conf_conv[i], (1, 0, 2)), state_ssm_conv[i],
                                     state_ssm[i], w)
        cs_l.append(jnp.transpose(c, (1, 0, 2)))
        ms_l.append(m)
        ss_l.append(s)
        assert depth == 1
        hp = _ffn_sparse(h1p.reshape(bsz * seq, D_MODEL), p_prompt[i].reshape(bsz * seq, PLE_DIM), w)
        hp = hp.reshape(bsz, seq, D_MODEL)
        hs = _ffn(h1s, p_sample[i].reshape(nb, PLE_DIM), w, tb=nb)
    return (hp, hs.reshape(nb, 1, D_MODEL), jnp.stack(cp_l), jnp.stack(mp_l), jnp.stack(sp_l),
            jnp.stack(cs_l), jnp.stack(ms_l), jnp.stack(ss_l))
```

```python
import functools

import jax
import jax.numpy as jnp
from jax import lax
from jax.experimental import pallas as pl
from jax.experimental.pallas import tpu as pltpu
from jax.experimental.pallas import tpu_sc as plsc

F32 = jnp.float32
BF16 = jnp.bfloat16

D_MODEL = 1024
C_CONF = 1024
K_CONF = 31
D_INNER = 1024
HEAD_DIM = 64
N_HEADS = 16
N_GROUPS = 4
HEADS_PER_GROUP = N_HEADS // N_GROUPS
GROUP_W = HEADS_PER_GROUP * HEAD_DIM
D_STATE = 128
K_SSM = 4
CHUNK = 128
CONV_DIM = D_INNER + 2 * N_GROUPS * D_STATE
IN_COLS = 2 * C_CONF + D_INNER + CONV_DIM + N_HEADS
O_GATE = C_CONF
O_Z = 2 * C_CONF
O_XBC = O_Z + D_INNER
O_DT = O_XBC + CONV_DIM
N_EXPERT_GROUPS = 4
EXPERTS_PER_GROUP = 4
N_EXPERTS = 16
D_FF = 512
PLE_DIM = 256
EPS = 1e-6

LANES = 128
SUBLANES = 8
N_LANE_TILES = C_CONF // LANES
N_XBC_TILES = CONV_DIM // LANES
CONF_PAD = 32
SSM_PAD = 8
VMEM_LIMIT = 56 * 1024 * 1024


def _dot(a, b):
    return jnp.dot(a, b, preferred_element_type=F32)


def _dot_nt(a, b):
    return lax.dot_general(a, b, (((1,), (1,)), ((), ())), preferred_element_type=F32)


def _dot_tn(a, b):
    return lax.dot_general(a, b, (((0,), (0,)), ((), ())), preferred_element_type=F32)


def _split3(v):
    hi = v.astype(BF16)
    r = v - hi.astype(F32)
    mid = r.astype(BF16)
    lo = (r - mid.astype(F32)).astype(BF16)
    return hi, mid, lo


def _rms(x, g):
    return x * lax.rsqrt(jnp.mean(x * x, axis=-1, keepdims=True) + EPS) * g


def _sigmoid(x):
    return jax.nn.sigmoid(x)


def _silu(x):
    return x * jax.nn.sigmoid(x)


def _softplus(x):
    return jax.nn.softplus(x)


def _mixer_prompt_kernel(x_ref, gmix_ref, win_ref, wdtT_ref, cw_ref, cb_ref, lng_ref, lnb_ref,
                         sw_ref, sb_ref, dtb_ref, dtbT_ref, alog_ref, alogT_ref, dskip_ref, sng_ref,
                         wout_ref, hexp_ref,
                         h1_ref, ncc_ref, nsc_ref, nss_ref,
                         cscr, cout, mscr, xbc_scr, st_scr, *, tl):
    t = pl.program_id(1)
    nt = pl.num_programs(1)

    @pl.when(t == 0)
    def _():
        cscr[:, 0:CONF_PAD, :] = jnp.zeros((N_LANE_TILES, CONF_PAD, LANES), F32)
        mscr[:, 0:SSM_PAD, :] = jnp.zeros((N_XBC_TILES, SSM_PAD, LANES), F32)
        st_scr[...] = jnp.zeros(st_scr.shape, F32)

    x = x_ref[0]
    u = _rms(x, gmix_ref[...]).astype(BF16)

    glu = _dot(u, win_ref[:, 0:O_GATE]) * _sigmoid(_dot(u, win_ref[:, O_GATE:O_Z]))
    for lc in range(N_LANE_TILES):
        cscr[lc, CONF_PAD:CONF_PAD + tl, :] = glu[:, lc * LANES:(lc + 1) * LANES]

    rc = 64

    def conv_lane_tile(lc, carry):
        bias = cb_ref[lc]
        for r0 in range(0, tl, rc):
            acc = jnp.broadcast_to(bias, (rc, LANES))
            for k in range(K_CONF):
                acc = acc + cw_ref[lc, pl.ds(k, 1), :] * cscr[lc, pl.ds(r0 + k + CONF_PAD - (K_CONF - 1), rc), :]
            cout[lc, pl.ds(r0, rc), :] = acc
        return carry

    for lc in range(N_LANE_TILES):
        conv_lane_tile(lc, 0)

    for lc in range(N_LANE_TILES):
        cscr[lc, 0:CONF_PAD, :] = cscr[lc, tl:tl + CONF_PAD, :]

    cc = [cout[lc] for lc in range(N_LANE_TILES)]
    tot = cc[0]
    for lc in range(1, N_LANE_TILES):
        tot = tot + cc[lc]
    mean = jnp.sum(tot, axis=-1, keepdims=True) * (1.0 / C_CONF)
    xc = [c - mean for c in cc]
    sq = xc[0] * xc[0]
    for lc in range(1, N_LANE_TILES):
        sq = sq + xc[lc] * xc[lc]
    rstd = lax.rsqrt(jnp.sum(sq, axis=-1, keepdims=True) * (1.0 / C_CONF) + EPS)
    a_out = jnp.concatenate(
        [_silu(xc[lc] * rstd * lng_ref[:, lc * LANES:(lc + 1) * LANES] + lnb_ref[:, lc * LANES:(lc + 1) * LANES])
         for lc in range(N_LANE_TILES)], axis=-1).astype(BF16)

    z = _dot(u, win_ref[:, O_Z:O_XBC])
    xbc_raw = _dot(u, win_ref[:, O_XBC:O_DT])
    for lt in range(N_XBC_TILES):
        mscr[lt, SSM_PAD:SSM_PAD + tl, :] = xbc_raw[:, lt * LANES:(lt + 1) * LANES]

    def ssm_conv_lane_tile(lt, carry):
        bias = sb_ref[lt]
        for r0 in range(0, tl, rc):
            acc = jnp.broadcast_to(bias, (rc, LANES))
            for k in range(K_SSM):
                acc = acc + sw_ref[lt, pl.ds(k, 1), :] * mscr[lt, pl.ds(r0 + k + SSM_PAD - (K_SSM - 1), rc), :]
            xbc_scr[lt, pl.ds(r0, rc), :] = _silu(acc)
        mscr[lt, 0:SSM_PAD, :] = mscr[lt, tl:tl + SSM_PAD, :]
        return carry

    for lt in range(N_XBC_TILES):
        ssm_conv_lane_tile(lt, 0)

    n_x = D_INNER // LANES
    n_b = N_GROUPS * D_STATE // LANES
    xs = jnp.concatenate([xbc_scr[lt] for lt in range(n_x)], axis=-1)
    bm = jnp.concatenate([xbc_scr[lt] for lt in range(n_x, n_x + n_b)], axis=-1)
    cm = jnp.concatenate([xbc_scr[lt] for lt in range(n_x + n_b, N_XBC_TILES)], axis=-1)

    dt = _softplus(_dot(u, win_ref[:, O_DT:IN_COLS]) + dtb_ref[...])
    dtT = _softplus(_dot_nt(wdtT_ref[...], u) + dtbT_ref[...])
    a = dt * (-jnp.exp(alog_ref[...]))
    aT = dtT * (-jnp.exp(alogT_ref[...]))
    hexp = hexp_ref[...]
    d_h, d_m, d_l = _split3(dt)
    xdt_all = xs * (_dot(d_h, hexp) + _dot(d_m, hexp) + _dot(d_l, hexp))

    row = lax.broadcasted_iota(jnp.int32, (CHUNK, CHUNK), 0)
    col = lax.broadcasted_iota(jnp.int32, (CHUNK, CHUNK), 1)
    lower = row >= col
    tri = lower.astype(BF16)
    triT = (row <= col).astype(BF16)

    y_chunks = []
    for c in range(tl // CHUNK):
        r0 = c * CHUNK
        a_c = a[r0:r0 + CHUNK]
        aT_c = aT[:, r0:r0 + CHUNK]
        ah, am, al = _split3(a_c)
        cs = _dot(tri, ah) + _dot(tri, am) + _dot(tri, al)
        th, tm, tlo = _split3(aT_c)
        csT = _dot(th, triT) + _dot(tm, triT) + _dot(tlo, triT)
        cs_last = cs[CHUNK - 1:CHUNK, :]
        cdec = jnp.exp(cs_last)
        xdt_c = xdt_all[r0:r0 + CHUNK]
        y_heads = []
        for g in range(N_GROUPS):
            cg = cm[r0:r0 + CHUNK, g * D_STATE:(g + 1) * D_STATE].astype(BF16)
            bg = bm[r0:r0 + CHUNK, g * D_STATE:(g + 1) * D_STATE].astype(BF16)
            cb = _dot_nt(cg, bg)
            y_off = _dot(cg, st_scr[g].astype(BF16))
            xdd = []
            dec_row = []
            m_parts_g = []
            x_bd = []
            e_parts = []
            xdt_g = xdt_c[:, g * GROUP_W:(g + 1) * GROUP_W]
            lane_head = lax.broadcasted_iota(jnp.int32, (CHUNK, GROUP_W), 1) // HEAD_DIM
            for hh in range(HEADS_PER_GROUP):
                h = g * HEADS_PER_GROUP + hh
                xdt = xdt_g[:, hh * HEAD_DIM:(hh + 1) * HEAD_DIM]
                cs_b = jnp.broadcast_to(cs[:, h:h + 1], (CHUNK, CHUNK))
                lmat = jnp.where(lower, jnp.exp(cs_b - csT[h:h + 1, :]), 0.0)
                m_parts_g.append((cb * lmat).astype(BF16))
                x_bd.append(jnp.where(lane_head == hh, xdt_g, 0.0).astype(BF16))
                cs_bh = cs_b[:, 0:HEAD_DIM]
                e_parts.append(jnp.exp(cs_bh))
                xdd.append((xdt * jnp.exp(csT[h:h + 1, CHUNK - 1:CHUNK] - cs_bh)).astype(BF16))
                dec_row.append(jnp.broadcast_to(cdec[:, h:h + 1], (1, HEAD_DIM)))
            y_diag = _dot(jnp.concatenate(m_parts_g, axis=1), jnp.concatenate(x_bd, axis=0))
            y_heads.append(y_diag + y_off * jnp.concatenate(e_parts, axis=-1))
            contrib = _dot_tn(bg, jnp.concatenate(xdd, axis=-1))
            st_scr[g] = st_scr[g] * jnp.concatenate(dec_row, axis=-1) + contrib
        y_chunks.append(jnp.concatenate(y_heads, axis=-1))
    y = y_chunks[0] if len(y_chunks) == 1 else jnp.concatenate(y_chunks, axis=0)
    y = (y + dskip_ref[...] * xs) * _silu(z)
    m_parts = []
    for g in range(N_GROUPS):
        yg = y[:, g * GROUP_W:(g + 1) * GROUP_W]
        m_parts.append(_rms(yg, sng_ref[:, g * GROUP_W:(g + 1) * GROUP_W]))
    m_out = jnp.concatenate(m_parts, axis=-1).astype(BF16)

    h1_ref[0] = x + _dot(a_out, wout_ref[0:C_CONF, :]) + _dot(m_out, wout_ref[C_CONF:, :])

    @pl.when(t == nt - 1)
    def _():
        for lc in range(N_LANE_TILES):
            ncc_ref[0, :, lc * LANES:(lc + 1) * LANES] = cscr[lc, pl.ds(CONF_PAD - (K_CONF - 1), K_CONF - 1), :]
        for lt in range(N_XBC_TILES):
            nsc_ref[0, :, lt * LANES:(lt + 1) * LANES] = mscr[lt, pl.ds(SSM_PAD - (K_SSM - 1), K_SSM - 1), :]
        for g in range(N_GROUPS):
            nss_ref[0, g * HEADS_PER_GROUP:(g + 1) * HEADS_PER_GROUP] = (
                st_scr[g].T.reshape(HEADS_PER_GROUP, HEAD_DIM, D_STATE))


def _const_spec(shape):
    nd = len(shape)
    return pl.BlockSpec(shape, lambda *_: (0,) * nd, pipeline_mode=pl.Buffered(1))


def _mixer_prompt(x, w, tl=256):
    bsz, seq, _ = x.shape
    kern = functools.partial(_mixer_prompt_kernel, tl=tl)
    consts = [w['g_mix'], w['w_in'], w['w_dtT'], w['conf_w'], w['conf_b'], w['ln_g'], w['ln_b'],
              w['ssm_w3'], w['ssm_b3'], w['dt_bias'], w['dt_biasT'], w['a_log'], w['a_logT'], w['d_skip'],
              w['ssm_norm_g'], w['w_out'], w['head_expand']]
    return pl.pallas_call(
        kern,
        grid=(bsz, seq // tl),
        in_specs=[pl.BlockSpec((1, tl, D_MODEL), lambda b, t: (b, t, 0))] + [_const_spec(c.shape) for c in consts],
        out_specs=[
            pl.BlockSpec((1, tl, D_MODEL), lambda b, t: (b, t, 0)),
            pl.BlockSpec((1, K_CONF - 1, C_CONF), lambda b, t: (b, 0, 0)),
            pl.BlockSpec((1, K_SSM - 1, CONV_DIM), lambda b, t: (b, 0, 0)),
            pl.BlockSpec((1, N_HEADS, HEAD_DIM, D_STATE), lambda b, t: (b, 0, 0, 0)),
        ],
        out_shape=[
            jax.ShapeDtypeStruct((bsz, seq, D_MODEL), F32),
            jax.ShapeDtypeStruct((bsz, K_CONF - 1, C_CONF), F32),
            jax.ShapeDtypeStruct((bsz, K_SSM - 1, CONV_DIM), F32),
            jax.ShapeDtypeStruct((bsz, N_HEADS, HEAD_DIM, D_STATE), F32),
        ],
        scratch_shapes=[
            pltpu.VMEM((N_LANE_TILES, CONF_PAD + tl, LANES), F32),
            pltpu.VMEM((N_LANE_TILES, tl, LANES), F32),
            pltpu.VMEM((N_XBC_TILES, SSM_PAD + tl, LANES), F32),
            pltpu.VMEM((N_XBC_TILES, tl, LANES), F32),
            pltpu.VMEM((N_GROUPS, D_STATE, GROUP_W), F32),
        ],
        compiler_params=pltpu.CompilerParams(
            dimension_semantics=("arbitrary", "arbitrary"), vmem_limit_bytes=VMEM_LIMIT),
        name="mixer_prompt",
    )(x, *consts)


def _mixer_sample_kernel(x_ref, cst_ref, mst_ref, sst_ref, gmix_ref, win_ref, cw_ref, cb_ref, lng_ref, lnb_ref,
                         sw_ref, sb_ref, dtb_ref, alog_ref, dskip_ref, sng_ref, wout_ref, hexp_ref,
                         h1_ref, ncc_ref, nsc_ref, nss_ref,
                         proj_scr, mix_scr, *, bb):
    i = pl.program_id(0)
    n = pl.num_programs(0)

    @pl.when(i == 0)
    def _():
        u = _rms(x_ref[...], gmix_ref[...]).astype(BF16)
        proj_scr[...] = _dot(u, win_ref[...])

    r0 = pl.multiple_of(i * bb, bb)
    proj = proj_scr[pl.ds(r0, bb), :]

    glu = proj[:, 0:O_GATE] * _sigmoid(proj[:, O_GATE:O_Z])
    acc = cb_ref[...] + cw_ref[pl.ds(K_CONF - 1, 1), :] * glu
    for k in range(K_CONF - 1):
        row_k = cst_ref[k]
        acc = acc + cw_ref[pl.ds(k, 1), :] * row_k
        if k >= 1:
            ncc_ref[k - 1] = row_k
    ncc_ref[K_CONF - 2] = glu
    mean = jnp.mean(acc, axis=-1, keepdims=True)
    xc = acc - mean
    rstd = lax.rsqrt(jnp.mean(xc * xc, axis=-1, keepdims=True) + EPS)
    a_out = _silu(xc * rstd * lng_ref[...] + lnb_ref[...])

    z = proj[:, O_Z:O_XBC]
    xbc_raw = proj[:, O_XBC:O_DT]
    acc = sb_ref[...] + sw_ref[pl.ds(K_SSM - 1, 1), :] * xbc_raw
    for k in range(K_SSM - 1):
        row_k = mst_ref[:, k, :]
        acc = acc + sw_ref[pl.ds(k, 1), :] * row_k
        if k >= 1:
            nsc_ref[:, k - 1, :] = row_k
    nsc_ref[:, K_SSM - 2, :] = xbc_raw
    xbc = _silu(acc)
    xs = xbc[:, 0:D_INNER]
    bm = xbc[:, D_INNER:D_INNER + N_GROUPS * D_STATE]
    cm = xbc[:, D_INNER + N_GROUPS * D_STATE:]
    dt = _softplus(proj[:, O_DT:IN_COLS] + dtb_ref[...])
    dec = jnp.exp(dt * (-jnp.exp(alog_ref[...])))
    hexp = hexp_ref[...]
    d_h, d_m, d_l = _split3(dec)
    dec_e = _dot(d_h, hexp) + _dot(d_m, hexp) + _dot(d_l, hexp)
    t_h, t_m, t_l = _split3(dt)
    dt_e = _dot(t_h, hexp) + _dot(t_m, hexp) + _dot(t_l, hexp)
    xdt = xs * dt_e

    lane = lax.broadcasted_iota(jnp.int32, (SUBLANES, D_INNER), 1)
    sub = lax.broadcasted_iota(jnp.int32, (SUBLANES, D_INNER), 0)
    gmask = (lane // GROUP_W) == sub
    ones_row = (lax.broadcasted_iota(jnp.int32, (SUBLANES, D_STATE), 0) < 3).astype(BF16)

    y_rows = []
    for b in range(bb):
        xrow = jnp.where(gmask, jnp.broadcast_to(xdt[b:b + 1, :], (SUBLANES, D_INNER)), 0.0)
        x_h, x_m, x_l = [v.astype(F32) for v in _split3(xrow)]
        bmat = jnp.concatenate([bm[b:b + 1, g * D_STATE:(g + 1) * D_STATE] for g in range(N_GROUPS)]
                               + [jnp.zeros((SUBLANES - N_GROUPS, D_STATE), F32)], axis=0)
        b_h, b_m, b_l = [v.astype(F32) for v in _split3(bmat)]
        lhs = jnp.concatenate([x_h, x_h, x_m, x_h, x_m, x_l], axis=0).astype(BF16)
        rhs = jnp.concatenate([b_h, b_m, b_h, b_l, b_m, b_h], axis=0).astype(BF16)
        upd = _dot_tn(lhs, rhs)
        dbc = jnp.broadcast_to(dec_e[b:b + 1, :], (SUBLANES, D_INNER))
        q_h = dbc.astype(BF16).astype(F32)
        q_m = (dbc - q_h).astype(BF16).astype(F32)
        q_l = dbc - q_h - q_m
        dlhs = jnp.where(sub == 0, q_h, jnp.where(sub == 1, q_m, jnp.where(sub == 2, q_l, 0.0))).astype(BF16)
        dfull = _dot_tn(dlhs, ones_row)
        hnew = dfull * sst_ref[b].reshape(D_INNER, D_STATE) + upd
        nss_ref[b] = hnew.reshape(N_HEADS, HEAD_DIM, D_STATE)
        cmat = jnp.concatenate([cm[b:b + 1, g * D_STATE:(g + 1) * D_STATE] for g in range(N_GROUPS)]
                               + [jnp.zeros((SUBLANES - N_GROUPS, D_STATE), F32)], axis=0)
        yg = _dot_nt(cmat.astype(BF16), hnew.astype(BF16))
        y_rows.append(jnp.sum(jnp.where(gmask, yg, 0.0), axis=0, keepdims=True))
    y = jnp.concatenate(y_rows, axis=0)
    y = (y + dskip_ref[...] * xs) * _silu(z)
    m_parts = []
    for g in range(N_GROUPS):
        m_parts.append(_rms(y[:, g * GROUP_W:(g + 1) * GROUP_W], sng_ref[:, g * GROUP_W:(g + 1) * GROUP_W]))
    mix_scr[pl.ds(r0, bb), :] = jnp.concatenate([a_out] + m_parts, axis=-1)

    @pl.when(i == n - 1)
    def _():
        h1_ref[...] = x_ref[...] + _dot(mix_scr[...].astype(BF16), wout_ref[...])


def _mixer_sample(x, cst, mst, sst, w, bb=8):
    nb = x.shape[0]
    kern = functools.partial(_mixer_sample_kernel, bb=bb)
    consts = [w['g_mix'], w['w_in'], w['conf_w2'], w['conf_b2'], w['ln_g'], w['ln_b'],
              w['ssm_w'], w['ssm_b'], w['dt_bias'], w['a_log'], w['d_skip'], w['ssm_norm_g'], w['w_out'],
              w['head_expand']]
    return pl.pallas_call(
        kern,
        grid=(nb // bb,),
        in_specs=[
            _const_spec((nb, D_MODEL)),
            pl.BlockSpec((K_CONF - 1, bb, C_CONF), lambda i: (0, i, 0)),
            pl.BlockSpec((bb, K_SSM - 1, CONV_DIM), lambda i: (i, 0, 0)),
            pl.BlockSpec((bb, N_HEADS, HEAD_DIM, D_STATE), lambda i: (i, 0, 0, 0)),
        ] + [_const_spec(c.shape) for c in consts],
        out_specs=[
            pl.BlockSpec((nb, D_MODEL), lambda i: (0, 0)),
            pl.BlockSpec((K_CONF - 1, bb, C_CONF), lambda i: (0, i, 0)),
            pl.BlockSpec((bb, K_SSM - 1, CONV_DIM), lambda i: (i, 0, 0)),
            pl.BlockSpec((bb, N_HEADS, HEAD_DIM, D_STATE), lambda i: (i, 0, 0, 0)),
        ],
        out_shape=[
            jax.ShapeDtypeStruct((nb, D_MODEL), F32),
            jax.ShapeDtypeStruct((K_CONF - 1, nb, C_CONF), F32),
            jax.ShapeDtypeStruct((nb, K_SSM - 1, CONV_DIM), F32),
            jax.ShapeDtypeStruct((nb, N_HEADS, HEAD_DIM, D_STATE), F32),
        ],
        scratch_shapes=[
            pltpu.VMEM((nb, IN_COLS), F32),
            pltpu.VMEM((nb, D_MODEL + D_INNER), F32),
        ],
        compiler_params=pltpu.CompilerParams(dimension_semantics=("arbitrary",), vmem_limit_bytes=VMEM_LIMIT),
        name="mixer_sample",
    )(x, cst, mst, sst, *consts)


def _first_argmax(v, width):
    lane = lax.broadcasted_iota(jnp.int32, v.shape, 1)
    m = jnp.max(v, axis=-1, keepdims=True)
    idx = jnp.min(jnp.where(v == m, lane, width), axis=-1, keepdims=True)
    return m, idx


def _ffn_kernel(h_ref, p_ref, gffn_ref, wr_ref, br_ref, wg_ref, wu_ref, wd_ref, gple_ref, wpg_ref, wp_ref,
                gfin_ref, y_ref, t_scr, comb_scr, acc_scr):
    e = pl.program_id(1)
    ne = pl.num_programs(1)

    @pl.when(e == 0)
    def _():
        tf = _rms(h_ref[...], gffn_ref[...])
        t_scr[...] = tf.astype(BF16)
        t_h, t_m, t_l = _split3(tf)
        w_h = wr_ref[0]
        w_m = wr_ref[1]
        logits = (_dot(t_h, w_h) + _dot(t_m, w_h) + _dot(t_h, w_m) + _dot(t_l, w_h) + _dot(t_m, w_m)) + br_ref[...]
        lg = logits[:, 0:N_EXPERT_GROUPS]
        le = logits[:, N_EXPERTS:2 * N_EXPERTS]
        eg = jnp.exp(lg - jnp.max(lg, axis=-1, keepdims=True))
        pg = eg / jnp.sum(eg, axis=-1, keepdims=True)
        g_val, g_idx = _first_argmax(pg, N_EXPERT_GROUPS)
        lane16 = lax.broadcasted_iota(jnp.int32, le.shape, 1)
        in_grp = (lane16 // EXPERTS_PER_GROUP) == g_idx
        neg = jnp.float32(-jnp.inf)
        le_m = jnp.where(in_grp, le, neg)
        ee = jnp.where(in_grp, jnp.exp(le - jnp.max(le_m, axis=-1, keepdims=True)), 0.0)
        pe = ee / jnp.sum(ee, axis=-1, keepdims=True)
        pe_m = jnp.where(in_grp, pe, -1.0)
        v1, i1 = _first_argmax(pe_m, N_EXPERTS)
        pe_m2 = jnp.where(lane16 == i1, -1.0, pe_m)
        v2, i2 = _first_argmax(pe_m2, N_EXPERTS)
        den = v1 + v2
        comb = jnp.where(lane16 == i1, g_val * v1 / den, 0.0) + jnp.where(lane16 == i2, g_val * v2 / den, 0.0)
        comb_scr[...] = comb
        acc_scr[...] = jnp.zeros(acc_scr.shape, F32)

    t = t_scr[...]
    hdn = _silu(_dot(t, wg_ref[0].astype(BF16))) * _dot(t, wu_ref[0].astype(BF16))
    out_e = _dot(hdn.astype(BF16), wd_ref[0].astype(BF16))
    lane16 = lax.broadcasted_iota(jnp.int32, comb_scr.shape, 1)
    c_e = jnp.sum(jnp.where(lane16 == e, comb_scr[...], 0.0), axis=-1, keepdims=True)
    acc_scr[...] += c_e * out_e

    @pl.when(e == ne - 1)
    def _():
        h2 = h_ref[...] + acc_scr[...]
        gate = _sigmoid(_dot(_rms(h2, gple_ref[...]).astype(BF16), wpg_ref[...]))
        h3 = h2 + _dot(p_ref[...].astype(BF16), wp_ref[...]) * gate
        y_ref[...] = _rms(h3, gfin_ref[...])


def _ffn(h, p, w, tb):
    nt = h.shape[0]
    return pl.pallas_call(
        _ffn_kernel,
        grid=(nt // tb, N_EXPERTS),
        in_specs=[
            pl.BlockSpec((tb, D_MODEL), lambda i, e: (i, 0)),
            pl.BlockSpec((tb, PLE_DIM), lambda i, e: (i, 0)),
            _const_spec((1, D_MODEL)),
            _const_spec((2, D_MODEL, 2 * N_EXPERTS)),
            _const_spec((1, 2 * N_EXPERTS)),
            pl.BlockSpec((1, D_MODEL, D_FF), lambda i, e: (e, 0, 0)),
            pl.BlockSpec((1, D_MODEL, D_FF), lambda i, e: (e, 0, 0)),
            pl.BlockSpec((1, D_FF, D_MODEL), lambda i, e: (e, 0, 0)),
            _const_spec((1, D_MODEL)),
            _const_spec((D_MODEL, D_MODEL)),
            _const_spec((PLE_DIM, D_MODEL)),
            _const_spec((1, D_MODEL)),
        ],
        out_specs=pl.BlockSpec((tb, D_MODEL), lambda i, e: (i, 0)),
        out_shape=jax.ShapeDtypeStruct((nt, D_MODEL), F32),
        scratch_shapes=[
            pltpu.VMEM((tb, D_MODEL), BF16),
            pltpu.VMEM((tb, N_EXPERTS), F32),
            pltpu.VMEM((tb, D_MODEL), F32),
        ],
        compiler_params=pltpu.CompilerParams(
            dimension_semantics=("arbitrary", "arbitrary"), vmem_limit_bytes=VMEM_LIMIT),
        name="ffn",
    )(h, p, w['g_ffn'], w['w_router'], w['b_router'], w['w_gate_e'], w['w_up_e'], w['w_down_e'],
      w['g_ple'], w['w_ple_gate'], w['w_ple'], w['g_final'])


ROUTE_TB = 1024
EXPERT_TM = 512
SC_WINDOW = 128
SC_CORES = 2
SC_SUBCORES = 16
SC_WORKERS = SC_CORES * SC_SUBCORES
HALF = D_MODEL // 2


def _pack_bf16_pairs(v):
    bits = pltpu.bitcast(v.astype(BF16).astype(F32), jnp.uint32)
    packed = bits[:, HALF:] | (bits[:, :HALF] >> 16)
    return pltpu.bitcast(packed, jnp.int32)


def _unpack_bf16_pairs(p):
    u = pltpu.bitcast(p, jnp.uint32)
    lo = pltpu.bitcast(u << 16, F32)
    hi = pltpu.bitcast(u & jnp.uint32(0xFFFF0000), F32)
    return jnp.concatenate([lo, hi], axis=-1)


def _route_kernel(h_ref, gffn_ref, wr_ref, br_ref, tp_ref, meta_ref, wts_ref, cnt_ref, lower_scr, carry_scr):
    i = pl.program_id(0)
    tb = h_ref.shape[0]

    @pl.when(i == 0)
    def _():
        r = lax.broadcasted_iota(jnp.int32, (tb, tb), 0)
        c = lax.broadcasted_iota(jnp.int32, (tb, tb), 1)
        lower_scr[...] = (r > c).astype(BF16)
        carry_scr[...] = jnp.zeros(carry_scr.shape, F32)

    tf = _rms(h_ref[...], gffn_ref[...])
    tp_ref[...] = _pack_bf16_pairs(tf)
    t_h = tf.astype(BF16)
    t_m = (tf - t_h.astype(F32)).astype(BF16)
    w_h = wr_ref[0]
    w_m = wr_ref[1]
    logits = (_dot(t_h, w_h) + _dot(t_m, w_h) + _dot(t_h, w_m)) + br_ref[...]
    lg = logits[:, 0:N_EXPERT_GROUPS]
    le = logits[:, N_EXPERTS:2 * N_EXPERTS]
    eg = jnp.exp(lg - jnp.max(lg, axis=-1, keepdims=True))
    pg = eg / jnp.sum(eg, axis=-1, keepdims=True)
    g_val, g_idx = _first_argmax(pg, N_EXPERT_GROUPS)
    lane16 = lax.broadcasted_iota(jnp.int32, le.shape, 1)
    in_grp = (lane16 // EXPERTS_PER_GROUP) == g_idx
    le_m = jnp.where(in_grp, le, -jnp.inf)
    ee = jnp.where(in_grp, jnp.exp(le - jnp.max(le_m, axis=-1, keepdims=True)), 0.0)
    pe = ee / jnp.sum(ee, axis=-1, keepdims=True)
    pe_m = jnp.where(in_grp, pe, -1.0)
    v1, i1 = _first_argmax(pe_m, N_EXPERTS)
    v2, i2 = _first_argmax(jnp.where(lane16 == i1, -1.0, pe_m), N_EXPERTS)
    den = v1 + v2
    w0 = g_val * v1 / den
    w1 = g_val * v2 / den

    sel0 = lane16 == i1
    sel1 = lane16 == i2
    hot = jnp.where(sel0 | sel1, 1.0, 0.0)
    rank = _dot(lower_scr[...], hot.astype(BF16)) + carry_scr[...]
    r0 = jnp.sum(jnp.where(sel0, rank, 0.0), axis=-1, keepdims=True)
    r1 = jnp.sum(jnp.where(sel1, rank, 0.0), axis=-1, keepdims=True)
    carry_scr[...] += jnp.sum(hot, axis=0, keepdims=True)
    cnt_ref[...] = carry_scr[...]

    lane = lax.broadcasted_iota(jnp.int32, (tb, LANES), 1)
    metaf = jnp.where(lane == 0, i1.astype(F32), jnp.where(lane == 1, i2.astype(F32),
                      jnp.where(lane == 2, r0, jnp.where(lane == 3, r1, 0.0))))
    meta_ref[...] = metaf.T[0:SUBLANES, :].astype(jnp.int32)
    wts_ref[...] = jnp.where(lane == 0, w0, jnp.where(lane == 1, w1, 0.0))


def _route(h, w, tb=ROUTE_TB):
    nt = h.shape[0]
    return pl.pallas_call(
        _route_kernel,
        grid=(nt // tb,),
        in_specs=[
            pl.BlockSpec((tb, D_MODEL), lambda i: (i, 0)),
            _const_spec((1, D_MODEL)),
            _const_spec((2, D_MODEL, 2 * N_EXPERTS)),
            _const_spec((1, 2 * N_EXPERTS)),
        ],
        out_specs=[
            pl.BlockSpec((tb, HALF), lambda i: (i, 0)),
            pl.BlockSpec((SUBLANES, tb), lambda i: (0, i)),
            pl.BlockSpec((tb, LANES), lambda i: (i, 0)),
            pl.BlockSpec((1, N_EXPERTS), lambda i: (0, 0)),
        ],
        out_shape=[
            jax.ShapeDtypeStruct((nt, HALF), jnp.int32),
            jax.ShapeDtypeStruct((SUBLANES, nt), jnp.int32),
            jax.ShapeDtypeStruct((nt, LANES), F32),
            jax.ShapeDtypeStruct((1, N_EXPERTS), F32),
        ],
        scratch_shapes=[pltpu.VMEM((tb, tb), BF16), pltpu.VMEM((1, N_EXPERTS), F32)],
        compiler_params=pltpu.CompilerParams(dimension_semantics=("arbitrary",), vmem_limit_bytes=VMEM_LIMIT),
        name="moe_route",
    )(h, w['g_ffn'], w['w_router'], w['b_router'])


def _sc_mesh():
    return plsc.VectorSubcoreMesh(core_axis_name="c", subcore_axis_name="s")


def _sc_dispatch(tp, slot0, slot1, n_slots):
    nt = tp.shape[0]

    per_worker = nt // (SC_WINDOW * SC_WORKERS)

    @pl.kernel(out_type=jax.ShapeDtypeStruct((n_slots, HALF), tp.dtype), mesh=_sc_mesh(), name="moe_dispatch",
               scratch_types=[pltpu.VMEM((1, SC_WINDOW), jnp.int32), pltpu.VMEM((1, SC_WINDOW), jnp.int32),
                              pltpu.VMEM((SC_WINDOW, HALF), tp.dtype)])
    def run(x_hbm, i0_hbm, i1_hbm, o_hbm, i0_v, i1_v, buf_v):
        worker = lax.axis_index("c") * SC_SUBCORES + lax.axis_index("s")

        @pl.loop(0, per_worker)
        def _(k):
            base = (worker * per_worker + k) * SC_WINDOW
            pltpu.sync_copy(i0_hbm.at[:, pl.ds(base, SC_WINDOW)], i0_v)
            pltpu.sync_copy(i1_hbm.at[:, pl.ds(base, SC_WINDOW)], i1_v)
            pltpu.sync_copy(x_hbm.at[pl.ds(base, SC_WINDOW)], buf_v)
            pltpu.sync_copy(buf_v, o_hbm.at[i0_v.at[0]])
            pltpu.sync_copy(buf_v, o_hbm.at[i1_v.at[0]])

    return run(tp, slot0, slot1)


def _sc_combine(ys, slot0, slot1):
    nt = slot0.shape[1]
    out = jax.ShapeDtypeStruct((nt, HALF), ys.dtype)

    per_worker = nt // (SC_WINDOW * SC_WORKERS)

    @pl.kernel(out_type=(out, out), mesh=_sc_mesh(), name="moe_combine",
               scratch_types=[pltpu.VMEM((1, SC_WINDOW), jnp.int32), pltpu.VMEM((1, SC_WINDOW), jnp.int32),
                              pltpu.VMEM((SC_WINDOW, HALF), ys.dtype)])
    def run(y_hbm, i0_hbm, i1_hbm, o0_hbm, o1_hbm, i0_v, i1_v, buf_v):
        worker = lax.axis_index("c") * SC_SUBCORES + lax.axis_index("s")

        @pl.loop(0, per_worker)
        def _(k):
            base = (worker * per_worker + k) * SC_WINDOW
            pltpu.sync_copy(i0_hbm.at[:, pl.ds(base, SC_WINDOW)], i0_v)
            pltpu.sync_copy(i1_hbm.at[:, pl.ds(base, SC_WINDOW)], i1_v)
            pltpu.sync_copy(y_hbm.at[i0_v.at[0]], buf_v)
            pltpu.sync_copy(buf_v, o0_hbm.at[pl.ds(base, SC_WINDOW)])
            pltpu.sync_copy(y_hbm.at[i1_v.at[0]], buf_v)
            pltpu.sync_copy(buf_v, o1_hbm.at[pl.ds(base, SC_WINDOW)])

    return run(ys, slot0, slot1)


def _experts_kernel(te_ref, nv_ref, x_ref, wg_ref, wu_ref, wd_ref, y_ref, wg_s, wu_s, wd_s):
    j = pl.program_id(0)

    @pl.when((j == 0) | (te_ref[j] != te_ref[jnp.maximum(j - 1, 0)]))
    def _():
        wg_s[...] = wg_ref[0].astype(BF16)
        wu_s[...] = wu_ref[0].astype(BF16)
        wd_s[...] = wd_ref[0].astype(BF16)

    @pl.when(j < nv_ref[0])
    def _():
        x = _unpack_bf16_pairs(x_ref[...]).astype(BF16)
        hdn = _silu(_dot(x, wg_s[...])) * _dot(x, wu_s[...])
        y_ref[...] = _pack_bf16_pairs(_dot(hdn.astype(BF16), wd_s[...]))


def _experts(xs, tile_expert, n_valid, w, tm=EXPERT_TM):
    n_slots = xs.shape[0]
    grid_spec = pltpu.PrefetchScalarGridSpec(
        num_scalar_prefetch=2,
        grid=(n_slots // tm,),
        in_specs=[
            pl.BlockSpec((tm, HALF), lambda j, te, nv: (j, 0)),
            pl.BlockSpec((1, D_MODEL, D_FF), lambda j, te, nv: (te[j], 0, 0)),
            pl.BlockSpec((1, D_MODEL, D_FF), lambda j, te, nv: (te[j], 0, 0)),
            pl.BlockSpec((1, D_FF, D_MODEL), lambda j, te, nv: (te[j], 0, 0)),
        ],
        out_specs=pl.BlockSpec((tm, HALF), lambda j, te, nv: (j, 0)),
        scratch_shapes=[pltpu.VMEM((D_MODEL, D_FF), BF16), pltpu.VMEM((D_MODEL, D_FF), BF16),
                        pltpu.VMEM((D_FF, D_MODEL), BF16)],
    )
    return pl.pallas_call(
        _experts_kernel,
        grid_spec=grid_spec,
        out_shape=jax.ShapeDtypeStruct((n_slots, HALF), jnp.int32),
        compiler_params=pltpu.CompilerParams(dimension_semantics=("arbitrary",), vmem_limit_bytes=VMEM_LIMIT),
        name="moe_experts",
    )(tile_expert, n_valid, xs, w['w_gate_e'], w['w_up_e'], w['w_down_e'])


def _finish_kernel(h_ref, y0_ref, y1_ref, wts_ref, p_ref, gple_ref, wpg_ref, wp_ref, gfin_ref, o_ref):
    wts = wts_ref[...]
    moe = wts[:, 0:1] * _unpack_bf16_pairs(y0_ref[...]) + wts[:, 1:2] * _unpack_bf16_pairs(y1_ref[...])
    h2 = h_ref[...] + moe
    gate = _sigmoid(_dot(_rms(h2, gple_ref[...]).astype(BF16), wpg_ref[...]))
    h3 = h2 + _dot(p_ref[...].astype(BF16), wp_ref[...]) * gate
    o_ref[...] = _rms(h3, gfin_ref[...])


def _finish(h, y0, y1, wts, p, w, tb=1024):
    nt = h.shape[0]
    return pl.pallas_call(
        _finish_kernel,
        grid=(nt // tb,),
        in_specs=[
            pl.BlockSpec((tb, D_MODEL), lambda i: (i, 0)),
            pl.BlockSpec((tb, HALF), lambda i: (i, 0)),
            pl.BlockSpec((tb, HALF), lambda i: (i, 0)),
            pl.BlockSpec((tb, LANES), lambda i: (i, 0)),
            pl.BlockSpec((tb, PLE_DIM), lambda i: (i, 0)),
            _const_spec((1, D_MODEL)),
            _const_spec((D_MODEL, D_MODEL)),
            _const_spec((PLE_DIM, D_MODEL)),
            _const_spec((1, D_MODEL)),
        ],
        out_specs=pl.BlockSpec((tb, D_MODEL), lambda i: (i, 0)),
        out_shape=jax.ShapeDtypeStruct((nt, D_MODEL), F32),
        compiler_params=pltpu.CompilerParams(dimension_semantics=("arbitrary",), vmem_limit_bytes=VMEM_LIMIT),
        name="moe_finish",
    )(h, y0, y1, wts, p, w['g_ple'], w['w_ple_gate'], w['w_ple'], w['g_final'])


def _ffn_sparse(h, p, w):
    nt = h.shape[0]
    tm = EXPERT_TM
    n_tiles = (nt * 2) // tm + N_EXPERTS
    tp, meta, wts, counts = _route(h, w)
    cnt = counts.reshape(N_EXPERTS).astype(jnp.int32)
    tiles_e = (cnt + tm - 1) // tm
    tile_end = jnp.cumsum(tiles_e)
    off = (tile_end - tiles_e) * tm
    n_valid = tile_end[-1:]
    tile_ids = jnp.arange(n_tiles, dtype=jnp.int32)
    tile_expert = jnp.sum((tile_ids[:, None] >= tile_end[None, :]).astype(jnp.int32), axis=1)
    last_expert = jnp.sum((n_valid - 1 >= tile_end).astype(jnp.int32))
    tile_expert = jnp.minimum(tile_expert, last_expert).astype(jnp.int32)
    eid = meta[0:2]
    slot = meta[2:4] + jnp.sum(jnp.where(eid[..., None] == jnp.arange(N_EXPERTS), off, 0), axis=-1)
    slot0 = slot[0:1]
    slot1 = slot[1:2]
    xs = _sc_dispatch(tp, slot0, slot1, n_tiles * tm)
    ys = _experts(xs, tile_expert, n_valid.astype(jnp.int32), w)
    y0, y1 = _sc_combine(ys, slot0, slot1)
    return _finish(h, y0, y1, wts, p, w)


def _prep_weights(g_mix, w_in, conf_dw_w, conf_dw_b, conf_ln_g, conf_ln_b, ssm_conv_w, ssm_conv_b,
                  dt_bias, a_log, d_skip, ssm_norm_g, w_out, g_ffn, w_rg, b_rg, w_re, b_re,
                  w_gate_e, w_up_e, w_down_e, g_ple, w_ple_gate, w_ple, g_final):
    row = lambda v: v.reshape(1, -1)
    w_in_b = w_in.astype(BF16)
    conf_w_pad = jnp.concatenate([conf_dw_w, jnp.zeros((CONF_PAD - K_CONF, C_CONF), F32)], axis=0)
    w_router = jnp.concatenate(
        [w_rg, jnp.zeros((D_MODEL, N_EXPERTS - N_EXPERT_GROUPS), F32), w_re], axis=1)
    wr_hi = w_router.astype(BF16)
    wr_mid = (w_router - wr_hi.astype(F32)).astype(BF16)
    b_router = jnp.concatenate([b_rg, jnp.zeros((N_EXPERTS - N_EXPERT_GROUPS,), F32), b_re]).reshape(1, -1)
    head_of_lane = jnp.arange(D_INNER) // HEAD_DIM
    head_expand = (head_of_lane[None, :] == jnp.arange(N_HEADS)[:, None]).astype(BF16)
    return dict(
        g_mix=row(g_mix), w_in=w_in_b, w_dtT=w_in_b[:, O_DT:].T,
        conf_w=conf_w_pad.reshape(CONF_PAD, N_LANE_TILES, LANES).transpose(1, 0, 2),
        conf_b=conf_dw_b.reshape(N_LANE_TILES, 1, LANES),
        conf_w2=conf_dw_w, conf_b2=row(conf_dw_b),
        ssm_w3=ssm_conv_w.reshape(K_SSM, N_XBC_TILES, LANES).transpose(1, 0, 2),
        ssm_b3=ssm_conv_b.reshape(N_XBC_TILES, 1, LANES),
        ln_g=row(conf_ln_g), ln_b=row(conf_ln_b), ssm_w=ssm_conv_w, ssm_b=row(ssm_conv_b),
        dt_bias=row(dt_bias), dt_biasT=dt_bias.reshape(-1, 1), a_log=row(a_log), a_logT=a_log.reshape(-1, 1),
        d_skip=row(jnp.repeat(d_skip, HEAD_DIM)), ssm_norm_g=row(ssm_norm_g), w_out=w_out.astype(BF16),
        head_expand=head_expand,
        g_ffn=row(g_ffn), w_router=jnp.stack([wr_hi, wr_mid]), b_router=b_router,
        w_gate_e=w_gate_e, w_up_e=w_up_e, w_down_e=w_down_e,
        g_ple=row(g_ple), w_ple_gate=w_ple_gate.astype(BF16), w_ple=w_ple.astype(BF16), g_final=row(g_final),
    )


def kernel(x_prompt, x_sample, p_prompt, p_sample, state_conf_conv, state_ssm_conv, state_ssm, g_mix, w_in, conf_dw_w, conf_dw_b, conf_ln_g, conf_ln_b, ssm_conv_w, ssm_conv_b, dt_bias, a_log, d_skip, ssm_norm_g, w_out, g_ffn, w_rg, b_rg, w_re, b_re, w_gate_e, w_up_e, w_down_e, g_ple, w_ple_gate, w_ple, g_final):
    depth = g_mix.shape[0]
    bsz, seq, _ = x_prompt.shape
    nb = x_sample.shape[0]
    hp = x_prompt
    hs = x_sample.reshape(nb, D_MODEL)
    cp_l, mp_l, sp_l, cs_l, ms_l, ss_l = [], [], [], [], [], []
    for i in range(depth):
        w = _prep_weights(g_mix[i], w_in[i], conf_dw_w[i], conf_dw_b[i], conf_ln_g[i], conf_ln_b[i],
                          ssm_conv_w[i], ssm_conv_b[i], dt_bias[i], a_log[i], d_skip[i], ssm_norm_g[i], w_out[i],
                          g_ffn[i], w_rg[i], b_rg[i], w_re[i], b_re[i], w_gate_e[i], w_up_e[i], w_down_e[i],
                          g_ple[i], w_ple_gate[i], w_ple[i], g_final)
        h1p, c, m, s = _mixer_prompt(hp, w)
        cp_l.append(c)
        mp_l.append(m)
        sp_l.append(s)
        h1s, c, m, s = _mixer_sample(hs, jnp.transpose(state_conf_conv[i], (1, 0, 2)), state_ssm_conv[i],
                                     state_ssm[i], w)
        cs_l.append(jnp.transpose(c, (1, 0, 2)))
        ms_l.append(m)
        ss_l.append(s)
        assert depth == 1
        hp = _ffn_sparse(h1p.reshape(bsz * seq, D_MODEL), p_prompt[i].reshape(bsz * seq, PLE_DIM), w)
        hp = hp.reshape(bsz, seq, D_MODEL)
        hs = _ffn(h1s, p_sample[i].reshape(nb, PLE_DIM), w, tb=nb)
    return (hp, hs.reshape(nb, 1, D_MODEL), jnp.stack(cp_l), jnp.stack(mp_l), jnp.stack(sp_l),
            jnp.stack(cs_l), jnp.stack(ms_l), jnp.stack(ss_l))
```

```python
import functools

import jax
import jax.numpy as jnp
from jax import lax
from jax.experimental import pallas as pl
from jax.experimental.pallas import tpu as pltpu
from jax.experimental.pallas import tpu_sc as plsc

F32 = jnp.float32
BF16 = jnp.bfloat16

D_MODEL = 1024
C_CONF = 1024
K_CONF = 31
D_INNER = 1024
HEAD_DIM = 64
N_HEADS = 16
N_GROUPS = 4
HEADS_PER_GROUP = N_HEADS // N_GROUPS
GROUP_W = HEADS_PER_GROUP * HEAD_DIM
D_STATE = 128
K_SSM = 4
CHUNK = 128
CONV_DIM = D_INNER + 2 * N_GROUPS * D_STATE
IN_COLS = 2 * C_CONF + D_INNER + CONV_DIM + N_HEADS
O_GATE = C_CONF
O_Z = 2 * C_CONF
O_XBC = O_Z + D_INNER
O_DT = O_XBC + CONV_DIM
N_EXPERT_GROUPS = 4
EXPERTS_PER_GROUP = 4
N_EXPERTS = 16
D_FF = 512
PLE_DIM = 256
EPS = 1e-6

LANES = 128
SUBLANES = 8
N_LANE_TILES = C_CONF // LANES
N_XBC_TILES = CONV_DIM // LANES
CONF_PAD = 32
SSM_PAD = 8
VMEM_LIMIT = 56 * 1024 * 1024


def _dot(a, b):
    return jnp.dot(a, b, preferred_element_type=F32)


def _dot_nt(a, b):
    return lax.dot_general(a, b, (((1,), (1,)), ((), ())), preferred_element_type=F32)


def _dot_tn(a, b):
    return lax.dot_general(a, b, (((0,), (0,)), ((), ())), preferred_element_type=F32)


def _split3(v):
    hi = v.astype(BF16)
    r = v - hi.astype(F32)
    mid = r.astype(BF16)
    lo = (r - mid.astype(F32)).astype(BF16)
    return hi, mid, lo


def _rms(x, g):
    return x * lax.rsqrt(jnp.mean(x * x, axis=-1, keepdims=True) + EPS) * g


def _sigmoid(x):
    return jax.nn.sigmoid(x)


def _silu(x):
    return x * jax.nn.sigmoid(x)


def _softplus(x):
    return jax.nn.softplus(x)


def _mixer_prompt_kernel(x_ref, gmix_ref, win_ref, wdtT_ref, cw_ref, cb_ref, lng_ref, lnb_ref,
                         sw_ref, sb_ref, dtb_ref, dtbT_ref, alog_ref, alogT_ref, dskip_ref, sng_ref,
                         wout_ref, hexp_ref,
                         h1_ref, ncc_ref, nsc_ref, nss_ref,
                         cscr, cout, mscr, xbc_scr, st_scr, *, tl):
    t = pl.program_id(1)
    nt = pl.num_programs(1)

    @pl.when(t == 0)
    def _():
        cscr[:, 0:CONF_PAD, :] = jnp.zeros((N_LANE_TILES, CONF_PAD, LANES), F32)
        mscr[:, 0:SSM_PAD, :] = jnp.zeros((N_XBC_TILES, SSM_PAD, LANES), F32)
        st_scr[...] = jnp.zeros(st_scr.shape, F32)

    x = x_ref[0]
    u = _rms(x, gmix_ref[...]).astype(BF16)

    glu = _dot(u, win_ref[:, 0:O_GATE]) * _sigmoid(_dot(u, win_ref[:, O_GATE:O_Z]))
    for lc in range(N_LANE_TILES):
        cscr[lc, CONF_PAD:CONF_PAD + tl, :] = glu[:, lc * LANES:(lc + 1) * LANES]

    rc = 64

    def conv_lane_tile(lc, carry):
        bias = cb_ref[lc]
        for r0 in range(0, tl, rc):
            acc = jnp.broadcast_to(bias, (rc, LANES))
            for k in range(K_CONF):
                acc = acc + cw_ref[lc, pl.ds(k, 1), :] * cscr[lc, pl.ds(r0 + k + CONF_PAD - (K_CONF - 1), rc), :]
            cout[lc, pl.ds(r0, rc), :] = acc
        return carry

    for lc in range(N_LANE_TILES):
        conv_lane_tile(lc, 0)

    for lc in range(N_LANE_TILES):
        cscr[lc, 0:CONF_PAD, :] = cscr[lc, tl:tl + CONF_PAD, :]

    cc = [cout[lc] for lc in range(N_LANE_TILES)]
    tot = cc[0]
    for lc in range(1, N_LANE_TILES):
        tot = tot + cc[lc]
    mean = jnp.sum(tot, axis=-1, keepdims=True) * (1.0 / C_CONF)
    xc = [c - mean for c in cc]
    sq = xc[0] * xc[0]
    for lc in range(1, N_LANE_TILES):
        sq = sq + xc[lc] * xc[lc]
    rstd = lax.rsqrt(jnp.sum(sq, axis=-1, keepdims=True) * (1.0 / C_CONF) + EPS)
    a_out = jnp.concatenate(
        [_silu(xc[lc] * rstd * lng_ref[:, lc * LANES:(lc + 1) * LANES] + lnb_ref[:, lc * LANES:(lc + 1) * LANES])
         for lc in range(N_LANE_TILES)], axis=-1).astype(BF16)

    z = _dot(u, win_ref[:, O_Z:O_XBC])
    xbc_raw = _dot(u, win_ref[:, O_XBC:O_DT])
    for lt in range(N_XBC_TILES):
        mscr[lt, SSM_PAD:SSM_PAD + tl, :] = xbc_raw[:, lt * LANES:(lt + 1) * LANES]

    def ssm_conv_lane_tile(lt, carry):
        bias = sb_ref[lt]
        for r0 in range(0, tl, rc):
            acc = jnp.broadcast_to(bias, (rc, LANES))
            for k in range(K_SSM):
                acc = acc + sw_ref[lt, pl.ds(k, 1), :] * mscr[lt, pl.ds(r0 + k + SSM_PAD - (K_SSM - 1), rc), :]
            xbc_scr[lt, pl.ds(r0, rc), :] = _silu(acc)
        mscr[lt, 0:SSM_PAD, :] = mscr[lt, tl:tl + SSM_PAD, :]
        return carry

    for lt in range(N_XBC_TILES):
        ssm_conv_lane_tile(lt, 0)

    n_x = D_INNER // LANES
    n_b = N_GROUPS * D_STATE // LANES
    xs = jnp.concatenate([xbc_scr[lt] for lt in range(n_x)], axis=-1)
    bm = jnp.concatenate([xbc_scr[lt] for lt in range(n_x, n_x + n_b)], axis=-1)
    cm = jnp.concatenate([xbc_scr[lt] for lt in range(n_x + n_b, N_XBC_TILES)], axis=-1)

    dt = _softplus(_dot(u, win_ref[:, O_DT:IN_COLS]) + dtb_ref[...])
    dtT = _softplus(_dot_nt(wdtT_ref[...], u) + dtbT_ref[...])
    a = dt * (-jnp.exp(alog_ref[...]))
    aT = dtT * (-jnp.exp(alogT_ref[...]))
    hexp = hexp_ref[...]
    d_h, d_m, d_l = _split3(dt)
    xdt_all = xs * (_dot(d_h, hexp) + _dot(d_m, hexp) + _dot(d_l, hexp))

    row = lax.broadcasted_iota(jnp.int32, (CHUNK, CHUNK), 0)
    col = lax.broadcasted_iota(jnp.int32, (CHUNK, CHUNK), 1)
    lower = row >= col
    tri = lower.astype(BF16)
    triT = (row <= col).astype(BF16)

    y_chunks = []
    for c in range(tl // CHUNK):
        r0 = c * CHUNK
        a_c = a[r0:r0 + CHUNK]
        aT_c = aT[:, r0:r0 + CHUNK]
        ah, am, al = _split3(a_c)
        cs = _dot(tri, ah) + _dot(tri, am) + _dot(tri, al)
        th, tm, tlo = _split3(aT_c)
        csT = _dot(th, triT) + _dot(tm, triT) + _dot(tlo, triT)
        cs_last = cs[CHUNK - 1:CHUNK, :]
        cdec = jnp.exp(cs_last)
        xdt_c = xdt_all[r0:r0 + CHUNK]
        y_heads = []
        for g in range(N_GROUPS):
            cg = cm[r0:r0 + CHUNK, g * D_STATE:(g + 1) * D_STATE].astype(BF16)
            bg = bm[r0:r0 + CHUNK, g * D_STATE:(g + 1) * D_STATE].astype(BF16)
            cb = _dot_nt(cg, bg)
            y_off = _dot(cg, st_scr[g].astype(BF16))
            xdd = []
            dec_row = []
            m_parts_g = []
            x_bd = []
            e_parts = []
            xdt_g = xdt_c[:, g * GROUP_W:(g + 1) * GROUP_W]
            lane_head = lax.broadcasted_iota(jnp.int32, (CHUNK, GROUP_W), 1) // HEAD_DIM
            for hh in range(HEADS_PER_GROUP):
                h = g * HEADS_PER_GROUP + hh
                xdt = xdt_g[:, hh * HEAD_DIM:(hh + 1) * HEAD_DIM]
                cs_b = jnp.broadcast_to(cs[:, h:h + 1], (CHUNK, CHUNK))
                lmat = jnp.where(lower, jnp.exp(cs_b - csT[h:h + 1, :]), 0.0)
                m_parts_g.append((cb * lmat).astype(BF16))
                x_bd.append(jnp.where(lane_head == hh, xdt_g, 0.0).astype(BF16))
                cs_bh = cs_b[:, 0:HEAD_DIM]
                e_parts.append(jnp.exp(cs_bh))
                xdd.append((xdt * jnp.exp(csT[h:h + 1, CHUNK - 1:CHUNK] - cs_bh)).astype(BF16))
                dec_row.append(jnp.broadcast_to(cdec[:, h:h + 1], (1, HEAD_DIM)))
            y_diag = _dot(jnp.concatenate(m_parts_g, axis=1), jnp.concatenate(x_bd, axis=0))
            y_heads.append(y_diag + y_off * jnp.concatenate(e_parts, axis=-1))
            contrib = _dot_tn(bg, jnp.concatenate(xdd, axis=-1))
            st_scr[g] = st_scr[g] * jnp.concatenate(dec_row, axis=-1) + contrib
        y_chunks.append(jnp.concatenate(y_heads, axis=-1))
    y = y_chunks[0] if len(y_chunks) == 1 else jnp.concatenate(y_chunks, axis=0)
    y = (y + dskip_ref[...] * xs) * _silu(z)
    m_parts = []
    for g in range(N_GROUPS):
        yg = y[:, g * GROUP_W:(g + 1) * GROUP_W]
        m_parts.append(_rms(yg, sng_ref[:, g * GROUP_W:(g + 1) * GROUP_W]))
    m_out = jnp.concatenate(m_parts, axis=-1).astype(BF16)

    h1_ref[0] = x + _dot(a_out, wout_ref[0:C_CONF, :]) + _dot(m_out, wout_ref[C_CONF:, :])

    @pl.when(t == nt - 1)
    def _():
        for lc in range(N_LANE_TILES):
            ncc_ref[0, :, lc * LANES:(lc + 1) * LANES] = cscr[lc, pl.ds(CONF_PAD - (K_CONF - 1), K_CONF - 1), :]
        for lt in range(N_XBC_TILES):
            nsc_ref[0, :, lt * LANES:(lt + 1) * LANES] = mscr[lt, pl.ds(SSM_PAD - (K_SSM - 1), K_SSM - 1), :]
        for g in range(N_GROUPS):
            nss_ref[0, g * HEADS_PER_GROUP:(g + 1) * HEADS_PER_GROUP] = (
                st_scr[g].T.reshape(HEADS_PER_GROUP, HEAD_DIM, D_STATE))


def _const_spec(shape):
    nd = len(shape)
    return pl.BlockSpec(shape, lambda *_: (0,) * nd, pipeline_mode=pl.Buffered(1))


def _mixer_prompt(x, w, tl=256):
    bsz, seq, _ = x.shape
    kern = functools.partial(_mixer_prompt_kernel, tl=tl)
    consts = [w['g_mix'], w['w_in'], w['w_dtT'], w['conf_w'], w['conf_b'], w['ln_g'], w['ln_b'],
              w['ssm_w3'], w['ssm_b3'], w['dt_bias'], w['dt_biasT'], w['a_log'], w['a_logT'], w['d_skip'],
              w['ssm_norm_g'], w['w_out'], w['head_expand']]
    return pl.pallas_call(
        kern,
        grid=(bsz, seq // tl),
        in_specs=[pl.BlockSpec((1, tl, D_MODEL), lambda b, t: (b, t, 0))] + [_const_spec(c.shape) for c in consts],
        out_specs=[
            pl.BlockSpec((1, tl, D_MODEL), lambda b, t: (b, t, 0)),
            pl.BlockSpec((1, K_CONF - 1, C_CONF), lambda b, t: (b, 0, 0)),
            pl.BlockSpec((1, K_SSM - 1, CONV_DIM), lambda b, t: (b, 0, 0)),
            pl.BlockSpec((1, N_HEADS, HEAD_DIM, D_STATE), lambda b, t: (b, 0, 0, 0)),
        ],
        out_shape=[
            jax.ShapeDtypeStruct((bsz, seq, D_MODEL), F32),
            jax.ShapeDtypeStruct((bsz, K_CONF - 1, C_CONF), F32),
            jax.ShapeDtypeStruct((bsz, K_SSM - 1, CONV_DIM), F32),
            jax.ShapeDtypeStruct((bsz, N_HEADS, HEAD_DIM, D_STATE), F32),
        ],
        scratch_shapes=[
            pltpu.VMEM((N_LANE_TILES, CONF_PAD + tl, LANES), F32),
            pltpu.VMEM((N_LANE_TILES, tl, LANES), F32),
            pltpu.VMEM((N_XBC_TILES, SSM_PAD + tl, LANES), F32),
            pltpu.VMEM((N_XBC_TILES, tl, LANES), F32),
            pltpu.VMEM((N_GROUPS, D_STATE, GROUP_W), F32),
        ],
        compiler_params=pltpu.CompilerParams(
            dimension_semantics=("arbitrary", "arbitrary"), vmem_limit_bytes=VMEM_LIMIT),
        name="mixer_prompt",
    )(x, *consts)


def _mixer_sample_kernel(x_ref, cst_ref, mst_ref, sst_ref, gmix_ref, win_ref, cw_ref, cb_ref, lng_ref, lnb_ref,
                         sw_ref, sb_ref, dtb_ref, alog_ref, dskip_ref, sng_ref, wout_ref, hexp_ref,
                         h1_ref, ncc_ref, nsc_ref, nss_ref,
                         proj_scr, mix_scr, *, bb):
    i = pl.program_id(0)
    n = pl.num_programs(0)

    @pl.when(i == 0)
    def _():
        u = _rms(x_ref[...], gmix_ref[...]).astype(BF16)
        proj_scr[...] = _dot(u, win_ref[...])

    r0 = pl.multiple_of(i * bb, bb)
    proj = proj_scr[pl.ds(r0, bb), :]

    glu = proj[:, 0:O_GATE] * _sigmoid(proj[:, O_GATE:O_Z])
    acc = cb_ref[...] + cw_ref[pl.ds(K_CONF - 1, 1), :] * glu
    for k in range(K_CONF - 1):
        row_k = cst_ref[k]
        acc = acc + cw_ref[pl.ds(k, 1), :] * row_k
        if k >= 1:
            ncc_ref[k - 1] = row_k
    ncc_ref[K_CONF - 2] = glu
    mean = jnp.mean(acc, axis=-1, keepdims=True)
    xc = acc - mean
    rstd = lax.rsqrt(jnp.mean(xc * xc, axis=-1, keepdims=True) + EPS)
    a_out = _silu(xc * rstd * lng_ref[...] + lnb_ref[...])

    z = proj[:, O_Z:O_XBC]
    xbc_raw = proj[:, O_XBC:O_DT]
    acc = sb_ref[...] + sw_ref[pl.ds(K_SSM - 1, 1), :] * xbc_raw
    for k in range(K_SSM - 1):
        row_k = mst_ref[:, k, :]
        acc = acc + sw_ref[pl.ds(k, 1), :] * row_k
        if k >= 1:
            nsc_ref[:, k - 1, :] = row_k
    nsc_ref[:, K_SSM - 2, :] = xbc_raw
    xbc = _silu(acc)
    xs = xbc[:, 0:D_INNER]
    bm = xbc[:, D_INNER:D_INNER + N_GROUPS * D_STATE]
    cm = xbc[:, D_INNER + N_GROUPS * D_STATE:]
    dt = _softplus(proj[:, O_DT:IN_COLS] + dtb_ref[...])
    dec = jnp.exp(dt * (-jnp.exp(alog_ref[...])))
    hexp = hexp_ref[...]
    d_h, d_m, d_l = _split3(dec)
    dec_e = _dot(d_h, hexp) + _dot(d_m, hexp) + _dot(d_l, hexp)
    t_h, t_m, t_l = _split3(dt)
    dt_e = _dot(t_h, hexp) + _dot(t_m, hexp) + _dot(t_l, hexp)
    xdt = xs * dt_e

    lane = lax.broadcasted_iota(jnp.int32, (SUBLANES, D_INNER), 1)
    sub = lax.broadcasted_iota(jnp.int32, (SUBLANES, D_INNER), 0)
    gmask = (lane // GROUP_W) == sub
    ones_row = (lax.broadcasted_iota(jnp.int32, (SUBLANES, D_STATE), 0) < 3).astype(BF16)

    y_rows = []
    for b in range(bb):
        xrow = jnp.where(gmask, jnp.broadcast_to(xdt[b:b + 1, :], (SUBLANES, D_INNER)), 0.0)
        x_h, x_m, x_l = [v.astype(F32) for v in _split3(xrow)]
        bmat = jnp.concatenate([bm[b:b + 1, g * D_STATE:(g + 1) * D_STATE] for g in range(N_GROUPS)]
                               + [jnp.zeros((SUBLANES - N_GROUPS, D_STATE), F32)], axis=0)
        b_h, b_m, b_l = [v.astype(F32) for v in _split3(bmat)]
        lhs = jnp.concatenate([x_h, x_h, x_m, x_h, x_m, x_l], axis=0).astype(BF16)
        rhs = jnp.concatenate([b_h, b_m, b_h, b_l, b_m, b_h], axis=0).astype(BF16)
        upd = _dot_tn(lhs, rhs)
        dbc = jnp.broadcast_to(dec_e[b:b + 1, :], (SUBLANES, D_INNER))
        q_h = dbc.astype(BF16).astype(F32)
        q_m = (dbc - q_h).astype(BF16).astype(F32)
        q_l = dbc - q_h - q_m
        dlhs = jnp.where(sub == 0, q_h, jnp.where(sub == 1, q_m, jnp.where(sub == 2, q_l, 0.0))).astype(BF16)
        dfull = _dot_tn(dlhs, ones_row)
        hnew = dfull * sst_ref[b].reshape(D_INNER, D_STATE) + upd
        nss_ref[b] = hnew.reshape(N_HEADS, HEAD_DIM, D_STATE)
        cmat = jnp.concatenate([cm[b:b + 1, g * D_STATE:(g + 1) * D_STATE] for g in range(N_GROUPS)]
                               + [jnp.zeros((SUBLANES - N_GROUPS, D_STATE), F32)], axis=0)
        yg = _dot_nt(cmat.astype(BF16), hnew.astype(BF16))
        y_rows.append(jnp.sum(jnp.where(gmask, yg, 0.0), axis=0, keepdims=True))
    y = jnp.concatenate(y_rows, axis=0)
    y = (y + dskip_ref[...] * xs) * _silu(z)
    m_parts = []
    for g in range(N_GROUPS):
        m_parts.append(_rms(y[:, g * GROUP_W:(g + 1) * GROUP_W], sng_ref[:, g * GROUP_W:(g + 1) * GROUP_W]))
    mix_scr[pl.ds(r0, bb), :] = jnp.concatenate([a_out] + m_parts, axis=-1)

    @pl.when(i == n - 1)
    def _():
        h1_ref[...] = x_ref[...] + _dot(mix_scr[...].astype(BF16), wout_ref[...])


def _mixer_sample(x, cst, mst, sst, w, bb=8):
    nb = x.shape[0]
    kern = functools.partial(_mixer_sample_kernel, bb=bb)
    consts = [w['g_mix'], w['w_in'], w['conf_w2'], w['conf_b2'], w['ln_g'], w['ln_b'],
              w['ssm_w'], w['ssm_b'], w['dt_bias'], w['a_log'], w['d_skip'], w['ssm_norm_g'], w['w_out'],
              w['head_expand']]
    return pl.pallas_call(
        kern,
        grid=(nb // bb,),
        in_specs=[
            _const_spec((nb, D_MODEL)),
            pl.BlockSpec((K_CONF - 1, bb, C_CONF), lambda i: (0, i, 0)),
            pl.BlockSpec((bb, K_SSM - 1, CONV_DIM), lambda i: (i, 0, 0)),
            pl.BlockSpec((bb, N_HEADS, HEAD_DIM, D_STATE), lambda i: (i, 0, 0, 0)),
        ] + [_const_spec(c.shape) for c in consts],
        out_specs=[
            pl.BlockSpec((nb, D_MODEL), lambda i: (0, 0)),
            pl.BlockSpec((K_CONF - 1, bb, C_CONF), lambda i: (0, i, 0)),
            pl.BlockSpec((bb, K_SSM - 1, CONV_DIM), lambda i: (i, 0, 0)),
            pl.BlockSpec((bb, N_HEADS, HEAD_DIM, D_STATE), lambda i: (i, 0, 0, 0)),
        ],
        out_shape=[
            jax.ShapeDtypeStruct((nb, D_MODEL), F32),
            jax.ShapeDtypeStruct((K_CONF - 1, nb, C_CONF), F32),
            jax.ShapeDtypeStruct((nb, K_SSM - 1, CONV_DIM), F32),
            jax.ShapeDtypeStruct((nb, N_HEADS, HEAD_DIM, D_STATE), F32),
        ],
        scratch_shapes=[
            pltpu.VMEM((nb, IN_COLS), F32),
            pltpu.VMEM((nb, D_MODEL + D_INNER), F32),
        ],
        compiler_params=pltpu.CompilerParams(dimension_semantics=("arbitrary",), vmem_limit_bytes=VMEM_LIMIT),
        name="mixer_sample",
    )(x, cst, mst, sst, *consts)


def _first_argmax(v, width):
    lane = lax.broadcasted_iota(jnp.int32, v.shape, 1)
    m = jnp.max(v, axis=-1, keepdims=True)
    idx = jnp.min(jnp.where(v == m, lane, width), axis=-1, keepdims=True)
    return m, idx


def _ffn_kernel(h_ref, p_ref, gffn_ref, wr_ref, br_ref, wg_ref, wu_ref, wd_ref, gple_ref, wpg_ref, wp_ref,
                gfin_ref, y_ref, t_scr, comb_scr, acc_scr):
    e = pl.program_id(1)
    ne = pl.num_programs(1)

    @pl.when(e == 0)
    def _():
        tf = _rms(h_ref[...], gffn_ref[...])
        t_scr[...] = tf.astype(BF16)
        t_h, t_m, t_l = _split3(tf)
        w_h = wr_ref[0]
        w_m = wr_ref[1]
        logits = (_dot(t_h, w_h) + _dot(t_m, w_h) + _dot(t_h, w_m) + _dot(t_l, w_h) + _dot(t_m, w_m)) + br_ref[...]
        lg = logits[:, 0:N_EXPERT_GROUPS]
        le = logits[:, N_EXPERTS:2 * N_EXPERTS]
        eg = jnp.exp(lg - jnp.max(lg, axis=-1, keepdims=True))
        pg = eg / jnp.sum(eg, axis=-1, keepdims=True)
        g_val, g_idx = _first_argmax(pg, N_EXPERT_GROUPS)
        lane16 = lax.broadcasted_iota(jnp.int32, le.shape, 1)
        in_grp = (lane16 // EXPERTS_PER_GROUP) == g_idx
        neg = jnp.float32(-jnp.inf)
        le_m = jnp.where(in_grp, le, neg)
        ee = jnp.where(in_grp, jnp.exp(le - jnp.max(le_m, axis=-1, keepdims=True)), 0.0)
        pe = ee / jnp.sum(ee, axis=-1, keepdims=True)
        pe_m = jnp.where(in_grp, pe, -1.0)
        v1, i1 = _first_argmax(pe_m, N_EXPERTS)
        pe_m2 = jnp.where(lane16 == i1, -1.0, pe_m)
        v2, i2 = _first_argmax(pe_m2, N_EXPERTS)
        den = v1 + v2
        comb = jnp.where(lane16 == i1, g_val * v1 / den, 0.0) + jnp.where(lane16 == i2, g_val * v2 / den, 0.0)
        comb_scr[...] = comb
        acc_scr[...] = jnp.zeros(acc_scr.shape, F32)

    t = t_scr[...]
    hdn = _silu(_dot(t, wg_ref[0].astype(BF16))) * _dot(t, wu_ref[0].astype(BF16))
    out_e = _dot(hdn.astype(BF16), wd_ref[0].astype(BF16))
    lane16 = lax.broadcasted_iota(jnp.int32, comb_scr.shape, 1)
    c_e = jnp.sum(jnp.where(lane16 == e, comb_scr[...], 0.0), axis=-1, keepdims=True)
    acc_scr[...] += c_e * out_e

    @pl.when(e == ne - 1)
    def _():
        h2 = h_ref[...] + acc_scr[...]
        gate = _sigmoid(_dot(_rms(h2, gple_ref[...]).astype(BF16), wpg_ref[...]))
        h3 = h2 + _dot(p_ref[...].astype(BF16), wp_ref[...]) * gate
        y_ref[...] = _rms(h3, gfin_ref[...])


def _ffn(h, p, w, tb):
    nt = h.shape[0]
    return pl.pallas_call(
        _ffn_kernel,
        grid=(nt // tb, N_EXPERTS),
        in_specs=[
            pl.BlockSpec((tb, D_MODEL), lambda i, e: (i, 0)),
            pl.BlockSpec((tb, PLE_DIM), lambda i, e: (i, 0)),
            _const_spec((1, D_MODEL)),
            _const_spec((2, D_MODEL, 2 * N_EXPERTS)),
            _const_spec((1, 2 * N_EXPERTS)),
            pl.BlockSpec((1, D_MODEL, D_FF), lambda i, e: (e, 0, 0)),
            pl.BlockSpec((1, D_MODEL, D_FF), lambda i, e: (e, 0, 0)),
            pl.BlockSpec((1, D_FF, D_MODEL), lambda i, e: (e, 0, 0)),
            _const_spec((1, D_MODEL)),
            _const_spec((D_MODEL, D_MODEL)),
            _const_spec((PLE_DIM, D_MODEL)),
            _const_spec((1, D_MODEL)),
        ],
        out_specs=pl.BlockSpec((tb, D_MODEL), lambda i, e: (i, 0)),
        out_shape=jax.ShapeDtypeStruct((nt, D_MODEL), F32),
        scratch_shapes=[
            pltpu.VMEM((tb, D_MODEL), BF16),
            pltpu.VMEM((tb, N_EXPERTS), F32),
            pltpu.VMEM((tb, D_MODEL), F32),
        ],
        compiler_params=pltpu.CompilerParams(
            dimension_semantics=("arbitrary", "arbitrary"), vmem_limit_bytes=VMEM_LIMIT),
        name="ffn",
    )(h, p, w['g_ffn'], w['w_router'], w['b_router'], w['w_gate_e'], w['w_up_e'], w['w_down_e'],
      w['g_ple'], w['w_ple_gate'], w['w_ple'], w['g_final'])


ROUTE_TB = 1024
EXPERT_TM = 512
SC_WINDOW = 128
SC_CORES = 2
SC_SUBCORES = 16
SC_WORKERS = SC_CORES * SC_SUBCORES
HALF = D_MODEL // 2


def _pack_bf16_pairs(v):
    bits = pltpu.bitcast(v.astype(BF16).astype(F32), jnp.uint32)
    packed = bits[:, HALF:] | (bits[:, :HALF] >> 16)
    return pltpu.bitcast(packed, jnp.int32)


def _unpack_bf16_pairs(p):
    u = pltpu.bitcast(p, jnp.uint32)
    lo = pltpu.bitcast(u << 16, F32)
    hi = pltpu.bitcast(u & jnp.uint32(0xFFFF0000), F32)
    return jnp.concatenate([lo, hi], axis=-1)


def _route_kernel(h_ref, gffn_ref, wrT_ref, brT_ref, tp_ref, meta_ref, wts_ref, cnt_ref, upper_scr, carry_scr):
    i = pl.program_id(0)
    tb = h_ref.shape[0]

    @pl.when(i == 0)
    def _():
        r = lax.broadcasted_iota(jnp.int32, (tb, tb), 0)
        c = lax.broadcasted_iota(jnp.int32, (tb, tb), 1)
        upper_scr[...] = (r < c).astype(BF16)
        carry_scr[...] = jnp.zeros(carry_scr.shape, F32)

    tf = _rms(h_ref[...], gffn_ref[...])
    tp_ref[...] = _pack_bf16_pairs(tf)
    t_h = tf.astype(BF16)
    t_m = (tf - t_h.astype(F32)).astype(BF16)
    w_h = wrT_ref[0]
    w_m = wrT_ref[1]
    logits = (_dot_nt(w_h, t_h) + _dot_nt(w_h, t_m) + _dot_nt(w_m, t_h)) + brT_ref[...]
    neg = -jnp.inf
    row8 = lax.broadcasted_iota(jnp.int32, (SUBLANES, tb), 0)
    row16 = lax.broadcasted_iota(jnp.int32, (N_EXPERTS, tb), 0)
    lg = jnp.where(row8 < N_EXPERT_GROUPS, logits[0:SUBLANES], neg)
    le = logits[N_EXPERTS:2 * N_EXPERTS]
    eg = jnp.exp(lg - jnp.max(lg, axis=0, keepdims=True))
    pg = eg / jnp.sum(eg, axis=0, keepdims=True)
    g_val = jnp.max(pg, axis=0, keepdims=True)
    g_idx = jnp.min(jnp.where(pg == g_val, row8, SUBLANES), axis=0, keepdims=True)
    in_grp = (row16 // EXPERTS_PER_GROUP) == g_idx
    le_m = jnp.where(in_grp, le, neg)
    ee = jnp.where(in_grp, jnp.exp(le - jnp.max(le_m, axis=0, keepdims=True)), 0.0)
    pe = ee / jnp.sum(ee, axis=0, keepdims=True)
    pe_m = jnp.where(in_grp, pe, -1.0)
    v1 = jnp.max(pe_m, axis=0, keepdims=True)
    i1 = jnp.min(jnp.where(pe_m == v1, row16, N_EXPERTS), axis=0, keepdims=True)
    pe_m2 = jnp.where(row16 == i1, -1.0, pe_m)
    v2 = jnp.max(pe_m2, axis=0, keepdims=True)
    i2 = jnp.min(jnp.where(pe_m2 == v2, row16, N_EXPERTS), axis=0, keepdims=True)
    den = v1 + v2
    w0 = g_val * v1 / den
    w1 = g_val * v2 / den

    sel0 = row16 == i1
    sel1 = row16 == i2
    hot = jnp.where(sel0 | sel1, 1.0, 0.0)
    rank = _dot(hot.astype(BF16), upper_scr[...]) + carry_scr[...]
    r0 = jnp.sum(jnp.where(sel0, rank, 0.0), axis=0, keepdims=True)
    r1 = jnp.sum(jnp.where(sel1, rank, 0.0), axis=0, keepdims=True)
    carry_scr[...] += jnp.sum(hot, axis=1, keepdims=True)
    cnt_ref[...] = carry_scr[...]

    meta = jnp.where(row8 == 0, i1, jnp.where(row8 == 1, i2, jnp.where(
        row8 == 2, r0.astype(jnp.int32), jnp.where(row8 == 3, r1.astype(jnp.int32), 0))))
    meta_ref[...] = meta
    row128 = lax.broadcasted_iota(jnp.int32, (LANES, tb), 0)
    wts_ref[...] = jnp.where(row128 == 0, w0, jnp.where(row128 == 1, w1, 0.0)).T


def _route(h, w, tb=ROUTE_TB):
    nt = h.shape[0]
    return pl.pallas_call(
        _route_kernel,
        grid=(nt // tb,),
        in_specs=[
            pl.BlockSpec((tb, D_MODEL), lambda i: (i, 0)),
            _const_spec((1, D_MODEL)),
            _const_spec((2, 2 * N_EXPERTS, D_MODEL)),
            _const_spec((2 * N_EXPERTS, 1)),
        ],
        out_specs=[
            pl.BlockSpec((tb, HALF), lambda i: (i, 0)),
            pl.BlockSpec((SUBLANES, tb), lambda i: (0, i)),
            pl.BlockSpec((tb, LANES), lambda i: (i, 0)),
            pl.BlockSpec((N_EXPERTS, 1), lambda i: (0, 0)),
        ],
        out_shape=[
            jax.ShapeDtypeStruct((nt, HALF), jnp.int32),
            jax.ShapeDtypeStruct((SUBLANES, nt), jnp.int32),
            jax.ShapeDtypeStruct((nt, LANES), F32),
            jax.ShapeDtypeStruct((N_EXPERTS, 1), F32),
        ],
        scratch_shapes=[pltpu.VMEM((tb, tb), BF16), pltpu.VMEM((N_EXPERTS, 1), F32)],
        compiler_params=pltpu.CompilerParams(dimension_semantics=("arbitrary",), vmem_limit_bytes=VMEM_LIMIT),
        name="moe_route",
    )(h, w['g_ffn'], w['w_routerT'], w['b_routerT'])


def _sc_mesh():
    return plsc.VectorSubcoreMesh(core_axis_name="c", subcore_axis_name="s")


def _sc_dispatch(tp, slot0, slot1, n_slots):
    nt = tp.shape[0]

    per_worker = nt // (SC_WINDOW * SC_WORKERS)

    @pl.kernel(out_type=jax.ShapeDtypeStruct((n_slots, HALF), tp.dtype), mesh=_sc_mesh(), name="moe_dispatch",
               scratch_types=[pltpu.VMEM((1, SC_WINDOW), jnp.int32), pltpu.VMEM((1, SC_WINDOW), jnp.int32),
                              pltpu.VMEM((SC_WINDOW, HALF), tp.dtype)])
    def run(x_hbm, i0_hbm, i1_hbm, o_hbm, i0_v, i1_v, buf_v):
        worker = lax.axis_index("c") * SC_SUBCORES + lax.axis_index("s")

        @pl.loop(0, per_worker)
        def _(k):
            base = (worker * per_worker + k) * SC_WINDOW
            pltpu.sync_copy(i0_hbm.at[:, pl.ds(base, SC_WINDOW)], i0_v)
            pltpu.sync_copy(i1_hbm.at[:, pl.ds(base, SC_WINDOW)], i1_v)
            pltpu.sync_copy(x_hbm.at[pl.ds(base, SC_WINDOW)], buf_v)
            pltpu.sync_copy(buf_v, o_hbm.at[i0_v.at[0]])
            pltpu.sync_copy(buf_v, o_hbm.at[i1_v.at[0]])

    return run(tp, slot0, slot1)


def _sc_combine(ys, slot0, slot1):
    nt = slot0.shape[1]
    out = jax.ShapeDtypeStruct((nt, HALF), ys.dtype)

    per_worker = nt // (SC_WINDOW * SC_WORKERS)

    @pl.kernel(out_type=(out, out), mesh=_sc_mesh(), name="moe_combine",
               scratch_types=[pltpu.VMEM((1, SC_WINDOW), jnp.int32), pltpu.VMEM((1, SC_WINDOW), jnp.int32),
                              pltpu.VMEM((SC_WINDOW, HALF), ys.dtype)])
    def run(y_hbm, i0_hbm, i1_hbm, o0_hbm, o1_hbm, i0_v, i1_v, buf_v):
        worker = lax.axis_index("c") * SC_SUBCORES + lax.axis_index("s")

        @pl.loop(0, per_worker)
        def _(k):
            base = (worker * per_worker + k) * SC_WINDOW
            pltpu.sync_copy(i0_hbm.at[:, pl.ds(base, SC_WINDOW)], i0_v)
            pltpu.sync_copy(i1_hbm.at[:, pl.ds(base, SC_WINDOW)], i1_v)
            pltpu.sync_copy(y_hbm.at[i0_v.at[0]], buf_v)
            pltpu.sync_copy(buf_v, o0_hbm.at[pl.ds(base, SC_WINDOW)])
            pltpu.sync_copy(y_hbm.at[i1_v.at[0]], buf_v)
            pltpu.sync_copy(buf_v, o1_hbm.at[pl.ds(base, SC_WINDOW)])

    return run(ys, slot0, slot1)


def _experts_kernel(te_ref, nv_ref, x_ref, wg_ref, wu_ref, wd_ref, y_ref, wg_s, wu_s, wd_s):
    j = pl.program_id(0)

    @pl.when((j == 0) | (te_ref[j] != te_ref[jnp.maximum(j - 1, 0)]))
    def _():
        wg_s[...] = wg_ref[0].astype(BF16)
        wu_s[...] = wu_ref[0].astype(BF16)
        wd_s[...] = wd_ref[0].astype(BF16)

    @pl.when(j < nv_ref[0])
    def _():
        x = _unpack_bf16_pairs(x_ref[...]).astype(BF16)
        hdn = _silu(_dot(x, wg_s[...])) * _dot(x, wu_s[...])
        y_ref[...] = _pack_bf16_pairs(_dot(hdn.astype(BF16), wd_s[...]))


def _experts(xs, tile_expert, n_valid, w, tm=EXPERT_TM):
    n_slots = xs.shape[0]
    grid_spec = pltpu.PrefetchScalarGridSpec(
        num_scalar_prefetch=2,
        grid=(n_slots // tm,),
        in_specs=[
            pl.BlockSpec((tm, HALF), lambda j, te, nv: (j, 0)),
            pl.BlockSpec((1, D_MODEL, D_FF), lambda j, te, nv: (te[j], 0, 0)),
            pl.BlockSpec((1, D_MODEL, D_FF), lambda j, te, nv: (te[j], 0, 0)),
            pl.BlockSpec((1, D_FF, D_MODEL), lambda j, te, nv: (te[j], 0, 0)),
        ],
        out_specs=pl.BlockSpec((tm, HALF), lambda j, te, nv: (j, 0)),
        scratch_shapes=[pltpu.VMEM((D_MODEL, D_FF), BF16), pltpu.VMEM((D_MODEL, D_FF), BF16),
                        pltpu.VMEM((D_FF, D_MODEL), BF16)],
    )
    return pl.pallas_call(
        _experts_kernel,
        grid_spec=grid_spec,
        out_shape=jax.ShapeDtypeStruct((n_slots, HALF), jnp.int32),
        compiler_params=pltpu.CompilerParams(dimension_semantics=("arbitrary",), vmem_limit_bytes=VMEM_LIMIT),
        name="moe_experts",
    )(tile_expert, n_valid, xs, w['w_gate_e'], w['w_up_e'], w['w_down_e'])


def _finish_kernel(h_ref, y0_ref, y1_ref, wts_ref, p_ref, gple_ref, wpg_ref, wp_ref, gfin_ref, o_ref):
    wts = wts_ref[...]
    moe = wts[:, 0:1] * _unpack_bf16_pairs(y0_ref[...]) + wts[:, 1:2] * _unpack_bf16_pairs(y1_ref[...])
    h2 = h_ref[...] + moe
    gate = _sigmoid(_dot(_rms(h2, gple_ref[...]).astype(BF16), wpg_ref[...]))
    h3 = h2 + _dot(p_ref[...].astype(BF16), wp_ref[...]) * gate
    o_ref[...] = _rms(h3, gfin_ref[...])


def _finish(h, y0, y1, wts, p, w, tb=1024):
    nt = h.shape[0]
    return pl.pallas_call(
        _finish_kernel,
        grid=(nt // tb,),
        in_specs=[
            pl.BlockSpec((tb, D_MODEL), lambda i: (i, 0)),
            pl.BlockSpec((tb, HALF), lambda i: (i, 0)),
            pl.BlockSpec((tb, HALF), lambda i: (i, 0)),
            pl.BlockSpec((tb, LANES), lambda i: (i, 0)),
            pl.BlockSpec((tb, PLE_DIM), lambda i: (i, 0)),
            _const_spec((1, D_MODEL)),
            _const_spec((D_MODEL, D_MODEL)),
            _const_spec((PLE_DIM, D_MODEL)),
            _const_spec((1, D_MODEL)),
        ],
        out_specs=pl.BlockSpec((tb, D_MODEL), lambda i: (i, 0)),
        out_shape=jax.ShapeDtypeStruct((nt, D_MODEL), F32),
        compiler_params=pltpu.CompilerParams(dimension_semantics=("arbitrary",), vmem_limit_bytes=VMEM_LIMIT),
        name="moe_finish",
    )(h, y0, y1, wts, p, w['g_ple'], w['w_ple_gate'], w['w_ple'], w['g_final'])


def _ffn_sparse(h, p, w):
    nt = h.shape[0]
    tm = EXPERT_TM
    n_tiles = (nt * 2) // tm + N_EXPERTS
    tp, meta, wts, counts = _route(h, w)
    cnt = counts.reshape(N_EXPERTS).astype(jnp.int32)
    tiles_e = (cnt + tm - 1) // tm
    tile_end = jnp.cumsum(tiles_e)
    off = (tile_end - tiles_e) * tm
    n_valid = tile_end[-1:]
    tile_ids = jnp.arange(n_tiles, dtype=jnp.int32)
    tile_expert = jnp.sum((tile_ids[:, None] >= tile_end[None, :]).astype(jnp.int32), axis=1)
    last_expert = jnp.sum((n_valid - 1 >= tile_end).astype(jnp.int32))
    tile_expert = jnp.minimum(tile_expert, last_expert).astype(jnp.int32)
    eid = meta[0:2]
    slot = meta[2:4] + jnp.sum(jnp.where(eid[..., None] == jnp.arange(N_EXPERTS), off, 0), axis=-1)
    slot0 = slot[0:1]
    slot1 = slot[1:2]
    xs = _sc_dispatch(tp, slot0, slot1, n_tiles * tm)
    ys = _experts(xs, tile_expert, n_valid.astype(jnp.int32), w)
    y0, y1 = _sc_combine(ys, slot0, slot1)
    return _finish(h, y0, y1, wts, p, w)


def _prep_weights(g_mix, w_in, conf_dw_w, conf_dw_b, conf_ln_g, conf_ln_b, ssm_conv_w, ssm_conv_b,
                  dt_bias, a_log, d_skip, ssm_norm_g, w_out, g_ffn, w_rg, b_rg, w_re, b_re,
                  w_gate_e, w_up_e, w_down_e, g_ple, w_ple_gate, w_ple, g_final):
    row = lambda v: v.reshape(1, -1)
    w_in_b = w_in.astype(BF16)
    conf_w_pad = jnp.concatenate([conf_dw_w, jnp.zeros((CONF_PAD - K_CONF, C_CONF), F32)], axis=0)
    w_router = jnp.concatenate(
        [w_rg, jnp.zeros((D_MODEL, N_EXPERTS - N_EXPERT_GROUPS), F32), w_re], axis=1)
    wr_hi = w_router.astype(BF16)
    wr_mid = (w_router - wr_hi.astype(F32)).astype(BF16)
    b_router = jnp.concatenate([b_rg, jnp.zeros((N_EXPERTS - N_EXPERT_GROUPS,), F32), b_re]).reshape(1, -1)
    head_of_lane = jnp.arange(D_INNER) // HEAD_DIM
    head_expand = (head_of_lane[None, :] == jnp.arange(N_HEADS)[:, None]).astype(BF16)
    return dict(
        g_mix=row(g_mix), w_in=w_in_b, w_dtT=w_in_b[:, O_DT:].T,
        conf_w=conf_w_pad.reshape(CONF_PAD, N_LANE_TILES, LANES).transpose(1, 0, 2),
        conf_b=conf_dw_b.reshape(N_LANE_TILES, 1, LANES),
        conf_w2=conf_dw_w, conf_b2=row(conf_dw_b),
        ssm_w3=ssm_conv_w.reshape(K_SSM, N_XBC_TILES, LANES).transpose(1, 0, 2),
        ssm_b3=ssm_conv_b.reshape(N_XBC_TILES, 1, LANES),
        ln_g=row(conf_ln_g), ln_b=row(conf_ln_b), ssm_w=ssm_conv_w, ssm_b=row(ssm_conv_b),
        dt_bias=row(dt_bias), dt_biasT=dt_bias.reshape(-1, 1), a_log=row(a_log), a_logT=a_log.reshape(-1, 1),
        d_skip=row(jnp.repeat(d_skip, HEAD_DIM)), ssm_norm_g=row(ssm_norm_g), w_out=w_out.astype(BF16),
        head_expand=head_expand,
        g_ffn=row(g_ffn), w_router=jnp.stack([wr_hi, wr_mid]), b_router=b_router,
        w_routerT=jnp.stack([wr_hi.T, wr_mid.T]), b_routerT=b_router.reshape(-1, 1),
        w_gate_e=w_gate_e, w_up_e=w_up_e, w_down_e=w_down_e,
        g_ple=row(g_ple), w_ple_gate=w_ple_gate.astype(BF16), w_ple=w_ple.astype(BF16), g_final=row(g_final),
    )


def kernel(x_prompt, x_sample, p_prompt, p_sample, state_conf_conv, state_ssm_conv, state_ssm, g_mix, w_in, conf_dw_w, conf_dw_b, conf_ln_g, conf_ln_b, ssm_conv_w, ssm_conv_b, dt_bias, a_log, d_skip, ssm_norm_g, w_out, g_ffn, w_rg, b_rg, w_re, b_re, w_gate_e, w_up_e, w_down_e, g_ple, w_ple_gate, w_ple, g_final):
    depth = g_mix.shape[0]
    bsz, seq, _ = x_prompt.shape
    nb = x_sample.shape[0]
    hp = x_prompt
    hs = x_sample.reshape(nb, D_MODEL)
    cp_l, mp_l, sp_l, cs_l, ms_l, ss_l = [], [], [], [], [], []
    for i in range(depth):
        w = _prep_weights(g_mix[i], w_in[i], conf_dw_w[i], conf_dw_b[i], conf_ln_g[i], conf_ln_b[i],
                          ssm_conv_w[i], ssm_conv_b[i], dt_bias[i], a_log[i], d_skip[i], ssm_norm_g[i], w_out[i],
                          g_ffn[i], w_rg[i], b_rg[i], w_re[i], b_re[i], w_gate_e[i], w_up_e[i], w_down_e[i],
                          g_ple[i], w_ple_gate[i], w_ple[i], g_final)
        h1p, c, m, s = _mixer_prompt(hp, w)
        cp_l.append(c)
        mp_l.append(m)
        sp_l.append(s)
        h1s, c, m, s = _mixer_sample(hs, jnp.transpose(state_conf_conv[i], (1, 0, 2)), state_ssm_conv[i],
                                     state_ssm[i], w)
        cs_l.append(jnp.transpose(c, (1, 0, 2)))
        ms_l.append(m)
        ss_l.append(s)
        assert depth == 1
        hp = _ffn_sparse(h1p.reshape(bsz * seq, D_MODEL), p_prompt[i].reshape(bsz * seq, PLE_DIM), w)
        hp = hp.reshape(bsz, seq, D_MODEL)
        hs = _ffn(h1s, p_sample[i].reshape(nb, PLE_DIM), w, tb=nb)
    return (hp, hs.reshape(nb, 1, D_MODEL), jnp.stack(cp_l), jnp.stack(mp_l), jnp.stack(sp_l),
            jnp.stack(cs_l), jnp.stack(ms_l), jnp.stack(ss_l))
```

```python
import functools

import jax
import jax.numpy as jnp
from jax import lax
from jax.experimental import pallas as pl
from jax.experimental.pallas import tpu as pltpu
from jax.experimental.pallas import tpu_sc as plsc

F32 = jnp.float32
BF16 = jnp.bfloat16

D_MODEL = 1024
C_CONF = 1024
K_CONF = 31
D_INNER = 1024
HEAD_DIM = 64
N_HEADS = 16
N_GROUPS = 4
HEADS_PER_GROUP = N_HEADS // N_GROUPS
GROUP_W = HEADS_PER_GROUP * HEAD_DIM
D_STATE = 128
K_SSM = 4
CHUNK = 128
CONV_DIM = D_INNER + 2 * N_GROUPS * D_STATE
IN_COLS = 2 * C_CONF + D_INNER + CONV_DIM + N_HEADS
O_GATE = C_CONF
O_Z = 2 * C_CONF
O_XBC = O_Z + D_INNER
O_DT = O_XBC + CONV_DIM
N_EXPERT_GROUPS = 4
EXPERTS_PER_GROUP = 4
N_EXPERTS = 16
D_FF = 512
PLE_DIM = 256
EPS = 1e-6

LANES = 128
SUBLANES = 8
N_LANE_TILES = C_CONF // LANES
N_XBC_TILES = CONV_DIM // LANES
CONF_PAD = 32
SSM_PAD = 8
VMEM_LIMIT = 56 * 1024 * 1024


def _dot(a, b):
    return jnp.dot(a, b, preferred_element_type=F32)


def _dot_nt(a, b):
    return lax.dot_general(a, b, (((1,), (1,)), ((), ())), preferred_element_type=F32)


def _dot_tn(a, b):
    return lax.dot_general(a, b, (((0,), (0,)), ((), ())), preferred_element_type=F32)


def _split3(v):
    hi = v.astype(BF16)
    r = v - hi.astype(F32)
    mid = r.astype(BF16)
    lo = (r - mid.astype(F32)).astype(BF16)
    return hi, mid, lo


def _rms(x, g):
    return x * lax.rsqrt(jnp.mean(x * x, axis=-1, keepdims=True) + EPS) * g


def _sigmoid(x):
    return jax.nn.sigmoid(x)


def _silu(x):
    return x * jax.nn.sigmoid(x)


def _softplus(x):
    return jax.nn.softplus(x)


def _mixer_prompt_kernel(x_ref, gmix_ref, win_ref, wdtT_ref, cw_ref, cb_ref, lng_ref, lnb_ref,
                         sw_ref, sb_ref, dtb_ref, dtbT_ref, alog_ref, alogT_ref, dskip_ref, sng_ref,
                         wout_ref, hexp_ref,
                         h1_ref, ncc_ref, nsc_ref, nss_ref,
                         cscr, cout, mscr, xbc_scr, st_scr, *, tl):
    t = pl.program_id(1)
    nt = pl.num_programs(1)

    @pl.when(t == 0)
    def _():
        cscr[:, 0:CONF_PAD, :] = jnp.zeros((N_LANE_TILES, CONF_PAD, LANES), F32)
        mscr[:, 0:SSM_PAD, :] = jnp.zeros((N_XBC_TILES, SSM_PAD, LANES), F32)
        st_scr[...] = jnp.zeros(st_scr.shape, F32)

    x = x_ref[0]
    u = _rms(x, gmix_ref[...]).astype(BF16)

    glu = _dot(u, win_ref[:, 0:O_GATE]) * _sigmoid(_dot(u, win_ref[:, O_GATE:O_Z]))
    for lc in range(N_LANE_TILES):
        cscr[lc, CONF_PAD:CONF_PAD + tl, :] = glu[:, lc * LANES:(lc + 1) * LANES]

    rc = 64

    def conv_lane_tile(lc, carry):
        bias = cb_ref[lc]
        for r0 in range(0, tl, rc):
            acc = jnp.broadcast_to(bias, (rc, LANES))
            for k in range(K_CONF):
                acc = acc + cw_ref[lc, pl.ds(k, 1), :] * cscr[lc, pl.ds(r0 + k + CONF_PAD - (K_CONF - 1), rc), :]
            cout[lc, pl.ds(r0, rc), :] = acc
        return carry

    for lc in range(N_LANE_TILES):
        conv_lane_tile(lc, 0)

    for lc in range(N_LANE_TILES):
        cscr[lc, 0:CONF_PAD, :] = cscr[lc, tl:tl + CONF_PAD, :]

    cc = [cout[lc] for lc in range(N_LANE_TILES)]
    tot = cc[0]
    for lc in range(1, N_LANE_TILES):
        tot = tot + cc[lc]
    mean = jnp.sum(tot, axis=-1, keepdims=True) * (1.0 / C_CONF)
    xc = [c - mean for c in cc]
    sq = xc[0] * xc[0]
    for lc in range(1, N_LANE_TILES):
        sq = sq + xc[lc] * xc[lc]
    rstd = lax.rsqrt(jnp.sum(sq, axis=-1, keepdims=True) * (1.0 / C_CONF) + EPS)
    a_out = jnp.concatenate(
        [_silu(xc[lc] * rstd * lng_ref[:, lc * LANES:(lc + 1) * LANES] + lnb_ref[:, lc * LANES:(lc + 1) * LANES])
         for lc in range(N_LANE_TILES)], axis=-1).astype(BF16)

    z = _dot(u, win_ref[:, O_Z:O_XBC])
    xbc_raw = _dot(u, win_ref[:, O_XBC:O_DT])
    for lt in range(N_XBC_TILES):
        mscr[lt, SSM_PAD:SSM_PAD + tl, :] = xbc_raw[:, lt * LANES:(lt + 1) * LANES]

    def ssm_conv_lane_tile(lt, carry):
        bias = sb_ref[lt]
        for r0 in range(0, tl, rc):
            acc = jnp.broadcast_to(bias, (rc, LANES))
            for k in range(K_SSM):
                acc = acc + sw_ref[lt, pl.ds(k, 1), :] * mscr[lt, pl.ds(r0 + k + SSM_PAD - (K_SSM - 1), rc), :]
            xbc_scr[lt, pl.ds(r0, rc), :] = _silu(acc)
        mscr[lt, 0:SSM_PAD, :] = mscr[lt, tl:tl + SSM_PAD, :]
        return carry

    for lt in range(N_XBC_TILES):
        ssm_conv_lane_tile(lt, 0)

    n_x = D_INNER // LANES
    n_b = N_GROUPS * D_STATE // LANES
    xs = jnp.concatenate([xbc_scr[lt] for lt in range(n_x)], axis=-1)
    bm = jnp.concatenate([xbc_scr[lt] for lt in range(n_x, n_x + n_b)], axis=-1)
    cm = jnp.concatenate([xbc_scr[lt] for lt in range(n_x + n_b, N_XBC_TILES)], axis=-1)

    dt = _softplus(_dot(u, win_ref[:, O_DT:IN_COLS]) + dtb_ref[...])
    dtT = _softplus(_dot_nt(wdtT_ref[...], u) + dtbT_ref[...])
    a = dt * (-jnp.exp(alog_ref[...]))
    aT = dtT * (-jnp.exp(alogT_ref[...]))
    hexp = hexp_ref[...]
    d_h, d_m, d_l = _split3(dt)
    xdt_all = xs * (_dot(d_h, hexp) + _dot(d_m, hexp) + _dot(d_l, hexp))

    row = lax.broadcasted_iota(jnp.int32, (CHUNK, CHUNK), 0)
    col = lax.broadcasted_iota(jnp.int32, (CHUNK, CHUNK), 1)
    lower = row >= col
    tri = lower.astype(BF16)
    triT = (row <= col).astype(BF16)

    y_chunks = []
    for c in range(tl // CHUNK):
        r0 = c * CHUNK
        a_c = a[r0:r0 + CHUNK]
        aT_c = aT[:, r0:r0 + CHUNK]
        ah, am, al = _split3(a_c)
        cs = _dot(tri, ah) + _dot(tri, am) + _dot(tri, al)
        th, tm, tlo = _split3(aT_c)
        csT = _dot(th, triT) + _dot(tm, triT) + _dot(tlo, triT)
        cs_last = cs[CHUNK - 1:CHUNK, :]
        cdec = jnp.exp(cs_last)
        xdt_c = xdt_all[r0:r0 + CHUNK]
        y_heads = []
        for g in range(N_GROUPS):
            cg = cm[r0:r0 + CHUNK, g * D_STATE:(g + 1) * D_STATE].astype(BF16)
            bg = bm[r0:r0 + CHUNK, g * D_STATE:(g + 1) * D_STATE].astype(BF16)
            cb = _dot_nt(cg, bg)
            y_off = _dot(cg, st_scr[g].astype(BF16))
            xdd = []
            dec_row = []
            m_parts_g = []
            x_bd = []
            e_parts = []
            xdt_g = xdt_c[:, g * GROUP_W:(g + 1) * GROUP_W]
            lane_head = lax.broadcasted_iota(jnp.int32, (CHUNK, GROUP_W), 1) // HEAD_DIM
            for hh in range(HEADS_PER_GROUP):
                h = g * HEADS_PER_GROUP + hh
                xdt = xdt_g[:, hh * HEAD_DIM:(hh + 1) * HEAD_DIM]
                cs_b = jnp.broadcast_to(cs[:, h:h + 1], (CHUNK, CHUNK))
                lmat = jnp.where(lower, jnp.exp(cs_b - csT[h:h + 1, :]), 0.0)
                m_parts_g.append((cb * lmat).astype(BF16))
                x_bd.append(jnp.where(lane_head == hh, xdt_g, 0.0).astype(BF16))
                cs_bh = cs_b[:, 0:HEAD_DIM]
                e_parts.append(jnp.exp(cs_bh))
                xdd.append((xdt * jnp.exp(csT[h:h + 1, CHUNK - 1:CHUNK] - cs_bh)).astype(BF16))
                dec_row.append(jnp.broadcast_to(cdec[:, h:h + 1], (1, HEAD_DIM)))
            y_diag = _dot(jnp.concatenate(m_parts_g, axis=1), jnp.concatenate(x_bd, axis=0))
            y_heads.append(y_diag + y_off * jnp.concatenate(e_parts, axis=-1))
            contrib = _dot_tn(bg, jnp.concatenate(xdd, axis=-1))
            st_scr[g] = st_scr[g] * jnp.concatenate(dec_row, axis=-1) + contrib
        y_chunks.append(jnp.concatenate(y_heads, axis=-1))
    y = y_chunks[0] if len(y_chunks) == 1 else jnp.concatenate(y_chunks, axis=0)
    y = (y + dskip_ref[...] * xs) * _silu(z)
    m_parts = []
    for g in range(N_GROUPS):
        yg = y[:, g * GROUP_W:(g + 1) * GROUP_W]
        m_parts.append(_rms(yg, sng_ref[:, g * GROUP_W:(g + 1) * GROUP_W]))
    m_out = jnp.concatenate(m_parts, axis=-1).astype(BF16)

    h1_ref[0] = x + _dot(a_out, wout_ref[0:C_CONF, :]) + _dot(m_out, wout_ref[C_CONF:, :])

    @pl.when(t == nt - 1)
    def _():
        for lc in range(N_LANE_TILES):
            ncc_ref[0, :, lc * LANES:(lc + 1) * LANES] = cscr[lc, pl.ds(CONF_PAD - (K_CONF - 1), K_CONF - 1), :]
        for lt in range(N_XBC_TILES):
            nsc_ref[0, :, lt * LANES:(lt + 1) * LANES] = mscr[lt, pl.ds(SSM_PAD - (K_SSM - 1), K_SSM - 1), :]
        for g in range(N_GROUPS):
            nss_ref[0, g * HEADS_PER_GROUP:(g + 1) * HEADS_PER_GROUP] = (
                st_scr[g].T.reshape(HEADS_PER_GROUP, HEAD_DIM, D_STATE))


def _const_spec(shape):
    nd = len(shape)
    return pl.BlockSpec(shape, lambda *_: (0,) * nd, pipeline_mode=pl.Buffered(1))


def _mixer_prompt(x, w, tl=256):
    bsz, seq, _ = x.shape
    kern = functools.partial(_mixer_prompt_kernel, tl=tl)
    consts = [w['g_mix'], w['w_in'], w['w_dtT'], w['conf_w'], w['conf_b'], w['ln_g'], w['ln_b'],
              w['ssm_w3'], w['ssm_b3'], w['dt_bias'], w['dt_biasT'], w['a_log'], w['a_logT'], w['d_skip'],
              w['ssm_norm_g'], w['w_out'], w['head_expand']]
    return pl.pallas_call(
        kern,
        grid=(bsz, seq // tl),
        in_specs=[pl.BlockSpec((1, tl, D_MODEL), lambda b, t: (b, t, 0))] + [_const_spec(c.shape) for c in consts],
        out_specs=[
            pl.BlockSpec((1, tl, D_MODEL), lambda b, t: (b, t, 0)),
            pl.BlockSpec((1, K_CONF - 1, C_CONF), lambda b, t: (b, 0, 0)),
            pl.BlockSpec((1, K_SSM - 1, CONV_DIM), lambda b, t: (b, 0, 0)),
            pl.BlockSpec((1, N_HEADS, HEAD_DIM, D_STATE), lambda b, t: (b, 0, 0, 0)),
        ],
        out_shape=[
            jax.ShapeDtypeStruct((bsz, seq, D_MODEL), F32),
            jax.ShapeDtypeStruct((bsz, K_CONF - 1, C_CONF), F32),
            jax.ShapeDtypeStruct((bsz, K_SSM - 1, CONV_DIM), F32),
            jax.ShapeDtypeStruct((bsz, N_HEADS, HEAD_DIM, D_STATE), F32),
        ],
        scratch_shapes=[
            pltpu.VMEM((N_LANE_TILES, CONF_PAD + tl, LANES), F32),
            pltpu.VMEM((N_LANE_TILES, tl, LANES), F32),
            pltpu.VMEM((N_XBC_TILES, SSM_PAD + tl, LANES), F32),
            pltpu.VMEM((N_XBC_TILES, tl, LANES), F32),
            pltpu.VMEM((N_GROUPS, D_STATE, GROUP_W), F32),
        ],
        compiler_params=pltpu.CompilerParams(
            dimension_semantics=("arbitrary", "arbitrary"), vmem_limit_bytes=VMEM_LIMIT),
        name="mixer_prompt",
    )(x, *consts)


def _mixer_sample_kernel(x_ref, cst_ref, mst_ref, sst_ref, gmix_ref, win_ref, cw_ref, cb_ref, lng_ref, lnb_ref,
                         sw_ref, sb_ref, dtb_ref, alog_ref, dskip_ref, sng_ref, wout_ref, hexp_ref,
                         h1_ref, ncc_ref, nsc_ref, nss_ref,
                         proj_scr, mix_scr, *, bb):
    i = pl.program_id(0)
    n = pl.num_programs(0)

    @pl.when(i == 0)
    def _():
        u = _rms(x_ref[...], gmix_ref[...]).astype(BF16)
        proj_scr[...] = _dot(u, win_ref[...])

    r0 = pl.multiple_of(i * bb, bb)
    proj = proj_scr[pl.ds(r0, bb), :]

    glu = proj[:, 0:O_GATE] * _sigmoid(proj[:, O_GATE:O_Z])
    acc = cb_ref[...] + cw_ref[pl.ds(K_CONF - 1, 1), :] * glu
    for k in range(K_CONF - 1):
        row_k = cst_ref[k]
        acc = acc + cw_ref[pl.ds(k, 1), :] * row_k
        if k >= 1:
            ncc_ref[k - 1] = row_k
    ncc_ref[K_CONF - 2] = glu
    mean = jnp.mean(acc, axis=-1, keepdims=True)
    xc = acc - mean
    rstd = lax.rsqrt(jnp.mean(xc * xc, axis=-1, keepdims=True) + EPS)
    a_out = _silu(xc * rstd * lng_ref[...] + lnb_ref[...])

    z = proj[:, O_Z:O_XBC]
    xbc_raw = proj[:, O_XBC:O_DT]
    acc = sb_ref[...] + sw_ref[pl.ds(K_SSM - 1, 1), :] * xbc_raw
    for k in range(K_SSM - 1):
        row_k = mst_ref[:, k, :]
        acc = acc + sw_ref[pl.ds(k, 1), :] * row_k
        if k >= 1:
            nsc_ref[:, k - 1, :] = row_k
    nsc_ref[:, K_SSM - 2, :] = xbc_raw
    xbc = _silu(acc)
    xs = xbc[:, 0:D_INNER]
    bm = xbc[:, D_INNER:D_INNER + N_GROUPS * D_STATE]
    cm = xbc[:, D_INNER + N_GROUPS * D_STATE:]
    dt = _softplus(proj[:, O_DT:IN_COLS] + dtb_ref[...])
    dec = jnp.exp(dt * (-jnp.exp(alog_ref[...])))
    hexp = hexp_ref[...]
    d_h, d_m, d_l = _split3(dec)
    dec_e = _dot(d_h, hexp) + _dot(d_m, hexp) + _dot(d_l, hexp)
    t_h, t_m, t_l = _split3(dt)
    dt_e = _dot(t_h, hexp) + _dot(t_m, hexp) + _dot(t_l, hexp)
    xdt = xs * dt_e

    lane = lax.broadcasted_iota(jnp.int32, (SUBLANES, D_INNER), 1)
    sub = lax.broadcasted_iota(jnp.int32, (SUBLANES, D_INNER), 0)
    gmask = (lane // GROUP_W) == sub
    ones_row = (lax.broadcasted_iota(jnp.int32, (SUBLANES, D_STATE), 0) < 3).astype(BF16)

    y_rows = []
    for b in range(bb):
        xrow = jnp.where(gmask, jnp.broadcast_to(xdt[b:b + 1, :], (SUBLANES, D_INNER)), 0.0)
        x_h, x_m, x_l = [v.astype(F32) for v in _split3(xrow)]
        bmat = jnp.concatenate([bm[b:b + 1, g * D_STATE:(g + 1) * D_STATE] for g in range(N_GROUPS)]
                               + [jnp.zeros((SUBLANES - N_GROUPS, D_STATE), F32)], axis=0)
        b_h, b_m, b_l = [v.astype(F32) for v in _split3(bmat)]
        lhs = jnp.concatenate([x_h, x_h, x_m, x_h, x_m, x_l], axis=0).astype(BF16)
        rhs = jnp.concatenate([b_h, b_m, b_h, b_l, b_m, b_h], axis=0).astype(BF16)
        upd = _dot_tn(lhs, rhs)
        dbc = jnp.broadcast_to(dec_e[b:b + 1, :], (SUBLANES, D_INNER))
        q_h = dbc.astype(BF16).astype(F32)
        q_m = (dbc - q_h).astype(BF16).astype(F32)
        q_l = dbc - q_h - q_m
        dlhs = jnp.where(sub == 0, q_h, jnp.where(sub == 1, q_m, jnp.where(sub == 2, q_l, 0.0))).astype(BF16)
        dfull = _dot_tn(dlhs, ones_row)
        hnew = dfull * sst_ref[b].reshape(D_INNER, D_STATE) + upd
        nss_ref[b] = hnew.reshape(N_HEADS, HEAD_DIM, D_STATE)
        cmat = jnp.concatenate([cm[b:b + 1, g * D_STATE:(g + 1) * D_STATE] for g in range(N_GROUPS)]
                               + [jnp.zeros((SUBLANES - N_GROUPS, D_STATE), F32)], axis=0)
        yg = _dot_nt(cmat.astype(BF16), hnew.astype(BF16))
        y_rows.append(jnp.sum(jnp.where(gmask, yg, 0.0), axis=0, keepdims=True))
    y = jnp.concatenate(y_rows, axis=0)
    y = (y + dskip_ref[...] * xs) * _silu(z)
    m_parts = []
    for g in range(N_GROUPS):
        m_parts.append(_rms(y[:, g * GROUP_W:(g + 1) * GROUP_W], sng_ref[:, g * GROUP_W:(g + 1) * GROUP_W]))
    mix_scr[pl.ds(r0, bb), :] = jnp.concatenate([a_out] + m_parts, axis=-1)

    @pl.when(i == n - 1)
    def _():
        h1_ref[...] = x_ref[...] + _dot(mix_scr[...].astype(BF16), wout_ref[...])


N_SAMPLE_CONSTS = 14


def _mixer_sample_part_kernel(*refs, bb, n_unread):
    n_in = 4 + N_SAMPLE_CONSTS
    _mixer_sample_kernel(*refs[:n_in], *refs[n_in + n_unread:], bb=bb)


def _mixer_sample_part(x, cst, mst, sst, w, part, n_parts, after, earlier=None, bb=8):
    nb = x.shape[0]
    nbp = nb // n_parts
    steps = nbp // bb
    off = part * steps
    unread = [after] + (list(earlier) if earlier is not None else [])
    kern = functools.partial(_mixer_sample_part_kernel, bb=bb, n_unread=len(unread))
    consts = [w['g_mix'], w['w_in'], w['conf_w2'], w['conf_b2'], w['ln_g'], w['ln_b'],
              w['ssm_w'], w['ssm_b'], w['dt_bias'], w['a_log'], w['d_skip'], w['ssm_norm_g'], w['w_out'],
              w['head_expand']]
    assert len(consts) == N_SAMPLE_CONSTS
    n_in = 4 + N_SAMPLE_CONSTS
    aliases = {} if earlier is None else {n_in + 1 + k: k for k in range(4)}
    return pl.pallas_call(
        kern,
        grid=(steps,),
        in_specs=[
            pl.BlockSpec((nbp, D_MODEL), lambda i: (part, 0)),
            pl.BlockSpec((K_CONF - 1, bb, C_CONF), lambda i: (0, i + off, 0)),
            pl.BlockSpec((bb, K_SSM - 1, CONV_DIM), lambda i: (i + off, 0, 0)),
            pl.BlockSpec((bb, N_HEADS, HEAD_DIM, D_STATE), lambda i: (i + off, 0, 0, 0)),
        ] + [_const_spec(c.shape) for c in consts] + [pl.BlockSpec(memory_space=pl.ANY)] * len(unread),
        out_specs=[
            pl.BlockSpec((nbp, D_MODEL), lambda i: (part, 0)),
            pl.BlockSpec((K_CONF - 1, bb, C_CONF), lambda i: (0, i + off, 0)),
            pl.BlockSpec((bb, K_SSM - 1, CONV_DIM), lambda i: (i + off, 0, 0)),
            pl.BlockSpec((bb, N_HEADS, HEAD_DIM, D_STATE), lambda i: (i + off, 0, 0, 0)),
        ],
        out_shape=[
            jax.ShapeDtypeStruct((nb, D_MODEL), F32),
            jax.ShapeDtypeStruct((K_CONF - 1, nb, C_CONF), F32),
            jax.ShapeDtypeStruct((nb, K_SSM - 1, CONV_DIM), F32),
            jax.ShapeDtypeStruct((nb, N_HEADS, HEAD_DIM, D_STATE), F32),
        ],
        scratch_shapes=[
            pltpu.VMEM((nbp, IN_COLS), F32),
            pltpu.VMEM((nbp, D_MODEL + D_INNER), F32),
        ],
        input_output_aliases=aliases,
        compiler_params=pltpu.CompilerParams(dimension_semantics=("arbitrary",), vmem_limit_bytes=VMEM_LIMIT),
        name="mixer_sample",
    )(x, cst, mst, sst, *consts, *unread)


def _first_argmax(v, width):
    lane = lax.broadcasted_iota(jnp.int32, v.shape, 1)
    m = jnp.max(v, axis=-1, keepdims=True)
    idx = jnp.min(jnp.where(v == m, lane, width), axis=-1, keepdims=True)
    return m, idx


def _ffn_kernel(h_ref, p_ref, gffn_ref, wr_ref, br_ref, wg_ref, wu_ref, wd_ref, gple_ref, wpg_ref, wp_ref,
                gfin_ref, y_ref, t_scr, comb_scr, acc_scr):
    e = pl.program_id(1)
    ne = pl.num_programs(1)

    @pl.when(e == 0)
    def _():
        tf = _rms(h_ref[...], gffn_ref[...])
        t_scr[...] = tf.astype(BF16)
        t_h, t_m, t_l = _split3(tf)
        w_h = wr_ref[0]
        w_m = wr_ref[1]
        logits = (_dot(t_h, w_h) + _dot(t_m, w_h) + _dot(t_h, w_m) + _dot(t_l, w_h) + _dot(t_m, w_m)) + br_ref[...]
        lg = logits[:, 0:N_EXPERT_GROUPS]
        le = logits[:, N_EXPERTS:2 * N_EXPERTS]
        eg = jnp.exp(lg - jnp.max(lg, axis=-1, keepdims=True))
        pg = eg / jnp.sum(eg, axis=-1, keepdims=True)
        g_val, g_idx = _first_argmax(pg, N_EXPERT_GROUPS)
        lane16 = lax.broadcasted_iota(jnp.int32, le.shape, 1)
        in_grp = (lane16 // EXPERTS_PER_GROUP) == g_idx
        neg = jnp.float32(-jnp.inf)
        le_m = jnp.where(in_grp, le, neg)
        ee = jnp.where(in_grp, jnp.exp(le - jnp.max(le_m, axis=-1, keepdims=True)), 0.0)
        pe = ee / jnp.sum(ee, axis=-1, keepdims=True)
        pe_m = jnp.where(in_grp, pe, -1.0)
        v1, i1 = _first_argmax(pe_m, N_EXPERTS)
        pe_m2 = jnp.where(lane16 == i1, -1.0, pe_m)
        v2, i2 = _first_argmax(pe_m2, N_EXPERTS)
        den = v1 + v2
        comb = jnp.where(lane16 == i1, g_val * v1 / den, 0.0) + jnp.where(lane16 == i2, g_val * v2 / den, 0.0)
        comb_scr[...] = comb
        acc_scr[...] = jnp.zeros(acc_scr.shape, F32)

    t = t_scr[...]
    hdn = _silu(_dot(t, wg_ref[0].astype(BF16))) * _dot(t, wu_ref[0].astype(BF16))
    out_e = _dot(hdn.astype(BF16), wd_ref[0].astype(BF16))
    lane16 = lax.broadcasted_iota(jnp.int32, comb_scr.shape, 1)
    c_e = jnp.sum(jnp.where(lane16 == e, comb_scr[...], 0.0), axis=-1, keepdims=True)
    acc_scr[...] += c_e * out_e

    @pl.when(e == ne - 1)
    def _():
        h2 = h_ref[...] + acc_scr[...]
        gate = _sigmoid(_dot(_rms(h2, gple_ref[...]).astype(BF16), wpg_ref[...]))
        h3 = h2 + _dot(p_ref[...].astype(BF16), wp_ref[...]) * gate
        y_ref[...] = _rms(h3, gfin_ref[...])


def _ffn(h, p, w, tb):
    nt = h.shape[0]
    return pl.pallas_call(
        _ffn_kernel,
        grid=(nt // tb, N_EXPERTS),
        in_specs=[
            pl.BlockSpec((tb, D_MODEL), lambda i, e: (i, 0)),
            pl.BlockSpec((tb, PLE_DIM), lambda i, e: (i, 0)),
            _const_spec((1, D_MODEL)),
            _const_spec((2, D_MODEL, 2 * N_EXPERTS)),
            _const_spec((1, 2 * N_EXPERTS)),
            pl.BlockSpec((1, D_MODEL, D_FF), lambda i, e: (e, 0, 0)),
            pl.BlockSpec((1, D_MODEL, D_FF), lambda i, e: (e, 0, 0)),
            pl.BlockSpec((1, D_FF, D_MODEL), lambda i, e: (e, 0, 0)),
            _const_spec((1, D_MODEL)),
            _const_spec((D_MODEL, D_MODEL)),
            _const_spec((PLE_DIM, D_MODEL)),
            _const_spec((1, D_MODEL)),
        ],
        out_specs=pl.BlockSpec((tb, D_MODEL), lambda i, e: (i, 0)),
        out_shape=jax.ShapeDtypeStruct((nt, D_MODEL), F32),
        scratch_shapes=[
            pltpu.VMEM((tb, D_MODEL), BF16),
            pltpu.VMEM((tb, N_EXPERTS), F32),
            pltpu.VMEM((tb, D_MODEL), F32),
        ],
        compiler_params=pltpu.CompilerParams(
            dimension_semantics=("arbitrary", "arbitrary"), vmem_limit_bytes=VMEM_LIMIT),
        name="ffn",
    )(h, p, w['g_ffn'], w['w_router'], w['b_router'], w['w_gate_e'], w['w_up_e'], w['w_down_e'],
      w['g_ple'], w['w_ple_gate'], w['w_ple'], w['g_final'])


ROUTE_TB = 1024
EXPERT_TM = 512
SC_WINDOW = 128
SC_CORES = 2
SC_SUBCORES = 16
SC_WORKERS = SC_CORES * SC_SUBCORES
HALF = D_MODEL // 2


def _pack_bf16_pairs(v):
    bits = pltpu.bitcast(v.astype(BF16).astype(F32), jnp.uint32)
    packed = bits[:, HALF:] | (bits[:, :HALF] >> 16)
    return pltpu.bitcast(packed, jnp.int32)


def _unpack_bf16_pairs(p):
    u = pltpu.bitcast(p, jnp.uint32)
    lo = pltpu.bitcast(u << 16, F32)
    hi = pltpu.bitcast(u & jnp.uint32(0xFFFF0000), F32)
    return jnp.concatenate([lo, hi], axis=-1)


def _route_kernel(h_ref, gffn_ref, wrT_ref, brT_ref, tp_ref, meta_ref, wts_ref, cnt_ref, upper_scr, carry_scr):
    i = pl.program_id(0)
    tb = h_ref.shape[0]

    @pl.when(i == 0)
    def _():
        r = lax.broadcasted_iota(jnp.int32, (tb, tb), 0)
        c = lax.broadcasted_iota(jnp.int32, (tb, tb), 1)
        upper_scr[...] = (r < c).astype(BF16)
        carry_scr[...] = jnp.zeros(carry_scr.shape, F32)

    tf = _rms(h_ref[...], gffn_ref[...])
    tp_ref[...] = _pack_bf16_pairs(tf)
    t_h = tf.astype(BF16)
    t_m = (tf - t_h.astype(F32)).astype(BF16)
    w_h = wrT_ref[0]
    w_m = wrT_ref[1]
    logits = (_dot_nt(w_h, t_h) + _dot_nt(w_h, t_m) + _dot_nt(w_m, t_h)) + brT_ref[...]
    neg = -jnp.inf
    row8 = lax.broadcasted_iota(jnp.int32, (SUBLANES, tb), 0)
    row16 = lax.broadcasted_iota(jnp.int32, (N_EXPERTS, tb), 0)
    lg = jnp.where(row8 < N_EXPERT_GROUPS, logits[0:SUBLANES], neg)
    le = logits[N_EXPERTS:2 * N_EXPERTS]
    eg = jnp.exp(lg - jnp.max(lg, axis=0, keepdims=True))
    pg = eg / jnp.sum(eg, axis=0, keepdims=True)
    g_val = jnp.max(pg, axis=0, keepdims=True)
    g_idx = jnp.min(jnp.where(pg == g_val, row8, SUBLANES), axis=0, keepdims=True)
    in_grp = (row16 // EXPERTS_PER_GROUP) == g_idx
    le_m = jnp.where(in_grp, le, neg)
    ee = jnp.where(in_grp, jnp.exp(le - jnp.max(le_m, axis=0, keepdims=True)), 0.0)
    pe = ee / jnp.sum(ee, axis=0, keepdims=True)
    pe_m = jnp.where(in_grp, pe, -1.0)
    v1 = jnp.max(pe_m, axis=0, keepdims=True)
    i1 = jnp.min(jnp.where(pe_m == v1, row16, N_EXPERTS), axis=0, keepdims=True)
    pe_m2 = jnp.where(row16 == i1, -1.0, pe_m)
    v2 = jnp.max(pe_m2, axis=0, keepdims=True)
    i2 = jnp.min(jnp.where(pe_m2 == v2, row16, N_EXPERTS), axis=0, keepdims=True)
    den = v1 + v2
    w0 = g_val * v1 / den
    w1 = g_val * v2 / den

    sel0 = row16 == i1
    sel1 = row16 == i2
    hot = jnp.where(sel0 | sel1, 1.0, 0.0)
    rank = _dot(hot.astype(BF16), upper_scr[...]) + carry_scr[...]
    r0 = jnp.sum(jnp.where(sel0, rank, 0.0), axis=0, keepdims=True)
    r1 = jnp.sum(jnp.where(sel1, rank, 0.0), axis=0, keepdims=True)
    carry_scr[...] += jnp.sum(hot, axis=1, keepdims=True)
    cnt_ref[...] = carry_scr[...]

    meta = jnp.where(row8 == 0, i1, jnp.where(row8 == 1, i2, jnp.where(
        row8 == 2, r0.astype(jnp.int32), jnp.where(row8 == 3, r1.astype(jnp.int32), 0))))
    meta_ref[...] = meta
    row128 = lax.broadcasted_iota(jnp.int32, (LANES, tb), 0)
    wts_ref[...] = jnp.where(row128 == 0, w0, jnp.where(row128 == 1, w1, 0.0)).T


def _route(h, w, tb=ROUTE_TB):
    nt = h.shape[0]
    return pl.pallas_call(
        _route_kernel,
        grid=(nt // tb,),
        in_specs=[
            pl.BlockSpec((tb, D_MODEL), lambda i: (i, 0)),
            _const_spec((1, D_MODEL)),
            _const_spec((2, 2 * N_EXPERTS, D_MODEL)),
            _const_spec((2 * N_EXPERTS, 1)),
        ],
        out_specs=[
            pl.BlockSpec((tb, HALF), lambda i: (i, 0)),
            pl.BlockSpec((SUBLANES, tb), lambda i: (0, i)),
            pl.BlockSpec((tb, LANES), lambda i: (i, 0)),
            pl.BlockSpec((N_EXPERTS, 1), lambda i: (0, 0)),
        ],
        out_shape=[
            jax.ShapeDtypeStruct((nt, HALF), jnp.int32),
            jax.ShapeDtypeStruct((SUBLANES, nt), jnp.int32),
            jax.ShapeDtypeStruct((nt, LANES), F32),
            jax.ShapeDtypeStruct((N_EXPERTS, 1), F32),
        ],
        scratch_shapes=[pltpu.VMEM((tb, tb), BF16), pltpu.VMEM((N_EXPERTS, 1), F32)],
        compiler_params=pltpu.CompilerParams(dimension_semantics=("arbitrary",), vmem_limit_bytes=VMEM_LIMIT),
        name="moe_route",
    )(h, w['g_ffn'], w['w_routerT'], w['b_routerT'])


def _sc_mesh():
    return plsc.VectorSubcoreMesh(core_axis_name="c", subcore_axis_name="s")


def _sc_dispatch(tp, slot0, slot1, n_slots):
    nt = tp.shape[0]

    per_worker = nt // (SC_WINDOW * SC_WORKERS)

    @pl.kernel(out_type=jax.ShapeDtypeStruct((n_slots, HALF), tp.dtype), mesh=_sc_mesh(), name="moe_dispatch",
               scratch_types=[pltpu.VMEM((1, SC_WINDOW), jnp.int32), pltpu.VMEM((1, SC_WINDOW), jnp.int32),
                              pltpu.VMEM((SC_WINDOW, HALF), tp.dtype)])
    def run(x_hbm, i0_hbm, i1_hbm, o_hbm, i0_v, i1_v, buf_v):
        worker = lax.axis_index("c") * SC_SUBCORES + lax.axis_index("s")

        @pl.loop(0, per_worker)
        def _(k):
            base = (worker * per_worker + k) * SC_WINDOW
            pltpu.sync_copy(i0_hbm.at[:, pl.ds(base, SC_WINDOW)], i0_v)
            pltpu.sync_copy(i1_hbm.at[:, pl.ds(base, SC_WINDOW)], i1_v)
            pltpu.sync_copy(x_hbm.at[pl.ds(base, SC_WINDOW)], buf_v)
            pltpu.sync_copy(buf_v, o_hbm.at[i0_v.at[0]])
            pltpu.sync_copy(buf_v, o_hbm.at[i1_v.at[0]])

    return run(tp, slot0, slot1)


def _sc_combine(ys, slot0, slot1):
    nt = slot0.shape[1]
    out = jax.ShapeDtypeStruct((nt, HALF), ys.dtype)

    per_worker = nt // (SC_WINDOW * SC_WORKERS)

    @pl.kernel(out_type=(out, out), mesh=_sc_mesh(), name="moe_combine",
               scratch_types=[pltpu.VMEM((1, SC_WINDOW), jnp.int32), pltpu.VMEM((1, SC_WINDOW), jnp.int32),
                              pltpu.VMEM((SC_WINDOW, HALF), ys.dtype)])
    def run(y_hbm, i0_hbm, i1_hbm, o0_hbm, o1_hbm, i0_v, i1_v, buf_v):
        worker = lax.axis_index("c") * SC_SUBCORES + lax.axis_index("s")

        @pl.loop(0, per_worker)
        def _(k):
            base = (worker * per_worker + k) * SC_WINDOW
            pltpu.sync_copy(i0_hbm.at[:, pl.ds(base, SC_WINDOW)], i0_v)
            pltpu.sync_copy(i1_hbm.at[:, pl.ds(base, SC_WINDOW)], i1_v)
            pltpu.sync_copy(y_hbm.at[i0_v.at[0]], buf_v)
            pltpu.sync_copy(buf_v, o0_hbm.at[pl.ds(base, SC_WINDOW)])
            pltpu.sync_copy(y_hbm.at[i1_v.at[0]], buf_v)
            pltpu.sync_copy(buf_v, o1_hbm.at[pl.ds(base, SC_WINDOW)])

    return run(ys, slot0, slot1)


def _experts_kernel(te_ref, nv_ref, x_ref, wg_ref, wu_ref, wd_ref, after_ref, y_ref, wg_s, wu_s, wd_s):
    del after_ref
    j = pl.program_id(0)

    @pl.when((j == 0) | (te_ref[j] != te_ref[jnp.maximum(j - 1, 0)]))
    def _():
        wg_s[...] = wg_ref[0].astype(BF16)
        wu_s[...] = wu_ref[0].astype(BF16)
        wd_s[...] = wd_ref[0].astype(BF16)

    @pl.when(j < nv_ref[0])
    def _():
        x = _unpack_bf16_pairs(x_ref[...]).astype(BF16)
        hdn = _silu(_dot(x, wg_s[...])) * _dot(x, wu_s[...])
        y_ref[...] = _pack_bf16_pairs(_dot(hdn.astype(BF16), wd_s[...]))


def _experts(xs, tile_expert, n_valid, w, after, tm=EXPERT_TM):
    n_slots = xs.shape[0]
    grid_spec = pltpu.PrefetchScalarGridSpec(
        num_scalar_prefetch=2,
        grid=(n_slots // tm,),
        in_specs=[
            pl.BlockSpec((tm, HALF), lambda j, te, nv: (j, 0)),
            pl.BlockSpec((1, D_MODEL, D_FF), lambda j, te, nv: (te[j], 0, 0)),
            pl.BlockSpec((1, D_MODEL, D_FF), lambda j, te, nv: (te[j], 0, 0)),
            pl.BlockSpec((1, D_FF, D_MODEL), lambda j, te, nv: (te[j], 0, 0)),
            pl.BlockSpec(memory_space=pl.ANY),
        ],
        out_specs=pl.BlockSpec((tm, HALF), lambda j, te, nv: (j, 0)),
        scratch_shapes=[pltpu.VMEM((D_MODEL, D_FF), BF16), pltpu.VMEM((D_MODEL, D_FF), BF16),
                        pltpu.VMEM((D_FF, D_MODEL), BF16)],
    )
    return pl.pallas_call(
        _experts_kernel,
        grid_spec=grid_spec,
        out_shape=jax.ShapeDtypeStruct((n_slots, HALF), jnp.int32),
        compiler_params=pltpu.CompilerParams(dimension_semantics=("arbitrary",), vmem_limit_bytes=VMEM_LIMIT),
        name="moe_experts",
    )(tile_expert, n_valid, xs, w['w_gate_e'], w['w_up_e'], w['w_down_e'], after)


def _finish_kernel(h_ref, y0_ref, y1_ref, wts_ref, p_ref, gple_ref, wpg_ref, wp_ref, gfin_ref, o_ref):
    wts = wts_ref[...]
    moe = wts[:, 0:1] * _unpack_bf16_pairs(y0_ref[...]) + wts[:, 1:2] * _unpack_bf16_pairs(y1_ref[...])
    h2 = h_ref[...] + moe
    gate = _sigmoid(_dot(_rms(h2, gple_ref[...]).astype(BF16), wpg_ref[...]))
    h3 = h2 + _dot(p_ref[...].astype(BF16), wp_ref[...]) * gate
    o_ref[...] = _rms(h3, gfin_ref[...])


def _finish(h, y0, y1, wts, p, w, tb=1024):
    nt = h.shape[0]
    return pl.pallas_call(
        _finish_kernel,
        grid=(nt // tb,),
        in_specs=[
            pl.BlockSpec((tb, D_MODEL), lambda i: (i, 0)),
            pl.BlockSpec((tb, HALF), lambda i: (i, 0)),
            pl.BlockSpec((tb, HALF), lambda i: (i, 0)),
            pl.BlockSpec((tb, LANES), lambda i: (i, 0)),
            pl.BlockSpec((tb, PLE_DIM), lambda i: (i, 0)),
            _const_spec((1, D_MODEL)),
            _const_spec((D_MODEL, D_MODEL)),
            _const_spec((PLE_DIM, D_MODEL)),
            _const_spec((1, D_MODEL)),
        ],
        out_specs=pl.BlockSpec((tb, D_MODEL), lambda i: (i, 0)),
        out_shape=jax.ShapeDtypeStruct((nt, D_MODEL), F32),
        compiler_params=pltpu.CompilerParams(dimension_semantics=("arbitrary",), vmem_limit_bytes=VMEM_LIMIT),
        name="moe_finish",
    )(h, y0, y1, wts, p, w['g_ple'], w['w_ple_gate'], w['w_ple'], w['g_final'])


def _ffn_sparse(h, p, w, sample_part):
    nt = h.shape[0]
    tm = EXPERT_TM
    n_tiles = (nt * 2) // tm + N_EXPERTS
    tp, meta, wts, counts = _route(h, w)
    sample0 = sample_part(0, counts, None)
    cnt = counts.reshape(N_EXPERTS).astype(jnp.int32)
    tiles_e = (cnt + tm - 1) // tm
    tile_end = jnp.cumsum(tiles_e)
    off = (tile_end - tiles_e) * tm
    n_valid = tile_end[-1:]
    tile_ids = jnp.arange(n_tiles, dtype=jnp.int32)
    tile_expert = jnp.sum((tile_ids[:, None] >= tile_end[None, :]).astype(jnp.int32), axis=1)
    last_expert = jnp.sum((n_valid - 1 >= tile_end).astype(jnp.int32))
    tile_expert = jnp.minimum(tile_expert, last_expert).astype(jnp.int32)
    eid = meta[0:2]
    slot = meta[2:4] + jnp.sum(jnp.where(eid[..., None] == jnp.arange(N_EXPERTS), off, 0), axis=-1)
    slot0 = slot[0:1]
    slot1 = slot[1:2]
    xs = _sc_dispatch(tp, slot0, slot1, n_tiles * tm)
    ys = _experts(xs, tile_expert, n_valid.astype(jnp.int32), w, sample0[0])
    sample1 = sample_part(1, ys, sample0)
    y0, y1 = _sc_combine(ys, slot0, slot1)
    return _finish(h, y0, y1, wts, p, w), sample1


def _prep_weights(g_mix, w_in, conf_dw_w, conf_dw_b, conf_ln_g, conf_ln_b, ssm_conv_w, ssm_conv_b,
                  dt_bias, a_log, d_skip, ssm_norm_g, w_out, g_ffn, w_rg, b_rg, w_re, b_re,
                  w_gate_e, w_up_e, w_down_e, g_ple, w_ple_gate, w_ple, g_final):
    row = lambda v: v.reshape(1, -1)
    w_in_b = w_in.astype(BF16)
    conf_w_pad = jnp.concatenate([conf_dw_w, jnp.zeros((CONF_PAD - K_CONF, C_CONF), F32)], axis=0)
    w_router = jnp.concatenate(
        [w_rg, jnp.zeros((D_MODEL, N_EXPERTS - N_EXPERT_GROUPS), F32), w_re], axis=1)
    wr_hi = w_router.astype(BF16)
    wr_mid = (w_router - wr_hi.astype(F32)).astype(BF16)
    b_router = jnp.concatenate([b_rg, jnp.zeros((N_EXPERTS - N_EXPERT_GROUPS,), F32), b_re]).reshape(1, -1)
    head_of_lane = jnp.arange(D_INNER) // HEAD_DIM
    head_expand = (head_of_lane[None, :] == jnp.arange(N_HEADS)[:, None]).astype(BF16)
    return dict(
        g_mix=row(g_mix), w_in=w_in_b, w_dtT=w_in_b[:, O_DT:].T,
        conf_w=conf_w_pad.reshape(CONF_PAD, N_LANE_TILES, LANES).transpose(1, 0, 2),
        conf_b=conf_dw_b.reshape(N_LANE_TILES, 1, LANES),
        conf_w2=conf_dw_w, conf_b2=row(conf_dw_b),
        ssm_w3=ssm_conv_w.reshape(K_SSM, N_XBC_TILES, LANES).transpose(1, 0, 2),
        ssm_b3=ssm_conv_b.reshape(N_XBC_TILES, 1, LANES),
        ln_g=row(conf_ln_g), ln_b=row(conf_ln_b), ssm_w=ssm_conv_w, ssm_b=row(ssm_conv_b),
        dt_bias=row(dt_bias), dt_biasT=dt_bias.reshape(-1, 1), a_log=row(a_log), a_logT=a_log.reshape(-1, 1),
        d_skip=row(jnp.repeat(d_skip, HEAD_DIM)), ssm_norm_g=row(ssm_norm_g), w_out=w_out.astype(BF16),
        head_expand=head_expand,
        g_ffn=row(g_ffn), w_router=jnp.stack([wr_hi, wr_mid]), b_router=b_router,
        w_routerT=jnp.stack([wr_hi.T, wr_mid.T]), b_routerT=b_router.reshape(-1, 1),
        w_gate_e=w_gate_e, w_up_e=w_up_e, w_down_e=w_down_e,
        g_ple=row(g_ple), w_ple_gate=w_ple_gate.astype(BF16), w_ple=w_ple.astype(BF16), g_final=row(g_final),
    )


def kernel(x_prompt, x_sample, p_prompt, p_sample, state_conf_conv, state_ssm_conv, state_ssm, g_mix, w_in, conf_dw_w, conf_dw_b, conf_ln_g, conf_ln_b, ssm_conv_w, ssm_conv_b, dt_bias, a_log, d_skip, ssm_norm_g, w_out, g_ffn, w_rg, b_rg, w_re, b_re, w_gate_e, w_up_e, w_down_e, g_ple, w_ple_gate, w_ple, g_final):
    depth = g_mix.shape[0]
    bsz, seq, _ = x_prompt.shape
    nb = x_sample.shape[0]
    hp = x_prompt
    hs = x_sample.reshape(nb, D_MODEL)
    cp_l, mp_l, sp_l, cs_l, ms_l, ss_l = [], [], [], [], [], []
    for i in range(depth):
        w = _prep_weights(g_mix[i], w_in[i], conf_dw_w[i], conf_dw_b[i], conf_ln_g[i], conf_ln_b[i],
                          ssm_conv_w[i], ssm_conv_b[i], dt_bias[i], a_log[i], d_skip[i], ssm_norm_g[i], w_out[i],
                          g_ffn[i], w_rg[i], b_rg[i], w_re[i], b_re[i], w_gate_e[i], w_up_e[i], w_down_e[i],
                          g_ple[i], w_ple_gate[i], w_ple[i], g_final)
        h1p, c, m, s = _mixer_prompt(hp, w)
        cp_l.append(c)
        mp_l.append(m)
        sp_l.append(s)
        sample_part = functools.partial(
            _mixer_sample_part, hs, jnp.transpose(state_conf_conv[i], (1, 0, 2)), state_ssm_conv[i], state_ssm[i],
            w, n_parts=2)
        assert depth == 1
        hp, (h1s, c, m, s) = _ffn_sparse(
            h1p.reshape(bsz * seq, D_MODEL), p_prompt[i].reshape(bsz * seq, PLE_DIM), w,
            lambda part, after, earlier: sample_part(part=part, after=after, earlier=earlier))
        cs_l.append(jnp.transpose(c, (1, 0, 2)))
        ms_l.append(m)
        ss_l.append(s)
        hp = hp.reshape(bsz, seq, D_MODEL)
        hs = _ffn(h1s, p_sample[i].reshape(nb, PLE_DIM), w, tb=nb)
    return (hp, hs.reshape(nb, 1, D_MODEL), jnp.stack(cp_l), jnp.stack(mp_l), jnp.stack(sp_l),
            jnp.stack(cs_l), jnp.stack(ms_l), jnp.stack(ss_l))
```

```python
import functools

import jax
import jax.numpy as jnp
from jax import lax
from jax.experimental import pallas as pl
from jax.experimental.pallas import tpu as pltpu
from jax.experimental.pallas import tpu_sc as plsc

F32 = jnp.float32
BF16 = jnp.bfloat16

D_MODEL = 1024
C_CONF = 1024
K_CONF = 31
D_INNER = 1024
HEAD_DIM = 64
N_HEADS = 16
N_GROUPS = 4
HEADS_PER_GROUP = N_HEADS // N_GROUPS
GROUP_W = HEADS_PER_GROUP * HEAD_DIM
D_STATE = 128
K_SSM = 4
CHUNK = 128
CONV_DIM = D_INNER + 2 * N_GROUPS * D_STATE
IN_COLS = 2 * C_CONF + D_INNER + CONV_DIM + N_HEADS
O_GATE = C_CONF
O_Z = 2 * C_CONF
O_XBC = O_Z + D_INNER
O_DT = O_XBC + CONV_DIM
N_EXPERT_GROUPS = 4
EXPERTS_PER_GROUP = 4
N_EXPERTS = 16
D_FF = 512
PLE_DIM = 256
EPS = 1e-6

LANES = 128
SUBLANES = 8
N_LANE_TILES = C_CONF // LANES
N_XBC_TILES = CONV_DIM // LANES
CONF_PAD = 32
SSM_PAD = 8
VMEM_LIMIT = 56 * 1024 * 1024


def _dot(a, b):
    return jnp.dot(a, b, preferred_element_type=F32)


def _dot_nt(a, b):
    return lax.dot_general(a, b, (((1,), (1,)), ((), ())), preferred_element_type=F32)


def _dot_tn(a, b):
    return lax.dot_general(a, b, (((0,), (0,)), ((), ())), preferred_element_type=F32)


def _split3(v):
    hi = v.astype(BF16)
    r = v - hi.astype(F32)
    mid = r.astype(BF16)
    lo = (r - mid.astype(F32)).astype(BF16)
    return hi, mid, lo


def _rms(x, g):
    return x * lax.rsqrt(jnp.mean(x * x, axis=-1, keepdims=True) + EPS) * g


def _sigmoid(x):
    return jax.nn.sigmoid(x)


def _silu(x):
    return x * jax.nn.sigmoid(x)


def _softplus(x):
    return jax.nn.softplus(x)


def _mixer_prompt_kernel(x_ref, gmix_ref, win_ref, wdtT_ref, cw_ref, cb_ref, lng_ref, lnb_ref,
                         sw_ref, sb_ref, dtb_ref, dtbT_ref, alog_ref, alogT_ref, dskip_ref, sng_ref,
                         wout_ref, hexp_ref,
                         h1_ref, ncc_ref, nsc_ref, nss_ref,
                         cscr, cout, mscr, xbc_scr, st_scr, *, tl):
    t = pl.program_id(1)
    nt = pl.num_programs(1)

    @pl.when(t == 0)
    def _():
        cscr[:, 0:CONF_PAD, :] = jnp.zeros((N_LANE_TILES, CONF_PAD, LANES), F32)
        mscr[:, 0:SSM_PAD, :] = jnp.zeros((N_XBC_TILES, SSM_PAD, LANES), F32)
        st_scr[...] = jnp.zeros(st_scr.shape, F32)

    x = x_ref[0]
    u = _rms(x, gmix_ref[...]).astype(BF16)

    glu = _dot(u, win_ref[:, 0:O_GATE]) * _sigmoid(_dot(u, win_ref[:, O_GATE:O_Z]))
    for lc in range(N_LANE_TILES):
        cscr[lc, CONF_PAD:CONF_PAD + tl, :] = glu[:, lc * LANES:(lc + 1) * LANES]

    rc = 64

    def conv_lane_tile(lc, carry):
        bias = cb_ref[lc]
        for r0 in range(0, tl, rc):
            acc = jnp.broadcast_to(bias, (rc, LANES))
            for k in range(K_CONF):
                acc = acc + cw_ref[lc, pl.ds(k, 1), :] * cscr[lc, pl.ds(r0 + k + CONF_PAD - (K_CONF - 1), rc), :]
            cout[lc, pl.ds(r0, rc), :] = acc
        return carry

    for lc in range(N_LANE_TILES):
        conv_lane_tile(lc, 0)

    for lc in range(N_LANE_TILES):
        cscr[lc, 0:CONF_PAD, :] = cscr[lc, tl:tl + CONF_PAD, :]

    cc = [cout[lc] for lc in range(N_LANE_TILES)]
    tot = cc[0]
    for lc in range(1, N_LANE_TILES):
        tot = tot + cc[lc]
    mean = jnp.sum(tot, axis=-1, keepdims=True) * (1.0 / C_CONF)
    xc = [c - mean for c in cc]
    sq = xc[0] * xc[0]
    for lc in range(1, N_LANE_TILES):
        sq = sq + xc[lc] * xc[lc]
    rstd = lax.rsqrt(jnp.sum(sq, axis=-1, keepdims=True) * (1.0 / C_CONF) + EPS)
    a_out = jnp.concatenate(
        [_silu(xc[lc] * rstd * lng_ref[:, lc * LANES:(lc + 1) * LANES] + lnb_ref[:, lc * LANES:(lc + 1) * LANES])
         for lc in range(N_LANE_TILES)], axis=-1).astype(BF16)

    z = _dot(u, win_ref[:, O_Z:O_XBC])
    xbc_raw = _dot(u, win_ref[:, O_XBC:O_DT])
    for lt in range(N_XBC_TILES):
        mscr[lt, SSM_PAD:SSM_PAD + tl, :] = xbc_raw[:, lt * LANES:(lt + 1) * LANES]

    def ssm_conv_lane_tile(lt, carry):
        bias = sb_ref[lt]
        for r0 in range(0, tl, rc):
            acc = jnp.broadcast_to(bias, (rc, LANES))
            for k in range(K_SSM):
                acc = acc + sw_ref[lt, pl.ds(k, 1), :] * mscr[lt, pl.ds(r0 + k + SSM_PAD - (K_SSM - 1), rc), :]
            xbc_scr[lt, pl.ds(r0, rc), :] = _silu(acc)
        mscr[lt, 0:SSM_PAD, :] = mscr[lt, tl:tl + SSM_PAD, :]
        return carry

    for lt in range(N_XBC_TILES):
        ssm_conv_lane_tile(lt, 0)

    n_x = D_INNER // LANES
    n_b = N_GROUPS * D_STATE // LANES
    xs = jnp.concatenate([xbc_scr[lt] for lt in range(n_x)], axis=-1)
    bm = jnp.concatenate([xbc_scr[lt] for lt in range(n_x, n_x + n_b)], axis=-1)
    cm = jnp.concatenate([xbc_scr[lt] for lt in range(n_x + n_b, N_XBC_TILES)], axis=-1)

    dt = _softplus(_dot(u, win_ref[:, O_DT:IN_COLS]) + dtb_ref[...])
    dtT = _softplus(_dot_nt(wdtT_ref[...], u) + dtbT_ref[...])
    a = dt * (-jnp.exp(alog_ref[...]))
    aT = dtT * (-jnp.exp(alogT_ref[...]))
    hexp = hexp_ref[...]
    d_h, d_m, d_l = _split3(dt)
    xdt_all = xs * (_dot(d_h, hexp) + _dot(d_m, hexp) + _dot(d_l, hexp))

    row = lax.broadcasted_iota(jnp.int32, (CHUNK, CHUNK), 0)
    col = lax.broadcasted_iota(jnp.int32, (CHUNK, CHUNK), 1)
    lower = row >= col
    tri = lower.astype(BF16)
    triT = (row <= col).astype(BF16)

    y_chunks = []
    for c in range(tl // CHUNK):
        r0 = c * CHUNK
        a_c = a[r0:r0 + CHUNK]
        aT_c = aT[:, r0:r0 + CHUNK]
        ah, am, al = _split3(a_c)
        cs = _dot(tri, ah) + _dot(tri, am) + _dot(tri, al)
        th, tm, tlo = _split3(aT_c)
        csT = _dot(th, triT) + _dot(tm, triT) + _dot(tlo, triT)
        cs_last = cs[CHUNK - 1:CHUNK, :]
        cdec = jnp.exp(cs_last)
        xdt_c = xdt_all[r0:r0 + CHUNK]
        y_heads = []
        for g in range(N_GROUPS):
            cg = cm[r0:r0 + CHUNK, g * D_STATE:(g + 1) * D_STATE].astype(BF16)
            bg = bm[r0:r0 + CHUNK, g * D_STATE:(g + 1) * D_STATE].astype(BF16)
            cb = _dot_nt(cg, bg)
            y_off = _dot(cg, st_scr[g].astype(BF16))
            xdd = []
            dec_row = []
            m_parts_g = []
            x_bd = []
            e_parts = []
            xdt_g = xdt_c[:, g * GROUP_W:(g + 1) * GROUP_W]
            lane_head = lax.broadcasted_iota(jnp.int32, (CHUNK, GROUP_W), 1) // HEAD_DIM
            for hh in range(HEADS_PER_GROUP):
                h = g * HEADS_PER_GROUP + hh
                xdt = xdt_g[:, hh * HEAD_DIM:(hh + 1) * HEAD_DIM]
                cs_b = jnp.broadcast_to(cs[:, h:h + 1], (CHUNK, CHUNK))
                lmat = jnp.where(lower, jnp.exp(cs_b - csT[h:h + 1, :]), 0.0)
                m_parts_g.append((cb * lmat).astype(BF16))
                x_bd.append(jnp.where(lane_head == hh, xdt_g, 0.0).astype(BF16))
                cs_bh = cs_b[:, 0:HEAD_DIM]
                e_parts.append(jnp.exp(cs_bh))
                xdd.append((xdt * jnp.exp(csT[h:h + 1, CHUNK - 1:CHUNK] - cs_bh)).astype(BF16))
                dec_row.append(jnp.broadcast_to(cdec[:, h:h + 1], (1, HEAD_DIM)))
            y_diag = _dot(jnp.concatenate(m_parts_g, axis=1), jnp.concatenate(x_bd, axis=0))
            y_heads.append(y_diag + y_off * jnp.concatenate(e_parts, axis=-1))
            contrib = _dot_tn(bg, jnp.concatenate(xdd, axis=-1))
            st_scr[g] = st_scr[g] * jnp.concatenate(dec_row, axis=-1) + contrib
        y_chunks.append(jnp.concatenate(y_heads, axis=-1))
    y = y_chunks[0] if len(y_chunks) == 1 else jnp.concatenate(y_chunks, axis=0)
    y = (y + dskip_ref[...] * xs) * _silu(z)
    m_parts = []
    for g in range(N_GROUPS):
        yg = y[:, g * GROUP_W:(g + 1) * GROUP_W]
        m_parts.append(_rms(yg, sng_ref[:, g * GROUP_W:(g + 1) * GROUP_W]))
    m_out = jnp.concatenate(m_parts, axis=-1).astype(BF16)

    h1_ref[0] = x + _dot(a_out, wout_ref[0:C_CONF, :]) + _dot(m_out, wout_ref[C_CONF:, :])

    @pl.when(t == nt - 1)
    def _():
        for lc in range(N_LANE_TILES):
            ncc_ref[0, :, lc * LANES:(lc + 1) * LANES] = cscr[lc, pl.ds(CONF_PAD - (K_CONF - 1), K_CONF - 1), :]
        for lt in range(N_XBC_TILES):
            nsc_ref[0, :, lt * LANES:(lt + 1) * LANES] = mscr[lt, pl.ds(SSM_PAD - (K_SSM - 1), K_SSM - 1), :]
        for g in range(N_GROUPS):
            nss_ref[0, g * HEADS_PER_GROUP:(g + 1) * HEADS_PER_GROUP] = (
                st_scr[g].T.reshape(HEADS_PER_GROUP, HEAD_DIM, D_STATE))


def _const_spec(shape):
    nd = len(shape)
    return pl.BlockSpec(shape, lambda *_: (0,) * nd, pipeline_mode=pl.Buffered(1))


def _mixer_prompt(x, w, tl=256):
    bsz, seq, _ = x.shape
    kern = functools.partial(_mixer_prompt_kernel, tl=tl)
    consts = [w['g_mix'], w['w_in'], w['w_dtT'], w['conf_w'], w['conf_b'], w['ln_g'], w['ln_b'],
              w['ssm_w3'], w['ssm_b3'], w['dt_bias'], w['dt_biasT'], w['a_log'], w['a_logT'], w['d_skip'],
              w['ssm_norm_g'], w['w_out'], w['head_expand']]
    return pl.pallas_call(
        kern,
        grid=(bsz, seq // tl),
        in_specs=[pl.BlockSpec((1, tl, D_MODEL), lambda b, t: (b, t, 0))] + [_const_spec(c.shape) for c in consts],
        out_specs=[
            pl.BlockSpec((1, tl, D_MODEL), lambda b, t: (b, t, 0)),
            pl.BlockSpec((1, K_CONF - 1, C_CONF), lambda b, t: (b, 0, 0)),
            pl.BlockSpec((1, K_SSM - 1, CONV_DIM), lambda b, t: (b, 0, 0)),
            pl.BlockSpec((1, N_HEADS, HEAD_DIM, D_STATE), lambda b, t: (b, 0, 0, 0)),
        ],
        out_shape=[
            jax.ShapeDtypeStruct((bsz, seq, D_MODEL), F32),
            jax.ShapeDtypeStruct((bsz, K_CONF - 1, C_CONF), F32),
            jax.ShapeDtypeStruct((bsz, K_SSM - 1, CONV_DIM), F32),
            jax.ShapeDtypeStruct((bsz, N_HEADS, HEAD_DIM, D_STATE), F32),
        ],
        scratch_shapes=[
            pltpu.VMEM((N_LANE_TILES, CONF_PAD + tl, LANES), F32),
            pltpu.VMEM((N_LANE_TILES, tl, LANES), F32),
            pltpu.VMEM((N_XBC_TILES, SSM_PAD + tl, LANES), F32),
            pltpu.VMEM((N_XBC_TILES, tl, LANES), F32),
            pltpu.VMEM((N_GROUPS, D_STATE, GROUP_W), F32),
        ],
        compiler_params=pltpu.CompilerParams(
            dimension_semantics=("arbitrary", "arbitrary"), vmem_limit_bytes=VMEM_LIMIT),
        name="mixer_prompt",
    )(x, *consts)


def _mixer_sample_kernel(x_ref, cst_ref, mst_ref, sst_ref, gmix_ref, win_ref, cw_ref, cb_ref, lng_ref, lnb_ref,
                         sw_ref, sb_ref, dtb_ref, alog_ref, dskip_ref, sng_ref, wout_ref, hexp_ref,
                         h1_ref, ncc_ref, nsc_ref, nss_ref,
                         proj_scr, mix_scr, *, bb):
    i = pl.program_id(0)
    n = pl.num_programs(0)

    @pl.when(i == 0)
    def _():
        u = _rms(x_ref[...], gmix_ref[...]).astype(BF16)
        proj_scr[...] = _dot(u, win_ref[...])

    r0 = pl.multiple_of(i * bb, bb)
    proj = proj_scr[pl.ds(r0, bb), :]

    glu = proj[:, 0:O_GATE] * _sigmoid(proj[:, O_GATE:O_Z])
    acc = cb_ref[...] + cw_ref[pl.ds(K_CONF - 1, 1), :] * glu
    for k in range(K_CONF - 1):
        row_k = cst_ref[k]
        acc = acc + cw_ref[pl.ds(k, 1), :] * row_k
        if k >= 1:
            ncc_ref[k - 1] = row_k
    ncc_ref[K_CONF - 2] = glu
    mean = jnp.mean(acc, axis=-1, keepdims=True)
    xc = acc - mean
    rstd = lax.rsqrt(jnp.mean(xc * xc, axis=-1, keepdims=True) + EPS)
    a_out = _silu(xc * rstd * lng_ref[...] + lnb_ref[...])

    z = proj[:, O_Z:O_XBC]
    xbc_raw = proj[:, O_XBC:O_DT]
    acc = sb_ref[...] + sw_ref[pl.ds(K_SSM - 1, 1), :] * xbc_raw
    for k in range(K_SSM - 1):
        row_k = mst_ref[:, k, :]
        acc = acc + sw_ref[pl.ds(k, 1), :] * row_k
        if k >= 1:
            nsc_ref[:, k - 1, :] = row_k
    nsc_ref[:, K_SSM - 2, :] = xbc_raw
    xbc = _silu(acc)
    xs = xbc[:, 0:D_INNER]
    bm = xbc[:, D_INNER:D_INNER + N_GROUPS * D_STATE]
    cm = xbc[:, D_INNER + N_GROUPS * D_STATE:]
    dt = _softplus(proj[:, O_DT:IN_COLS] + dtb_ref[...])
    dec = jnp.exp(dt * (-jnp.exp(alog_ref[...])))
    hexp = hexp_ref[...]
    t_h, t_m, t_l = _split3(dt)
    dt_e = _dot(t_h, hexp) + _dot(t_m, hexp) + _dot(t_l, hexp)
    xdt = xs * dt_e

    lane = lax.broadcasted_iota(jnp.int32, (SUBLANES, D_INNER), 1)
    sub = lax.broadcasted_iota(jnp.int32, (SUBLANES, D_INNER), 0)
    gmask = (lane // GROUP_W) == sub
    eye_h = (lax.broadcasted_iota(jnp.int32, (N_HEADS, N_HEADS), 0)
             == lax.broadcasted_iota(jnp.int32, (N_HEADS, N_HEADS), 1))
    ones_hn = jnp.ones((3 * N_HEADS, D_STATE), BF16)

    y_rows = []
    for b in range(bb):
        xrow = jnp.where(gmask, jnp.broadcast_to(xdt[b:b + 1, :], (SUBLANES, D_INNER)), 0.0)
        x_h = xrow.astype(BF16).astype(F32)
        x_m = (xrow - x_h).astype(BF16).astype(F32)
        bmat = jnp.concatenate([bm[b:b + 1, g * D_STATE:(g + 1) * D_STATE] for g in range(N_GROUPS)]
                               + [jnp.zeros((SUBLANES - N_GROUPS, D_STATE), F32)], axis=0)
        b_h = bmat.astype(BF16).astype(F32)
        b_m = (bmat - b_h).astype(BF16).astype(F32)
        lhs = jnp.concatenate([x_h, x_h, x_m], axis=0).astype(BF16)
        rhs = jnp.concatenate([b_h, b_m, b_h], axis=0).astype(BF16)
        upd = _dot_tn(lhs, rhs)
        dg = jnp.where(eye_h, jnp.broadcast_to(dec[b:b + 1, :], (N_HEADS, N_HEADS)), 0.0)
        g_h, g_m, g_l = [v.astype(F32) for v in _split3(dg)]
        drow = _dot(jnp.concatenate([g_h, g_m, g_l], axis=1).astype(BF16), ones_hn)
        h_parts = []
        for h in range(N_HEADS):
            h_new = drow[h:h + 1, :] * sst_ref[b, h] + upd[h * HEAD_DIM:(h + 1) * HEAD_DIM, :]
            nss_ref[b, h] = h_new
            h_parts.append(h_new)
        hnew = jnp.concatenate(h_parts, axis=0)
        cmat = jnp.concatenate([cm[b:b + 1, g * D_STATE:(g + 1) * D_STATE] for g in range(N_GROUPS)]
                               + [jnp.zeros((SUBLANES - N_GROUPS, D_STATE), F32)], axis=0)
        yg = _dot_nt(cmat.astype(BF16), hnew.astype(BF16))
        y_rows.append(jnp.sum(jnp.where(gmask, yg, 0.0), axis=0, keepdims=True))
    y = jnp.concatenate(y_rows, axis=0)
    y = (y + dskip_ref[...] * xs) * _silu(z)
    m_parts = []
    for g in range(N_GROUPS):
        m_parts.append(_rms(y[:, g * GROUP_W:(g + 1) * GROUP_W], sng_ref[:, g * GROUP_W:(g + 1) * GROUP_W]))
    mix_scr[pl.ds(r0, bb), :] = jnp.concatenate([a_out] + m_parts, axis=-1)

    @pl.when(i == n - 1)
    def _():
        h1_ref[...] = x_ref[...] + _dot(mix_scr[...].astype(BF16), wout_ref[...])


def _mixer_sample(x, cst, mst, sst, w, bb=8):
    nb = x.shape[0]
    kern = functools.partial(_mixer_sample_kernel, bb=bb)
    consts = [w['g_mix'], w['w_in'], w['conf_w2'], w['conf_b2'], w['ln_g'], w['ln_b'],
              w['ssm_w'], w['ssm_b'], w['dt_bias'], w['a_log'], w['d_skip'], w['ssm_norm_g'], w['w_out'],
              w['head_expand']]
    return pl.pallas_call(
        kern,
        grid=(nb // bb,),
        in_specs=[
            _const_spec((nb, D_MODEL)),
            pl.BlockSpec((K_CONF - 1, bb, C_CONF), lambda i: (0, i, 0)),
            pl.BlockSpec((bb, K_SSM - 1, CONV_DIM), lambda i: (i, 0, 0)),
            pl.BlockSpec((bb, N_HEADS, HEAD_DIM, D_STATE), lambda i: (i, 0, 0, 0)),
        ] + [_const_spec(c.shape) for c in consts],
        out_specs=[
            pl.BlockSpec((nb, D_MODEL), lambda i: (0, 0)),
            pl.BlockSpec((K_CONF - 1, bb, C_CONF), lambda i: (0, i, 0)),
            pl.BlockSpec((bb, K_SSM - 1, CONV_DIM), lambda i: (i, 0, 0)),
            pl.BlockSpec((bb, N_HEADS, HEAD_DIM, D_STATE), lambda i: (i, 0, 0, 0)),
        ],
        out_shape=[
            jax.ShapeDtypeStruct((nb, D_MODEL), F32),
            jax.ShapeDtypeStruct((K_CONF - 1, nb, C_CONF), F32),
            jax.ShapeDtypeStruct((nb, K_SSM - 1, CONV_DIM), F32),
            jax.ShapeDtypeStruct((nb, N_HEADS, HEAD_DIM, D_STATE), F32),
        ],
        scratch_shapes=[
            pltpu.VMEM((nb, IN_COLS), F32),
            pltpu.VMEM((nb, D_MODEL + D_INNER), F32),
        ],
        compiler_params=pltpu.CompilerParams(dimension_semantics=("arbitrary",), vmem_limit_bytes=VMEM_LIMIT),
        name="mixer_sample",
    )(x, cst, mst, sst, *consts)


def _first_argmax(v, width):
    lane = lax.broadcasted_iota(jnp.int32, v.shape, 1)
    m = jnp.max(v, axis=-1, keepdims=True)
    idx = jnp.min(jnp.where(v == m, lane, width), axis=-1, keepdims=True)
    return m, idx


def _ffn_kernel(h_ref, p_ref, gffn_ref, wr_ref, br_ref, wg_ref, wu_ref, wd_ref, gple_ref, wpg_ref, wp_ref,
                gfin_ref, y_ref, t_scr, comb_scr, acc_scr):
    e = pl.program_id(1)
    ne = pl.num_programs(1)

    @pl.when(e == 0)
    def _():
        tf = _rms(h_ref[...], gffn_ref[...])
        t_scr[...] = tf.astype(BF16)
        t_h, t_m, t_l = _split3(tf)
        w_h = wr_ref[0]
        w_m = wr_ref[1]
        logits = (_dot(t_h, w_h) + _dot(t_m, w_h) + _dot(t_h, w_m) + _dot(t_l, w_h) + _dot(t_m, w_m)) + br_ref[...]
        lg = logits[:, 0:N_EXPERT_GROUPS]
        le = logits[:, N_EXPERTS:2 * N_EXPERTS]
        eg = jnp.exp(lg - jnp.max(lg, axis=-1, keepdims=True))
        pg = eg / jnp.sum(eg, axis=-1, keepdims=True)
        g_val, g_idx = _first_argmax(pg, N_EXPERT_GROUPS)
        lane16 = lax.broadcasted_iota(jnp.int32, le.shape, 1)
        in_grp = (lane16 // EXPERTS_PER_GROUP) == g_idx
        neg = jnp.float32(-jnp.inf)
        le_m = jnp.where(in_grp, le, neg)
        ee = jnp.where(in_grp, jnp.exp(le - jnp.max(le_m, axis=-1, keepdims=True)), 0.0)
        pe = ee / jnp.sum(ee, axis=-1, keepdims=True)
        pe_m = jnp.where(in_grp, pe, -1.0)
        v1, i1 = _first_argmax(pe_m, N_EXPERTS)
        pe_m2 = jnp.where(lane16 == i1, -1.0, pe_m)
        v2, i2 = _first_argmax(pe_m2, N_EXPERTS)
        den = v1 + v2
        comb = jnp.where(lane16 == i1, g_val * v1 / den, 0.0) + jnp.where(lane16 == i2, g_val * v2 / den, 0.0)
        comb_scr[...] = comb
        acc_scr[...] = jnp.zeros(acc_scr.shape, F32)

    t = t_scr[...]
    hdn = _silu(_dot(t, wg_ref[0].astype(BF16))) * _dot(t, wu_ref[0].astype(BF16))
    out_e = _dot(hdn.astype(BF16), wd_ref[0].astype(BF16))
    lane16 = lax.broadcasted_iota(jnp.int32, comb_scr.shape, 1)
    c_e = jnp.sum(jnp.where(lane16 == e, comb_scr[...], 0.0), axis=-1, keepdims=True)
    acc_scr[...] += c_e * out_e

    @pl.when(e == ne - 1)
    def _():
        h2 = h_ref[...] + acc_scr[...]
        gate = _sigmoid(_dot(_rms(h2, gple_ref[...]).astype(BF16), wpg_ref[...]))
        h3 = h2 + _dot(p_ref[...].astype(BF16), wp_ref[...]) * gate
        y_ref[...] = _rms(h3, gfin_ref[...])


def _ffn(h, p, w, tb):
    nt = h.shape[0]
    return pl.pallas_call(
        _ffn_kernel,
        grid=(nt // tb, N_EXPERTS),
        in_specs=[
            pl.BlockSpec((tb, D_MODEL), lambda i, e: (i, 0)),
            pl.BlockSpec((tb, PLE_DIM), lambda i, e: (i, 0)),
            _const_spec((1, D_MODEL)),
            _const_spec((2, D_MODEL, 2 * N_EXPERTS)),
            _const_spec((1, 2 * N_EXPERTS)),
            pl.BlockSpec((1, D_MODEL, D_FF), lambda i, e: (e, 0, 0)),
            pl.BlockSpec((1, D_MODEL, D_FF), lambda i, e: (e, 0, 0)),
            pl.BlockSpec((1, D_FF, D_MODEL), lambda i, e: (e, 0, 0)),
            _const_spec((1, D_MODEL)),
            _const_spec((D_MODEL, D_MODEL)),
            _const_spec((PLE_DIM, D_MODEL)),
            _const_spec((1, D_MODEL)),
        ],
        out_specs=pl.BlockSpec((tb, D_MODEL), lambda i, e: (i, 0)),
        out_shape=jax.ShapeDtypeStruct((nt, D_MODEL), F32),
        scratch_shapes=[
            pltpu.VMEM((tb, D_MODEL), BF16),
            pltpu.VMEM((tb, N_EXPERTS), F32),
            pltpu.VMEM((tb, D_MODEL), F32),
        ],
        compiler_params=pltpu.CompilerParams(
            dimension_semantics=("arbitrary", "arbitrary"), vmem_limit_bytes=VMEM_LIMIT),
        name="ffn",
    )(h, p, w['g_ffn'], w['w_router'], w['b_router'], w['w_gate_e'], w['w_up_e'], w['w_down_e'],
      w['g_ple'], w['w_ple_gate'], w['w_ple'], w['g_final'])


ROUTE_TB = 1024
EXPERT_TM = 512
SC_WINDOW = 128
SC_CORES = 2
SC_SUBCORES = 16
SC_WORKERS = SC_CORES * SC_SUBCORES
HALF = D_MODEL // 2


def _pack_bf16_pairs(v):
    bits = pltpu.bitcast(v.astype(BF16).astype(F32), jnp.uint32)
    packed = bits[:, HALF:] | (bits[:, :HALF] >> 16)
    return pltpu.bitcast(packed, jnp.int32)


def _unpack_bf16_pairs(p):
    u = pltpu.bitcast(p, jnp.uint32)
    lo = pltpu.bitcast(u << 16, F32)
    hi = pltpu.bitcast(u & jnp.uint32(0xFFFF0000), F32)
    return jnp.concatenate([lo, hi], axis=-1)


def _route_kernel(h_ref, gffn_ref, wrT_ref, brT_ref, tp_ref, meta_ref, wts_ref, cnt_ref, upper_scr, carry_scr):
    i = pl.program_id(0)
    tb = h_ref.shape[0]

    @pl.when(i == 0)
    def _():
        r = lax.broadcasted_iota(jnp.int32, (tb, tb), 0)
        c = lax.broadcasted_iota(jnp.int32, (tb, tb), 1)
        upper_scr[...] = (r < c).astype(BF16)
        carry_scr[...] = jnp.zeros(carry_scr.shape, F32)

    tf = _rms(h_ref[...], gffn_ref[...])
    tp_ref[...] = _pack_bf16_pairs(tf)
    t_h = tf.astype(BF16)
    t_m = (tf - t_h.astype(F32)).astype(BF16)
    w_h = wrT_ref[0]
    w_m = wrT_ref[1]
    logits = (_dot_nt(w_h, t_h) + _dot_nt(w_h, t_m) + _dot_nt(w_m, t_h)) + brT_ref[...]
    neg = -jnp.inf
    row8 = lax.broadcasted_iota(jnp.int32, (SUBLANES, tb), 0)
    row16 = lax.broadcasted_iota(jnp.int32, (N_EXPERTS, tb), 0)
    lg = jnp.where(row8 < N_EXPERT_GROUPS, logits[0:SUBLANES], neg)
    le = logits[N_EXPERTS:2 * N_EXPERTS]
    eg = jnp.exp(lg - jnp.max(lg, axis=0, keepdims=True))
    pg = eg / jnp.sum(eg, axis=0, keepdims=True)
    g_val = jnp.max(pg, axis=0, keepdims=True)
    g_idx = jnp.min(jnp.where(pg == g_val, row8, SUBLANES), axis=0, keepdims=True)
    in_grp = (row16 // EXPERTS_PER_GROUP) == g_idx
    le_m = jnp.where(in_grp, le, neg)
    ee = jnp.where(in_grp, jnp.exp(le - jnp.max(le_m, axis=0, keepdims=True)), 0.0)
    pe = ee / jnp.sum(ee, axis=0, keepdims=True)
    pe_m = jnp.where(in_grp, pe, -1.0)
    v1 = jnp.max(pe_m, axis=0, keepdims=True)
    i1 = jnp.min(jnp.where(pe_m == v1, row16, N_EXPERTS), axis=0, keepdims=True)
    pe_m2 = jnp.where(row16 == i1, -1.0, pe_m)
    v2 = jnp.max(pe_m2, axis=0, keepdims=True)
    i2 = jnp.min(jnp.where(pe_m2 == v2, row16, N_EXPERTS), axis=0, keepdims=True)
    den = v1 + v2
    w0 = g_val * v1 / den
    w1 = g_val * v2 / den

    sel0 = row16 == i1
    sel1 = row16 == i2
    hot = jnp.where(sel0 | sel1, 1.0, 0.0)
    rank = _dot(hot.astype(BF16), upper_scr[...]) + carry_scr[...]
    r0 = jnp.sum(jnp.where(sel0, rank, 0.0), axis=0, keepdims=True)
    r1 = jnp.sum(jnp.where(sel1, rank, 0.0), axis=0, keepdims=True)
    carry_scr[...] += jnp.sum(hot, axis=1, keepdims=True)
    cnt_ref[...] = carry_scr[...]

    meta = jnp.where(row8 == 0, i1, jnp.where(row8 == 1, i2, jnp.where(
        row8 == 2, r0.astype(jnp.int32), jnp.where(row8 == 3, r1.astype(jnp.int32), 0))))
    meta_ref[...] = meta
    row128 = lax.broadcasted_iota(jnp.int32, (LANES, tb), 0)
    wts_ref[...] = jnp.where(row128 == 0, w0, jnp.where(row128 == 1, w1, 0.0)).T


def _route(h, w, tb=ROUTE_TB):
    nt = h.shape[0]
    return pl.pallas_call(
        _route_kernel,
        grid=(nt // tb,),
        in_specs=[
            pl.BlockSpec((tb, D_MODEL), lambda i: (i, 0)),
            _const_spec((1, D_MODEL)),
            _const_spec((2, 2 * N_EXPERTS, D_MODEL)),
            _const_spec((2 * N_EXPERTS, 1)),
        ],
        out_specs=[
            pl.BlockSpec((tb, HALF), lambda i: (i, 0)),
            pl.BlockSpec((SUBLANES, tb), lambda i: (0, i)),
            pl.BlockSpec((tb, LANES), lambda i: (i, 0)),
            pl.BlockSpec((N_EXPERTS, 1), lambda i: (0, 0)),
        ],
        out_shape=[
            jax.ShapeDtypeStruct((nt, HALF), jnp.int32),
            jax.ShapeDtypeStruct((SUBLANES, nt), jnp.int32),
            jax.ShapeDtypeStruct((nt, LANES), F32),
            jax.ShapeDtypeStruct((N_EXPERTS, 1), F32),
        ],
        scratch_shapes=[pltpu.VMEM((tb, tb), BF16), pltpu.VMEM((N_EXPERTS, 1), F32)],
        compiler_params=pltpu.CompilerParams(dimension_semantics=("arbitrary",), vmem_limit_bytes=VMEM_LIMIT),
        name="moe_route",
    )(h, w['g_ffn'], w['w_routerT'], w['b_routerT'])


def _sc_mesh():
    return plsc.VectorSubcoreMesh(core_axis_name="c", subcore_axis_name="s")


def _sc_dispatch(tp, slot0, slot1, n_slots):
    nt = tp.shape[0]

    per_worker = nt // (SC_WINDOW * SC_WORKERS)

    @pl.kernel(out_type=jax.ShapeDtypeStruct((n_slots, HALF), tp.dtype), mesh=_sc_mesh(), name="moe_dispatch",
               scratch_types=[pltpu.VMEM((1, SC_WINDOW), jnp.int32), pltpu.VMEM((1, SC_WINDOW), jnp.int32),
                              pltpu.VMEM((SC_WINDOW, HALF), tp.dtype)])
    def run(x_hbm, i0_hbm, i1_hbm, o_hbm, i0_v, i1_v, buf_v):
        worker = lax.axis_index("c") * SC_SUBCORES + lax.axis_index("s")

        @pl.loop(0, per_worker)
        def _(k):
            base = (worker * per_worker + k) * SC_WINDOW
            pltpu.sync_copy(i0_hbm.at[:, pl.ds(base, SC_WINDOW)], i0_v)
            pltpu.sync_copy(i1_hbm.at[:, pl.ds(base, SC_WINDOW)], i1_v)
            pltpu.sync_copy(x_hbm.at[pl.ds(base, SC_WINDOW)], buf_v)
            pltpu.sync_copy(buf_v, o_hbm.at[i0_v.at[0]])
            pltpu.sync_copy(buf_v, o_hbm.at[i1_v.at[0]])

    return run(tp, slot0, slot1)


def _sc_combine(ys, slot0, slot1):
    nt = slot0.shape[1]
    out = jax.ShapeDtypeStruct((nt, HALF), ys.dtype)

    per_worker = nt // (SC_WINDOW * SC_WORKERS)

    @pl.kernel(out_type=(out, out), mesh=_sc_mesh(), name="moe_combine",
               scratch_types=[pltpu.VMEM((1, SC_WINDOW), jnp.int32), pltpu.VMEM((1, SC_WINDOW), jnp.int32),
                              pltpu.VMEM((SC_WINDOW, HALF), ys.dtype)])
    def run(y_hbm, i0_hbm, i1_hbm, o0_hbm, o1_hbm, i0_v, i1_v, buf_v):
        worker = lax.axis_index("c") * SC_SUBCORES + lax.axis_index("s")

        @pl.loop(0, per_worker)
        def _(k):
            base = (worker * per_worker + k) * SC_WINDOW
            pltpu.sync_copy(i0_hbm.at[:, pl.ds(base, SC_WINDOW)], i0_v)
            pltpu.sync_copy(i1_hbm.at[:, pl.ds(base, SC_WINDOW)], i1_v)
            pltpu.sync_copy(y_hbm.at[i0_v.at[0]], buf_v)
            pltpu.sync_copy(buf_v, o0_hbm.at[pl.ds(base, SC_WINDOW)])
            pltpu.sync_copy(y_hbm.at[i1_v.at[0]], buf_v)
            pltpu.sync_copy(buf_v, o1_hbm.at[pl.ds(base, SC_WINDOW)])

    return run(ys, slot0, slot1)


def _experts_kernel(te_ref, nv_ref, x_ref, wg_ref, wu_ref, wd_ref, y_ref, wg_s, wu_s, wd_s):
    j = pl.program_id(0)

    @pl.when((j == 0) | (te_ref[j] != te_ref[jnp.maximum(j - 1, 0)]))
    def _():
        wg_s[...] = wg_ref[0].astype(BF16)
        wu_s[...] = wu_ref[0].astype(BF16)
        wd_s[...] = wd_ref[0].astype(BF16)

    @pl.when(j < nv_ref[0])
    def _():
        x = _unpack_bf16_pairs(x_ref[...]).astype(BF16)
        hdn = _silu(_dot(x, wg_s[...])) * _dot(x, wu_s[...])
        y_ref[...] = _pack_bf16_pairs(_dot(hdn.astype(BF16), wd_s[...]))


def _experts(xs, tile_expert, n_valid, w, tm=EXPERT_TM):
    n_slots = xs.shape[0]
    grid_spec = pltpu.PrefetchScalarGridSpec(
        num_scalar_prefetch=2,
        grid=(n_slots // tm,),
        in_specs=[
            pl.BlockSpec((tm, HALF), lambda j, te, nv: (j, 0)),
            pl.BlockSpec((1, D_MODEL, D_FF), lambda j, te, nv: (te[j], 0, 0)),
            pl.BlockSpec((1, D_MODEL, D_FF), lambda j, te, nv: (te[j], 0, 0)),
            pl.BlockSpec((1, D_FF, D_MODEL), lambda j, te, nv: (te[j], 0, 0)),
        ],
        out_specs=pl.BlockSpec((tm, HALF), lambda j, te, nv: (j, 0)),
        scratch_shapes=[pltpu.VMEM((D_MODEL, D_FF), BF16), pltpu.VMEM((D_MODEL, D_FF), BF16),
                        pltpu.VMEM((D_FF, D_MODEL), BF16)],
    )
    return pl.pallas_call(
        _experts_kernel,
        grid_spec=grid_spec,
        out_shape=jax.ShapeDtypeStruct((n_slots, HALF), jnp.int32),
        compiler_params=pltpu.CompilerParams(dimension_semantics=("arbitrary",), vmem_limit_bytes=VMEM_LIMIT),
        name="moe_experts",
    )(tile_expert, n_valid, xs, w['w_gate_e'], w['w_up_e'], w['w_down_e'])


def _finish_kernel(h_ref, y0_ref, y1_ref, wts_ref, p_ref, gple_ref, wpg_ref, wp_ref, gfin_ref, o_ref):
    wts = wts_ref[...]
    moe = wts[:, 0:1] * _unpack_bf16_pairs(y0_ref[...]) + wts[:, 1:2] * _unpack_bf16_pairs(y1_ref[...])
    h2 = h_ref[...] + moe
    gate = _sigmoid(_dot(_rms(h2, gple_ref[...]).astype(BF16), wpg_ref[...]))
    h3 = h2 + _dot(p_ref[...].astype(BF16), wp_ref[...]) * gate
    o_ref[...] = _rms(h3, gfin_ref[...])


def _finish(h, y0, y1, wts, p, w, tb=1024):
    nt = h.shape[0]
    return pl.pallas_call(
        _finish_kernel,
        grid=(nt // tb,),
        in_specs=[
            pl.BlockSpec((tb, D_MODEL), lambda i: (i, 0)),
            pl.BlockSpec((tb, HALF), lambda i: (i, 0)),
            pl.BlockSpec((tb, HALF), lambda i: (i, 0)),
            pl.BlockSpec((tb, LANES), lambda i: (i, 0)),
            pl.BlockSpec((tb, PLE_DIM), lambda i: (i, 0)),
            _const_spec((1, D_MODEL)),
            _const_spec((D_MODEL, D_MODEL)),
            _const_spec((PLE_DIM, D_MODEL)),
            _const_spec((1, D_MODEL)),
        ],
        out_specs=pl.BlockSpec((tb, D_MODEL), lambda i: (i, 0)),
        out_shape=jax.ShapeDtypeStruct((nt, D_MODEL), F32),
        compiler_params=pltpu.CompilerParams(dimension_semantics=("arbitrary",), vmem_limit_bytes=VMEM_LIMIT),
        name="moe_finish",
    )(h, y0, y1, wts, p, w['g_ple'], w['w_ple_gate'], w['w_ple'], w['g_final'])


def _ffn_sparse(h, p, w):
    nt = h.shape[0]
    tm = EXPERT_TM
    n_tiles = (nt * 2) // tm + N_EXPERTS
    tp, meta, wts, counts = _route(h, w)
    cnt = counts.reshape(N_EXPERTS).astype(jnp.int32)
    tiles_e = (cnt + tm - 1) // tm
    tile_end = jnp.cumsum(tiles_e)
    off = (tile_end - tiles_e) * tm
    n_valid = tile_end[-1:]
    tile_ids = jnp.arange(n_tiles, dtype=jnp.int32)
    tile_expert = jnp.sum((tile_ids[:, None] >= tile_end[None, :]).astype(jnp.int32), axis=1)
    last_expert = jnp.sum((n_valid - 1 >= tile_end).astype(jnp.int32))
    tile_expert = jnp.minimum(tile_expert, last_expert).astype(jnp.int32)
    eid = meta[0:2]
    slot = meta[2:4] + jnp.sum(jnp.where(eid[..., None] == jnp.arange(N_EXPERTS), off, 0), axis=-1)
    slot0 = slot[0:1]
    slot1 = slot[1:2]
    xs = _sc_dispatch(tp, slot0, slot1, n_tiles * tm)
    ys = _experts(xs, tile_expert, n_valid.astype(jnp.int32), w)
    y0, y1 = _sc_combine(ys, slot0, slot1)
    return _finish(h, y0, y1, wts, p, w)


def _prep_weights(g_mix, w_in, conf_dw_w, conf_dw_b, conf_ln_g, conf_ln_b, ssm_conv_w, ssm_conv_b,
                  dt_bias, a_log, d_skip, ssm_norm_g, w_out, g_ffn, w_rg, b_rg, w_re, b_re,
                  w_gate_e, w_up_e, w_down_e, g_ple, w_ple_gate, w_ple, g_final):
    row = lambda v: v.reshape(1, -1)
    w_in_b = w_in.astype(BF16)
    conf_w_pad = jnp.concatenate([conf_dw_w, jnp.zeros((CONF_PAD - K_CONF, C_CONF), F32)], axis=0)
    w_router = jnp.concatenate(
        [w_rg, jnp.zeros((D_MODEL, N_EXPERTS - N_EXPERT_GROUPS), F32), w_re], axis=1)
    wr_hi = w_router.astype(BF16)
    wr_mid = (w_router - wr_hi.astype(F32)).astype(BF16)
    b_router = jnp.concatenate([b_rg, jnp.zeros((N_EXPERTS - N_EXPERT_GROUPS,), F32), b_re]).reshape(1, -1)
    head_of_lane = jnp.arange(D_INNER) // HEAD_DIM
    head_expand = (head_of_lane[None, :] == jnp.arange(N_HEADS)[:, None]).astype(BF16)
    return dict(
        g_mix=row(g_mix), w_in=w_in_b, w_dtT=w_in_b[:, O_DT:].T,
        conf_w=conf_w_pad.reshape(CONF_PAD, N_LANE_TILES, LANES).transpose(1, 0, 2),
        conf_b=conf_dw_b.reshape(N_LANE_TILES, 1, LANES),
        conf_w2=conf_dw_w, conf_b2=row(conf_dw_b),
        ssm_w3=ssm_conv_w.reshape(K_SSM, N_XBC_TILES, LANES).transpose(1, 0, 2),
        ssm_b3=ssm_conv_b.reshape(N_XBC_TILES, 1, LANES),
        ln_g=row(conf_ln_g), ln_b=row(conf_ln_b), ssm_w=ssm_conv_w, ssm_b=row(ssm_conv_b),
        dt_bias=row(dt_bias), dt_biasT=dt_bias.reshape(-1, 1), a_log=row(a_log), a_logT=a_log.reshape(-1, 1),
        d_skip=row(jnp.repeat(d_skip, HEAD_DIM)), ssm_norm_g=row(ssm_norm_g), w_out=w_out.astype(BF16),
        head_expand=head_expand,
        g_ffn=row(g_ffn), w_router=jnp.stack([wr_hi, wr_mid]), b_router=b_router,
        w_routerT=jnp.stack([wr_hi.T, wr_mid.T]), b_routerT=b_router.reshape(-1, 1),
        w_gate_e=w_gate_e, w_up_e=w_up_e, w_down_e=w_down_e,
        g_ple=row(g_ple), w_ple_gate=w_ple_gate.astype(BF16), w_ple=w_ple.astype(BF16), g_final=row(g_final),
    )


def kernel(x_prompt, x_sample, p_prompt, p_sample, state_conf_conv, state_ssm_conv, state_ssm, g_mix, w_in, conf_dw_w, conf_dw_b, conf_ln_g, conf_ln_b, ssm_conv_w, ssm_conv_b, dt_bias, a_log, d_skip, ssm_norm_g, w_out, g_ffn, w_rg, b_rg, w_re, b_re, w_gate_e, w_up_e, w_down_e, g_ple, w_ple_gate, w_ple, g_final):
    depth = g_mix.shape[0]
    bsz, seq, _ = x_prompt.shape
    nb = x_sample.shape[0]
    hp = x_prompt
    hs = x_sample.reshape(nb, D_MODEL)
    cp_l, mp_l, sp_l, cs_l, ms_l, ss_l = [], [], [], [], [], []
    for i in range(depth):
        w = _prep_weights(g_mix[i], w_in[i], conf_dw_w[i], conf_dw_b[i], conf_ln_g[i], conf_ln_b[i],
                          ssm_conv_w[i], ssm_conv_b[i], dt_bias[i], a_log[i], d_skip[i], ssm_norm_g[i], w_out[i],
                          g_ffn[i], w_rg[i], b_rg[i], w_re[i], b_re[i], w_gate_e[i], w_up_e[i], w_down_e[i],
                          g_ple[i], w_ple_gate[i], w_ple[i], g_final)
        h1p, c, m, s = _mixer_prompt(hp, w)
        cp_l.append(c)
        mp_l.append(m)
        sp_l.append(s)
        h1s, c, m, s = _mixer_sample(hs, jnp.transpose(state_conf_conv[i], (1, 0, 2)), state_ssm_conv[i],
                                     state_ssm[i], w)
        cs_l.append(jnp.transpose(c, (1, 0, 2)))
        ms_l.append(m)
        ss_l.append(s)
        assert depth == 1
        hp = _ffn_sparse(h1p.reshape(bsz * seq, D_MODEL), p_prompt[i].reshape(bsz * seq, PLE_DIM), w)
        hp = hp.reshape(bsz, seq, D_MODEL)
        hs = _ffn(h1s, p_sample[i].reshape(nb, PLE_DIM), w, tb=nb)
    return (hp, hs.reshape(nb, 1, D_MODEL), jnp.stack(cp_l), jnp.stack(mp_l), jnp.stack(sp_l),
            jnp.stack(cs_l), jnp.stack(ms_l), jnp.stack(ss_l))
```

```python
import functools

import jax
import jax.numpy as jnp
from jax import lax
from jax.experimental import pallas as pl
from jax.experimental.pallas import tpu as pltpu
from jax.experimental.pallas import tpu_sc as plsc

F32 = jnp.float32
BF16 = jnp.bfloat16

D_MODEL = 1024
C_CONF = 1024
K_CONF = 31
D_INNER = 1024
HEAD_DIM = 64
N_HEADS = 16
N_GROUPS = 4
HEADS_PER_GROUP = N_HEADS // N_GROUPS
GROUP_W = HEADS_PER_GROUP * HEAD_DIM
D_STATE = 128
K_SSM = 4
CHUNK = 128
CONV_DIM = D_INNER + 2 * N_GROUPS * D_STATE
IN_COLS = 2 * C_CONF + D_INNER + CONV_DIM + N_HEADS
O_GATE = C_CONF
O_Z = 2 * C_CONF
O_XBC = O_Z + D_INNER
O_DT = O_XBC + CONV_DIM
N_EXPERT_GROUPS = 4
EXPERTS_PER_GROUP = 4
N_EXPERTS = 16
D_FF = 512
PLE_DIM = 256
EPS = 1e-6

LANES = 128
SUBLANES = 8
N_LANE_TILES = C_CONF // LANES
N_XBC_TILES = CONV_DIM // LANES
CONF_PAD = 32
SSM_PAD = 8
VMEM_LIMIT = 56 * 1024 * 1024
W_STAGE_ROWS = 128


def _dot(a, b):
    return jnp.dot(a, b, preferred_element_type=F32)


def _dot_nt(a, b):
    return lax.dot_general(a, b, (((1,), (1,)), ((), ())), preferred_element_type=F32)


def _dot_tn(a, b):
    return lax.dot_general(a, b, (((0,), (0,)), ((), ())), preferred_element_type=F32)


def _split3(v):
    hi = v.astype(BF16)
    r = v - hi.astype(F32)
    mid = r.astype(BF16)
    lo = (r - mid.astype(F32)).astype(BF16)
    return hi, mid, lo


def _rms(x, g):
    return x * lax.rsqrt(jnp.mean(x * x, axis=-1, keepdims=True) + EPS) * g


def _sigmoid(x):
    return jax.nn.sigmoid(x)


def _silu(x):
    return x * jax.nn.sigmoid(x)


def _softplus(x):
    return jax.nn.softplus(x)


def _mixer_prompt_kernel(x_ref, gmix_ref, winf_hbm, wdtT_ref, cw_ref, cb_ref, lng_ref, lnb_ref,
                         sw_ref, sb_ref, dtb_ref, dtbT_ref, alog_ref, alogT_ref, dskip_ref, sng_ref,
                         wout_ref, hexp_ref,
                         h1_ref, ncc_ref, nsc_ref, nss_ref,
                         cscr, cout, mscr, xbc_scr, st_scr, win_ref, wstage, wsem, *, tl):
    t = pl.program_id(1)
    nt = pl.num_programs(1)

    @pl.when((pl.program_id(0) == 0) & (t == 0))
    def _():
        n_chunks = D_MODEL // W_STAGE_ROWS

        def fetch(c):
            return pltpu.make_async_copy(winf_hbm.at[pl.ds(c * W_STAGE_ROWS, W_STAGE_ROWS)], wstage.at[c % 2],
                                         wsem.at[c % 2])

        fetch(0).start()
        for c in range(n_chunks):
            if c + 1 < n_chunks:
                fetch(c + 1).start()
            fetch(c).wait()
            win_ref[c * W_STAGE_ROWS:(c + 1) * W_STAGE_ROWS, :] = wstage[c % 2].astype(BF16)

    @pl.when(t == 0)
    def _():
        cscr[:, 0:CONF_PAD, :] = jnp.zeros((N_LANE_TILES, CONF_PAD, LANES), F32)
        mscr[:, 0:SSM_PAD, :] = jnp.zeros((N_XBC_TILES, SSM_PAD, LANES), F32)
        st_scr[...] = jnp.zeros(st_scr.shape, F32)

    x = x_ref[0]
    u = _rms(x, gmix_ref[...]).astype(BF16)

    glu = _dot(u, win_ref[:, 0:O_GATE]) * _sigmoid(_dot(u, win_ref[:, O_GATE:O_Z]))
    for lc in range(N_LANE_TILES):
        cscr[lc, CONF_PAD:CONF_PAD + tl, :] = glu[:, lc * LANES:(lc + 1) * LANES]

    rc = 64

    def conv_lane_tile(lc, carry):
        bias = cb_ref[lc]
        for r0 in range(0, tl, rc):
            acc = jnp.broadcast_to(bias, (rc, LANES))
            for k in range(K_CONF):
                acc = acc + cw_ref[lc, pl.ds(k, 1), :] * cscr[lc, pl.ds(r0 + k + CONF_PAD - (K_CONF - 1), rc), :]
            cout[lc, pl.ds(r0, rc), :] = acc
        return carry

    for lc in range(N_LANE_TILES):
        conv_lane_tile(lc, 0)

    for lc in range(N_LANE_TILES):
        cscr[lc, 0:CONF_PAD, :] = cscr[lc, tl:tl + CONF_PAD, :]

    cc = [cout[lc] for lc in range(N_LANE_TILES)]
    tot = cc[0]
    for lc in range(1, N_LANE_TILES):
        tot = tot + cc[lc]
    mean = jnp.sum(tot, axis=-1, keepdims=True) * (1.0 / C_CONF)
    xc = [c - mean for c in cc]
    sq = xc[0] * xc[0]
    for lc in range(1, N_LANE_TILES):
        sq = sq + xc[lc] * xc[lc]
    rstd = lax.rsqrt(jnp.sum(sq, axis=-1, keepdims=True) * (1.0 / C_CONF) + EPS)
    a_out = jnp.concatenate(
        [_silu(xc[lc] * rstd * lng_ref[:, lc * LANES:(lc + 1) * LANES] + lnb_ref[:, lc * LANES:(lc + 1) * LANES])
         for lc in range(N_LANE_TILES)], axis=-1).astype(BF16)

    z = _dot(u, win_ref[:, O_Z:O_XBC])
    xbc_raw = _dot(u, win_ref[:, O_XBC:O_DT])
    for lt in range(N_XBC_TILES):
        mscr[lt, SSM_PAD:SSM_PAD + tl, :] = xbc_raw[:, lt * LANES:(lt + 1) * LANES]

    def ssm_conv_lane_tile(lt, carry):
        bias = sb_ref[lt]
        for r0 in range(0, tl, rc):
            acc = jnp.broadcast_to(bias, (rc, LANES))
            for k in range(K_SSM):
                acc = acc + sw_ref[lt, pl.ds(k, 1), :] * mscr[lt, pl.ds(r0 + k + SSM_PAD - (K_SSM - 1), rc), :]
            xbc_scr[lt, pl.ds(r0, rc), :] = _silu(acc)
        mscr[lt, 0:SSM_PAD, :] = mscr[lt, tl:tl + SSM_PAD, :]
        return carry

    for lt in range(N_XBC_TILES):
        ssm_conv_lane_tile(lt, 0)

    n_x = D_INNER // LANES
    n_b = N_GROUPS * D_STATE // LANES
    xs = jnp.concatenate([xbc_scr[lt] for lt in range(n_x)], axis=-1)
    bm = jnp.concatenate([xbc_scr[lt] for lt in range(n_x, n_x + n_b)], axis=-1)
    cm = jnp.concatenate([xbc_scr[lt] for lt in range(n_x + n_b, N_XBC_TILES)], axis=-1)

    dt = _softplus(_dot(u, win_ref[:, O_DT:IN_COLS]) + dtb_ref[...])
    dtT = _softplus(_dot_nt(wdtT_ref[...], u) + dtbT_ref[...])
    a = dt * (-jnp.exp(alog_ref[...]))
    aT = dtT * (-jnp.exp(alogT_ref[...]))
    hexp = hexp_ref[...]
    d_h, d_m, d_l = _split3(dt)
    xdt_all = xs * (_dot(d_h, hexp) + _dot(d_m, hexp) + _dot(d_l, hexp))

    row = lax.broadcasted_iota(jnp.int32, (CHUNK, CHUNK), 0)
    col = lax.broadcasted_iota(jnp.int32, (CHUNK, CHUNK), 1)
    lower = row >= col
    tri = lower.astype(BF16)
    triT = (row <= col).astype(BF16)

    y_chunks = []
    for c in range(tl // CHUNK):
        r0 = c * CHUNK
        a_c = a[r0:r0 + CHUNK]
        aT_c = aT[:, r0:r0 + CHUNK]
        ah, am, al = _split3(a_c)
        cs = _dot(tri, ah) + _dot(tri, am) + _dot(tri, al)
        th, tm, tlo = _split3(aT_c)
        csT = _dot(th, triT) + _dot(tm, triT) + _dot(tlo, triT)
        cs_last = cs[CHUNK - 1:CHUNK, :]
        cdec = jnp.exp(cs_last)
        xdt_c = xdt_all[r0:r0 + CHUNK]
        y_heads = []
        for g in range(N_GROUPS):
            cg = cm[r0:r0 + CHUNK, g * D_STATE:(g + 1) * D_STATE].astype(BF16)
            bg = bm[r0:r0 + CHUNK, g * D_STATE:(g + 1) * D_STATE].astype(BF16)
            cb = _dot_nt(cg, bg)
            y_off = _dot(cg, st_scr[g].astype(BF16))
            xdd = []
            dec_row = []
            m_parts_g = []
            x_bd = []
            e_parts = []
            xdt_g = xdt_c[:, g * GROUP_W:(g + 1) * GROUP_W]
            lane_head = lax.broadcasted_iota(jnp.int32, (CHUNK, GROUP_W), 1) // HEAD_DIM
            for hh in range(HEADS_PER_GROUP):
                h = g * HEADS_PER_GROUP + hh
                xdt = xdt_g[:, hh * HEAD_DIM:(hh + 1) * HEAD_DIM]
                cs_b = jnp.broadcast_to(cs[:, h:h + 1], (CHUNK, CHUNK))
                lmat = jnp.where(lower, jnp.exp(cs_b - csT[h:h + 1, :]), 0.0)
                m_parts_g.append((cb * lmat).astype(BF16))
                x_bd.append(jnp.where(lane_head == hh, xdt_g, 0.0).astype(BF16))
                cs_bh = cs_b[:, 0:HEAD_DIM]
                e_parts.append(jnp.exp(cs_bh))
                xdd.append((xdt * jnp.exp(csT[h:h + 1, CHUNK - 1:CHUNK] - cs_bh)).astype(BF16))
                dec_row.append(jnp.broadcast_to(cdec[:, h:h + 1], (1, HEAD_DIM)))
            y_diag = _dot(jnp.concatenate(m_parts_g, axis=1), jnp.concatenate(x_bd, axis=0))
            y_heads.append(y_diag + y_off * jnp.concatenate(e_parts, axis=-1))
            contrib = _dot_tn(bg, jnp.concatenate(xdd, axis=-1))
            st_scr[g] = st_scr[g] * jnp.concatenate(dec_row, axis=-1) + contrib
        y_chunks.append(jnp.concatenate(y_heads, axis=-1))
    y = y_chunks[0] if len(y_chunks) == 1 else jnp.concatenate(y_chunks, axis=0)
    y = (y + dskip_ref[...] * xs) * _silu(z)
    m_parts = []
    for g in range(N_GROUPS):
        yg = y[:, g * GROUP_W:(g + 1) * GROUP_W]
        m_parts.append(_rms(yg, sng_ref[:, g * GROUP_W:(g + 1) * GROUP_W]))
    m_out = jnp.concatenate(m_parts, axis=-1).astype(BF16)

    h1_ref[0] = x + _dot(a_out, wout_ref[0:C_CONF, :]) + _dot(m_out, wout_ref[C_CONF:, :])

    @pl.when(t == nt - 1)
    def _():
        for lc in range(N_LANE_TILES):
            ncc_ref[0, :, lc * LANES:(lc + 1) * LANES] = cscr[lc, pl.ds(CONF_PAD - (K_CONF - 1), K_CONF - 1), :]
        for lt in range(N_XBC_TILES):
            nsc_ref[0, :, lt * LANES:(lt + 1) * LANES] = mscr[lt, pl.ds(SSM_PAD - (K_SSM - 1), K_SSM - 1), :]
        for g in range(N_GROUPS):
            nss_ref[0, g * HEADS_PER_GROUP:(g + 1) * HEADS_PER_GROUP] = (
                st_scr[g].T.reshape(HEADS_PER_GROUP, HEAD_DIM, D_STATE))


def _const_spec(shape):
    nd = len(shape)
    return pl.BlockSpec(shape, lambda *_: (0,) * nd, pipeline_mode=pl.Buffered(1))


def _mixer_prompt(x, w, tl=256):
    bsz, seq, _ = x.shape
    kern = functools.partial(_mixer_prompt_kernel, tl=tl)
    consts = [w['g_mix'], w['w_in_f32'], w['w_dtT'], w['conf_w'], w['conf_b'], w['ln_g'], w['ln_b'],
              w['ssm_w3'], w['ssm_b3'], w['dt_bias'], w['dt_biasT'], w['a_log'], w['a_logT'], w['d_skip'],
              w['ssm_norm_g'], w['w_out'], w['head_expand']]
    return pl.pallas_call(
        kern,
        grid=(bsz, seq // tl),
        in_specs=[pl.BlockSpec((1, tl, D_MODEL), lambda b, t: (b, t, 0))]
                 + [pl.BlockSpec(memory_space=pl.ANY) if c is w['w_in_f32'] else _const_spec(c.shape) for c in consts],
        out_specs=[
            pl.BlockSpec((1, tl, D_MODEL), lambda b, t: (b, t, 0)),
            pl.BlockSpec((1, K_CONF - 1, C_CONF), lambda b, t: (b, 0, 0)),
            pl.BlockSpec((1, K_SSM - 1, CONV_DIM), lambda b, t: (b, 0, 0)),
            pl.BlockSpec((1, N_HEADS, HEAD_DIM, D_STATE), lambda b, t: (b, 0, 0, 0)),
        ],
        out_shape=[
            jax.ShapeDtypeStruct((bsz, seq, D_MODEL), F32),
            jax.ShapeDtypeStruct((bsz, K_CONF - 1, C_CONF), F32),
            jax.ShapeDtypeStruct((bsz, K_SSM - 1, CONV_DIM), F32),
            jax.ShapeDtypeStruct((bsz, N_HEADS, HEAD_DIM, D_STATE), F32),
        ],
        scratch_shapes=[
            pltpu.VMEM((N_LANE_TILES, CONF_PAD + tl, LANES), F32),
            pltpu.VMEM((N_LANE_TILES, tl, LANES), F32),
            pltpu.VMEM((N_XBC_TILES, SSM_PAD + tl, LANES), F32),
            pltpu.VMEM((N_XBC_TILES, tl, LANES), F32),
            pltpu.VMEM((N_GROUPS, D_STATE, GROUP_W), F32),
            pltpu.VMEM((D_MODEL, IN_COLS), BF16),
            pltpu.VMEM((2, W_STAGE_ROWS, IN_COLS), F32),
            pltpu.SemaphoreType.DMA((2,)),
        ],
        compiler_params=pltpu.CompilerParams(
            dimension_semantics=("arbitrary", "arbitrary"), vmem_limit_bytes=VMEM_LIMIT),
        name="mixer_prompt",
    )(x, *consts)


def _mixer_sample_kernel(x_ref, cst_ref, mst_ref, sst_ref, gmix_ref, win_ref, cw_ref, cb_ref, lng_ref, lnb_ref,
                         sw_ref, sb_ref, dtb_ref, alog_ref, dskip_ref, sng_ref, wout_ref, hexp_ref,
                         h1_ref, ncc_ref, nsc_ref, nss_ref,
                         proj_scr, mix_scr, *, bb):
    i = pl.program_id(0)
    n = pl.num_programs(0)

    @pl.when(i == 0)
    def _():
        u = _rms(x_ref[...], gmix_ref[...]).astype(BF16)
        proj_scr[...] = _dot(u, win_ref[...])

    r0 = pl.multiple_of(i * bb, bb)
    proj = proj_scr[pl.ds(r0, bb), :]

    glu = proj[:, 0:O_GATE] * _sigmoid(proj[:, O_GATE:O_Z])
    acc = cb_ref[...] + cw_ref[pl.ds(K_CONF - 1, 1), :] * glu
    for k in range(K_CONF - 1):
        row_k = cst_ref[k]
        acc = acc + cw_ref[pl.ds(k, 1), :] * row_k
        if k >= 1:
            ncc_ref[k - 1] = row_k
    ncc_ref[K_CONF - 2] = glu
    mean = jnp.mean(acc, axis=-1, keepdims=True)
    xc = acc - mean
    rstd = lax.rsqrt(jnp.mean(xc * xc, axis=-1, keepdims=True) + EPS)
    a_out = _silu(xc * rstd * lng_ref[...] + lnb_ref[...])

    z = proj[:, O_Z:O_XBC]
    xbc_raw = proj[:, O_XBC:O_DT]
    acc = sb_ref[...] + sw_ref[pl.ds(K_SSM - 1, 1), :] * xbc_raw
    for k in range(K_SSM - 1):
        row_k = mst_ref[:, k, :]
        acc = acc + sw_ref[pl.ds(k, 1), :] * row_k
        if k >= 1:
            nsc_ref[:, k - 1, :] = row_k
    nsc_ref[:, K_SSM - 2, :] = xbc_raw
    xbc = _silu(acc)
    xs = xbc[:, 0:D_INNER]
    bm = xbc[:, D_INNER:D_INNER + N_GROUPS * D_STATE]
    cm = xbc[:, D_INNER + N_GROUPS * D_STATE:]
    dt = _softplus(proj[:, O_DT:IN_COLS] + dtb_ref[...])
    dec = jnp.exp(dt * (-jnp.exp(alog_ref[...])))
    hexp = hexp_ref[...]
    t_h, t_m, t_l = _split3(dt)
    dt_e = _dot(t_h, hexp) + _dot(t_m, hexp) + _dot(t_l, hexp)
    xdt = xs * dt_e

    lane = lax.broadcasted_iota(jnp.int32, (SUBLANES, D_INNER), 1)
    sub = lax.broadcasted_iota(jnp.int32, (SUBLANES, D_INNER), 0)
    gmask = (lane // GROUP_W) == sub
    eye_h = (lax.broadcasted_iota(jnp.int32, (N_HEADS, N_HEADS), 0)
             == lax.broadcasted_iota(jnp.int32, (N_HEADS, N_HEADS), 1))
    ones_hn = jnp.ones((3 * N_HEADS, D_STATE), BF16)

    y_rows = []
    for b in range(bb):
        xrow = jnp.where(gmask, jnp.broadcast_to(xdt[b:b + 1, :], (SUBLANES, D_INNER)), 0.0)
        x_h = xrow.astype(BF16).astype(F32)
        x_m = (xrow - x_h).astype(BF16).astype(F32)
        bmat = jnp.concatenate([bm[b:b + 1, g * D_STATE:(g + 1) * D_STATE] for g in range(N_GROUPS)]
                               + [jnp.zeros((SUBLANES - N_GROUPS, D_STATE), F32)], axis=0)
        b_h = bmat.astype(BF16).astype(F32)
        b_m = (bmat - b_h).astype(BF16).astype(F32)
        lhs = jnp.concatenate([x_h, x_h, x_m], axis=0).astype(BF16)
        rhs = jnp.concatenate([b_h, b_m, b_h], axis=0).astype(BF16)
        upd = _dot_tn(lhs, rhs)
        dg = jnp.where(eye_h, jnp.broadcast_to(dec[b:b + 1, :], (N_HEADS, N_HEADS)), 0.0)
        g_h, g_m, g_l = [v.astype(F32) for v in _split3(dg)]
        drow = _dot(jnp.concatenate([g_h, g_m, g_l], axis=1).astype(BF16), ones_hn)
        h_parts = []
        for h in range(N_HEADS):
            h_new = drow[h:h + 1, :] * sst_ref[b, h] + upd[h * HEAD_DIM:(h + 1) * HEAD_DIM, :]
            nss_ref[b, h] = h_new
            h_parts.append(h_new)
        hnew = jnp.concatenate(h_parts, axis=0)
        cmat = jnp.concatenate([cm[b:b + 1, g * D_STATE:(g + 1) * D_STATE] for g in range(N_GROUPS)]
                               + [jnp.zeros((SUBLANES - N_GROUPS, D_STATE), F32)], axis=0)
        yg = _dot_nt(cmat.astype(BF16), hnew.astype(BF16))
        y_rows.append(jnp.sum(jnp.where(gmask, yg, 0.0), axis=0, keepdims=True))
    y = jnp.concatenate(y_rows, axis=0)
    y = (y + dskip_ref[...] * xs) * _silu(z)
    m_parts = []
    for g in range(N_GROUPS):
        m_parts.append(_rms(y[:, g * GROUP_W:(g + 1) * GROUP_W], sng_ref[:, g * GROUP_W:(g + 1) * GROUP_W]))
    mix_scr[pl.ds(r0, bb), :] = jnp.concatenate([a_out] + m_parts, axis=-1)

    @pl.when(i == n - 1)
    def _():
        h1_ref[...] = x_ref[...] + _dot(mix_scr[...].astype(BF16), wout_ref[...])


def _mixer_sample(x, cst, mst, sst, w, bb=8):
    nb = x.shape[0]
    kern = functools.partial(_mixer_sample_kernel, bb=bb)
    consts = [w['g_mix'], w['w_in'], w['conf_w2'], w['conf_b2'], w['ln_g'], w['ln_b'],
              w['ssm_w'], w['ssm_b'], w['dt_bias'], w['a_log'], w['d_skip'], w['ssm_norm_g'], w['w_out'],
              w['head_expand']]
    return pl.pallas_call(
        kern,
        grid=(nb // bb,),
        in_specs=[
            _const_spec((nb, D_MODEL)),
            pl.BlockSpec((K_CONF - 1, bb, C_CONF), lambda i: (0, i, 0)),
            pl.BlockSpec((bb, K_SSM - 1, CONV_DIM), lambda i: (i, 0, 0)),
            pl.BlockSpec((bb, N_HEADS, HEAD_DIM, D_STATE), lambda i: (i, 0, 0, 0)),
        ] + [_const_spec(c.shape) for c in consts],
        out_specs=[
            pl.BlockSpec((nb, D_MODEL), lambda i: (0, 0)),
            pl.BlockSpec((K_CONF - 1, bb, C_CONF), lambda i: (0, i, 0)),
            pl.BlockSpec((bb, K_SSM - 1, CONV_DIM), lambda i: (i, 0, 0)),
            pl.BlockSpec((bb, N_HEADS, HEAD_DIM, D_STATE), lambda i: (i, 0, 0, 0)),
        ],
        out_shape=[
            jax.ShapeDtypeStruct((nb, D_MODEL), F32),
            jax.ShapeDtypeStruct((K_CONF - 1, nb, C_CONF), F32),
            jax.ShapeDtypeStruct((nb, K_SSM - 1, CONV_DIM), F32),
            jax.ShapeDtypeStruct((nb, N_HEADS, HEAD_DIM, D_STATE), F32),
        ],
        scratch_shapes=[
            pltpu.VMEM((nb, IN_COLS), F32),
            pltpu.VMEM((nb, D_MODEL + D_INNER), F32),
        ],
        compiler_params=pltpu.CompilerParams(dimension_semantics=("arbitrary",), vmem_limit_bytes=VMEM_LIMIT),
        name="mixer_sample",
    )(x, cst, mst, sst, *consts)


def _first_argmax(v, width):
    lane = lax.broadcasted_iota(jnp.int32, v.shape, 1)
    m = jnp.max(v, axis=-1, keepdims=True)
    idx = jnp.min(jnp.where(v == m, lane, width), axis=-1, keepdims=True)
    return m, idx


def _ffn_kernel(h_ref, p_ref, gffn_ref, wr_ref, br_ref, wg_ref, wu_ref, wd_ref, gple_ref, wpg_ref, wp_ref,
                gfin_ref, y_ref, t_scr, comb_scr, acc_scr):
    e = pl.program_id(1)
    ne = pl.num_programs(1)

    @pl.when(e == 0)
    def _():
        tf = _rms(h_ref[...], gffn_ref[...])
        t_scr[...] = tf.astype(BF16)
        t_h, t_m, t_l = _split3(tf)
        w_h = wr_ref[0]
        w_m = wr_ref[1]
        logits = (_dot(t_h, w_h) + _dot(t_m, w_h) + _dot(t_h, w_m) + _dot(t_l, w_h) + _dot(t_m, w_m)) + br_ref[...]
        lg = logits[:, 0:N_EXPERT_GROUPS]
        le = logits[:, N_EXPERTS:2 * N_EXPERTS]
        eg = jnp.exp(lg - jnp.max(lg, axis=-1, keepdims=True))
        pg = eg / jnp.sum(eg, axis=-1, keepdims=True)
        g_val, g_idx = _first_argmax(pg, N_EXPERT_GROUPS)
        lane16 = lax.broadcasted_iota(jnp.int32, le.shape, 1)
        in_grp = (lane16 // EXPERTS_PER_GROUP) == g_idx
        neg = jnp.float32(-jnp.inf)
        le_m = jnp.where(in_grp, le, neg)
        ee = jnp.where(in_grp, jnp.exp(le - jnp.max(le_m, axis=-1, keepdims=True)), 0.0)
        pe = ee / jnp.sum(ee, axis=-1, keepdims=True)
        pe_m = jnp.where(in_grp, pe, -1.0)
        v1, i1 = _first_argmax(pe_m, N_EXPERTS)
        pe_m2 = jnp.where(lane16 == i1, -1.0, pe_m)
        v2, i2 = _first_argmax(pe_m2, N_EXPERTS)
        den = v1 + v2
        comb = jnp.where(lane16 == i1, g_val * v1 / den, 0.0) + jnp.where(lane16 == i2, g_val * v2 / den, 0.0)
        comb_scr[...] = comb
        acc_scr[...] = jnp.zeros(acc_scr.shape, F32)

    t = t_scr[...]
    hdn = _silu(_dot(t, wg_ref[0].astype(BF16))) * _dot(t, wu_ref[0].astype(BF16))
    out_e = _dot(hdn.astype(BF16), wd_ref[0].astype(BF16))
    lane16 = lax.broadcasted_iota(jnp.int32, comb_scr.shape, 1)
    c_e = jnp.sum(jnp.where(lane16 == e, comb_scr[...], 0.0), axis=-1, keepdims=True)
    acc_scr[...] += c_e * out_e

    @pl.when(e == ne - 1)
    def _():
        h2 = h_ref[...] + acc_scr[...]
        gate = _sigmoid(_dot(_rms(h2, gple_ref[...]).astype(BF16), wpg_ref[...]))
        h3 = h2 + _dot(p_ref[...].astype(BF16), wp_ref[...]) * gate
        y_ref[...] = _rms(h3, gfin_ref[...])


def _ffn(h, p, w, tb):
    nt = h.shape[0]
    return pl.pallas_call(
        _ffn_kernel,
        grid=(nt // tb, N_EXPERTS),
        in_specs=[
            pl.BlockSpec((tb, D_MODEL), lambda i, e: (i, 0)),
            pl.BlockSpec((tb, PLE_DIM), lambda i, e: (i, 0)),
            _const_spec((1, D_MODEL)),
            _const_spec((2, D_MODEL, 2 * N_EXPERTS)),
            _const_spec((1, 2 * N_EXPERTS)),
            pl.BlockSpec((1, D_MODEL, D_FF), lambda i, e: (e, 0, 0)),
            pl.BlockSpec((1, D_MODEL, D_FF), lambda i, e: (e, 0, 0)),
            pl.BlockSpec((1, D_FF, D_MODEL), lambda i, e: (e, 0, 0)),
            _const_spec((1, D_MODEL)),
            _const_spec((D_MODEL, D_MODEL)),
            _const_spec((PLE_DIM, D_MODEL)),
            _const_spec((1, D_MODEL)),
        ],
        out_specs=pl.BlockSpec((tb, D_MODEL), lambda i, e: (i, 0)),
        out_shape=jax.ShapeDtypeStruct((nt, D_MODEL), F32),
        scratch_shapes=[
            pltpu.VMEM((tb, D_MODEL), BF16),
            pltpu.VMEM((tb, N_EXPERTS), F32),
            pltpu.VMEM((tb, D_MODEL), F32),
        ],
        compiler_params=pltpu.CompilerParams(
            dimension_semantics=("arbitrary", "arbitrary"), vmem_limit_bytes=VMEM_LIMIT),
        name="ffn",
    )(h, p, w['g_ffn'], w['w_router'], w['b_router'], w['w_gate_e'], w['w_up_e'], w['w_down_e'],
      w['g_ple'], w['w_ple_gate'], w['w_ple'], w['g_final'])


ROUTE_TB = 1024
EXPERT_TM = 512
SC_WINDOW = 128
SC_CORES = 2
SC_SUBCORES = 16
SC_WORKERS = SC_CORES * SC_SUBCORES
HALF = D_MODEL // 2


def _pack_bf16_pairs(v):
    bits = pltpu.bitcast(v.astype(BF16).astype(F32), jnp.uint32)
    packed = bits[:, HALF:] | (bits[:, :HALF] >> 16)
    return pltpu.bitcast(packed, jnp.int32)


def _unpack_bf16_pairs(p):
    u = pltpu.bitcast(p, jnp.uint32)
    lo = pltpu.bitcast(u << 16, F32)
    hi = pltpu.bitcast(u & jnp.uint32(0xFFFF0000), F32)
    return jnp.concatenate([lo, hi], axis=-1)


def _route_kernel(h_ref, gffn_ref, wrT_ref, brT_ref, tp_ref, meta_ref, wts_ref, cnt_ref, upper_scr, carry_scr):
    i = pl.program_id(0)
    tb = h_ref.shape[0]

    @pl.when(i == 0)
    def _():
        r = lax.broadcasted_iota(jnp.int32, (tb, tb), 0)
        c = lax.broadcasted_iota(jnp.int32, (tb, tb), 1)
        upper_scr[...] = (r < c).astype(BF16)
        carry_scr[...] = jnp.zeros(carry_scr.shape, F32)

    tf = _rms(h_ref[...], gffn_ref[...])
    tp_ref[...] = _pack_bf16_pairs(tf)
    t_h = tf.astype(BF16)
    t_m = (tf - t_h.astype(F32)).astype(BF16)
    w_h = wrT_ref[0]
    w_m = wrT_ref[1]
    logits = (_dot_nt(w_h, t_h) + _dot_nt(w_h, t_m) + _dot_nt(w_m, t_h)) + brT_ref[...]
    neg = -jnp.inf
    row8 = lax.broadcasted_iota(jnp.int32, (SUBLANES, tb), 0)
    row16 = lax.broadcasted_iota(jnp.int32, (N_EXPERTS, tb), 0)
    lg = jnp.where(row8 < N_EXPERT_GROUPS, logits[0:SUBLANES], neg)
    le = logits[N_EXPERTS:2 * N_EXPERTS]
    eg = jnp.exp(lg - jnp.max(lg, axis=0, keepdims=True))
    pg = eg / jnp.sum(eg, axis=0, keepdims=True)
    g_val = jnp.max(pg, axis=0, keepdims=True)
    g_idx = jnp.min(jnp.where(pg == g_val, row8, SUBLANES), axis=0, keepdims=True)
    in_grp = (row16 // EXPERTS_PER_GROUP) == g_idx
    le_m = jnp.where(in_grp, le, neg)
    ee = jnp.where(in_grp, jnp.exp(le - jnp.max(le_m, axis=0, keepdims=True)), 0.0)
    pe = ee / jnp.sum(ee, axis=0, keepdims=True)
    pe_m = jnp.where(in_grp, pe, -1.0)
    v1 = jnp.max(pe_m, axis=0, keepdims=True)
    i1 = jnp.min(jnp.where(pe_m == v1, row16, N_EXPERTS), axis=0, keepdims=True)
    pe_m2 = jnp.where(row16 == i1, -1.0, pe_m)
    v2 = jnp.max(pe_m2, axis=0, keepdims=True)
    i2 = jnp.min(jnp.where(pe_m2 == v2, row16, N_EXPERTS), axis=0, keepdims=True)
    den = v1 + v2
    w0 = g_val * v1 / den
    w1 = g_val * v2 / den

    sel0 = row16 == i1
    sel1 = row16 == i2
    hot = jnp.where(sel0 | sel1, 1.0, 0.0)
    rank = _dot(hot.astype(BF16), upper_scr[...]) + carry_scr[...]
    r0 = jnp.sum(jnp.where(sel0, rank, 0.0), axis=0, keepdims=True)
    r1 = jnp.sum(jnp.where(sel1, rank, 0.0), axis=0, keepdims=True)
    carry_scr[...] += jnp.sum(hot, axis=1, keepdims=True)
    cnt_ref[...] = carry_scr[...]

    meta = jnp.where(row8 == 0, i1, jnp.where(row8 == 1, i2, jnp.where(
        row8 == 2, r0.astype(jnp.int32), jnp.where(row8 == 3, r1.astype(jnp.int32), 0))))
    meta_ref[...] = meta
    row128 = lax.broadcasted_iota(jnp.int32, (LANES, tb), 0)
    wts_ref[...] = jnp.where(row128 == 0, w0, jnp.where(row128 == 1, w1, 0.0)).T


def _route(h, w, tb=ROUTE_TB):
    nt = h.shape[0]
    return pl.pallas_call(
        _route_kernel,
        grid=(nt // tb,),
        in_specs=[
            pl.BlockSpec((tb, D_MODEL), lambda i: (i, 0)),
            _const_spec((1, D_MODEL)),
            _const_spec((2, 2 * N_EXPERTS, D_MODEL)),
            _const_spec((2 * N_EXPERTS, 1)),
        ],
        out_specs=[
            pl.BlockSpec((tb, HALF), lambda i: (i, 0)),
            pl.BlockSpec((SUBLANES, tb), lambda i: (0, i)),
            pl.BlockSpec((tb, LANES), lambda i: (i, 0)),
            pl.BlockSpec((N_EXPERTS, 1), lambda i: (0, 0)),
        ],
        out_shape=[
            jax.ShapeDtypeStruct((nt, HALF), jnp.int32),
            jax.ShapeDtypeStruct((SUBLANES, nt), jnp.int32),
            jax.ShapeDtypeStruct((nt, LANES), F32),
            jax.ShapeDtypeStruct((N_EXPERTS, 1), F32),
        ],
        scratch_shapes=[pltpu.VMEM((tb, tb), BF16), pltpu.VMEM((N_EXPERTS, 1), F32)],
        compiler_params=pltpu.CompilerParams(dimension_semantics=("arbitrary",), vmem_limit_bytes=VMEM_LIMIT),
        name="moe_route",
    )(h, w['g_ffn'], w['w_routerT'], w['b_routerT'])


def _sc_mesh():
    return plsc.VectorSubcoreMesh(core_axis_name="c", subcore_axis_name="s")


def _sc_dispatch(tp, slot0, slot1, n_slots):
    nt = tp.shape[0]

    per_worker = nt // (SC_WINDOW * SC_WORKERS)

    @pl.kernel(out_type=jax.ShapeDtypeStruct((n_slots, HALF), tp.dtype), mesh=_sc_mesh(), name="moe_dispatch",
               scratch_types=[pltpu.VMEM((1, SC_WINDOW), jnp.int32), pltpu.VMEM((1, SC_WINDOW), jnp.int32),
                              pltpu.VMEM((SC_WINDOW, HALF), tp.dtype)])
    def run(x_hbm, i0_hbm, i1_hbm, o_hbm, i0_v, i1_v, buf_v):
        worker = lax.axis_index("c") * SC_SUBCORES + lax.axis_index("s")

        @pl.loop(0, per_worker)
        def _(k):
            base = (worker * per_worker + k) * SC_WINDOW
            pltpu.sync_copy(i0_hbm.at[:, pl.ds(base, SC_WINDOW)], i0_v)
            pltpu.sync_copy(i1_hbm.at[:, pl.ds(base, SC_WINDOW)], i1_v)
            pltpu.sync_copy(x_hbm.at[pl.ds(base, SC_WINDOW)], buf_v)
            pltpu.sync_copy(buf_v, o_hbm.at[i0_v.at[0]])
            pltpu.sync_copy(buf_v, o_hbm.at[i1_v.at[0]])

    return run(tp, slot0, slot1)


def _sc_combine(ys, slot0, slot1):
    nt = slot0.shape[1]
    out = jax.ShapeDtypeStruct((nt, HALF), ys.dtype)

    per_worker = nt // (SC_WINDOW * SC_WORKERS)

    @pl.kernel(out_type=(out, out), mesh=_sc_mesh(), name="moe_combine",
               scratch_types=[pltpu.VMEM((1, SC_WINDOW), jnp.int32), pltpu.VMEM((1, SC_WINDOW), jnp.int32),
                              pltpu.VMEM((SC_WINDOW, HALF), ys.dtype)])
    def run(y_hbm, i0_hbm, i1_hbm, o0_hbm, o1_hbm, i0_v, i1_v, buf_v):
        worker = lax.axis_index("c") * SC_SUBCORES + lax.axis_index("s")

        @pl.loop(0, per_worker)
        def _(k):
            base = (worker * per_worker + k) * SC_WINDOW
            pltpu.sync_copy(i0_hbm.at[:, pl.ds(base, SC_WINDOW)], i0_v)
            pltpu.sync_copy(i1_hbm.at[:, pl.ds(base, SC_WINDOW)], i1_v)
            pltpu.sync_copy(y_hbm.at[i0_v.at[0]], buf_v)
            pltpu.sync_copy(buf_v, o0_hbm.at[pl.ds(base, SC_WINDOW)])
            pltpu.sync_copy(y_hbm.at[i1_v.at[0]], buf_v)
            pltpu.sync_copy(buf_v, o1_hbm.at[pl.ds(base, SC_WINDOW)])

    return run(ys, slot0, slot1)


def _experts_kernel(te_ref, nv_ref, x_ref, wg_ref, wu_ref, wd_ref, y_ref, wg_s, wu_s, wd_s):
    j = pl.program_id(0)

    @pl.when((j == 0) | (te_ref[j] != te_ref[jnp.maximum(j - 1, 0)]))
    def _():
        wg_s[...] = wg_ref[0].astype(BF16)
        wu_s[...] = wu_ref[0].astype(BF16)
        wd_s[...] = wd_ref[0].astype(BF16)

    @pl.when(j < nv_ref[0])
    def _():
        x = _unpack_bf16_pairs(x_ref[...]).astype(BF16)
        hdn = _silu(_dot(x, wg_s[...])) * _dot(x, wu_s[...])
        y_ref[...] = _pack_bf16_pairs(_dot(hdn.astype(BF16), wd_s[...]))


def _experts(xs, tile_expert, n_valid, w, tm=EXPERT_TM):
    n_slots = xs.shape[0]
    grid_spec = pltpu.PrefetchScalarGridSpec(
        num_scalar_prefetch=2,
        grid=(n_slots // tm,),
        in_specs=[
            pl.BlockSpec((tm, HALF), lambda j, te, nv: (j, 0)),
            pl.BlockSpec((1, D_MODEL, D_FF), lambda j, te, nv: (te[j], 0, 0)),
            pl.BlockSpec((1, D_MODEL, D_FF), lambda j, te, nv: (te[j], 0, 0)),
            pl.BlockSpec((1, D_FF, D_MODEL), lambda j, te, nv: (te[j], 0, 0)),
        ],
        out_specs=pl.BlockSpec((tm, HALF), lambda j, te, nv: (j, 0)),
        scratch_shapes=[pltpu.VMEM((D_MODEL, D_FF), BF16), pltpu.VMEM((D_MODEL, D_FF), BF16),
                        pltpu.VMEM((D_FF, D_MODEL), BF16)],
    )
    return pl.pallas_call(
        _experts_kernel,
        grid_spec=grid_spec,
        out_shape=jax.ShapeDtypeStruct((n_slots, HALF), jnp.int32),
        compiler_params=pltpu.CompilerParams(dimension_semantics=("arbitrary",), vmem_limit_bytes=VMEM_LIMIT),
        name="moe_experts",
    )(tile_expert, n_valid, xs, w['w_gate_e'], w['w_up_e'], w['w_down_e'])


def _finish_kernel(h_ref, y0_ref, y1_ref, wts_ref, p_ref, gple_ref, wpg_ref, wp_ref, gfin_ref, o_ref):
    wts = wts_ref[...]
    moe = wts[:, 0:1] * _unpack_bf16_pairs(y0_ref[...]) + wts[:, 1:2] * _unpack_bf16_pairs(y1_ref[...])
    h2 = h_ref[...] + moe
    gate = _sigmoid(_dot(_rms(h2, gple_ref[...]).astype(BF16), wpg_ref[...]))
    h3 = h2 + _dot(p_ref[...].astype(BF16), wp_ref[...]) * gate
    o_ref[...] = _rms(h3, gfin_ref[...])


def _finish(h, y0, y1, wts, p, w, tb=1024):
    nt = h.shape[0]
    return pl.pallas_call(
        _finish_kernel,
        grid=(nt // tb,),
        in_specs=[
            pl.BlockSpec((tb, D_MODEL), lambda i: (i, 0)),
            pl.BlockSpec((tb, HALF), lambda i: (i, 0)),
            pl.BlockSpec((tb, HALF), lambda i: (i, 0)),
            pl.BlockSpec((tb, LANES), lambda i: (i, 0)),
            pl.BlockSpec((tb, PLE_DIM), lambda i: (i, 0)),
            _const_spec((1, D_MODEL)),
            _const_spec((D_MODEL, D_MODEL)),
            _const_spec((PLE_DIM, D_MODEL)),
            _const_spec((1, D_MODEL)),
        ],
        out_specs=pl.BlockSpec((tb, D_MODEL), lambda i: (i, 0)),
        out_shape=jax.ShapeDtypeStruct((nt, D_MODEL), F32),
        compiler_params=pltpu.CompilerParams(dimension_semantics=("arbitrary",), vmem_limit_bytes=VMEM_LIMIT),
        name="moe_finish",
    )(h, y0, y1, wts, p, w['g_ple'], w['w_ple_gate'], w['w_ple'], w['g_final'])


def _ffn_sparse(h, p, w):
    nt = h.shape[0]
    tm = EXPERT_TM
    n_tiles = (nt * 2) // tm + N_EXPERTS
    tp, meta, wts, counts = _route(h, w)
    cnt = counts.reshape(N_EXPERTS).astype(jnp.int32)
    tiles_e = (cnt + tm - 1) // tm
    tile_end = jnp.cumsum(tiles_e)
    off = (tile_end - tiles_e) * tm
    n_valid = tile_end[-1:]
    tile_ids = jnp.arange(n_tiles, dtype=jnp.int32)
    tile_expert = jnp.sum((tile_ids[:, None] >= tile_end[None, :]).astype(jnp.int32), axis=1)
    last_expert = jnp.sum((n_valid - 1 >= tile_end).astype(jnp.int32))
    tile_expert = jnp.minimum(tile_expert, last_expert).astype(jnp.int32)
    eid = meta[0:2]
    slot = meta[2:4] + jnp.sum(jnp.where(eid[..., None] == jnp.arange(N_EXPERTS), off, 0), axis=-1)
    slot0 = slot[0:1]
    slot1 = slot[1:2]
    xs = _sc_dispatch(tp, slot0, slot1, n_tiles * tm)
    ys = _experts(xs, tile_expert, n_valid.astype(jnp.int32), w)
    y0, y1 = _sc_combine(ys, slot0, slot1)
    return _finish(h, y0, y1, wts, p, w)


def _prep_weights(g_mix, w_in, conf_dw_w, conf_dw_b, conf_ln_g, conf_ln_b, ssm_conv_w, ssm_conv_b,
                  dt_bias, a_log, d_skip, ssm_norm_g, w_out, g_ffn, w_rg, b_rg, w_re, b_re,
                  w_gate_e, w_up_e, w_down_e, g_ple, w_ple_gate, w_ple, g_final):
    row = lambda v: v.reshape(1, -1)
    w_in_b = w_in.astype(BF16)
    conf_w_pad = jnp.concatenate([conf_dw_w, jnp.zeros((CONF_PAD - K_CONF, C_CONF), F32)], axis=0)
    w_router = jnp.concatenate(
        [w_rg, jnp.zeros((D_MODEL, N_EXPERTS - N_EXPERT_GROUPS), F32), w_re], axis=1)
    wr_hi = w_router.astype(BF16)
    wr_mid = (w_router - wr_hi.astype(F32)).astype(BF16)
    b_router = jnp.concatenate([b_rg, jnp.zeros((N_EXPERTS - N_EXPERT_GROUPS,), F32), b_re]).reshape(1, -1)
    head_of_lane = jnp.arange(D_INNER) // HEAD_DIM
    head_expand = (head_of_lane[None, :] == jnp.arange(N_HEADS)[:, None]).astype(BF16)
    return dict(
        g_mix=row(g_mix), w_in=w_in_b, w_in_f32=w_in, w_dtT=w_in_b[:, O_DT:].T,
        conf_w=conf_w_pad.reshape(CONF_PAD, N_LANE_TILES, LANES).transpose(1, 0, 2),
        conf_b=conf_dw_b.reshape(N_LANE_TILES, 1, LANES),
        conf_w2=conf_dw_w, conf_b2=row(conf_dw_b),
        ssm_w3=ssm_conv_w.reshape(K_SSM, N_XBC_TILES, LANES).transpose(1, 0, 2),
        ssm_b3=ssm_conv_b.reshape(N_XBC_TILES, 1, LANES),
        ln_g=row(conf_ln_g), ln_b=row(conf_ln_b), ssm_w=ssm_conv_w, ssm_b=row(ssm_conv_b),
        dt_bias=row(dt_bias), dt_biasT=dt_bias.reshape(-1, 1), a_log=row(a_log), a_logT=a_log.reshape(-1, 1),
        d_skip=row(jnp.repeat(d_skip, HEAD_DIM)), ssm_norm_g=row(ssm_norm_g), w_out=w_out.astype(BF16),
        head_expand=head_expand,
        g_ffn=row(g_ffn), w_router=jnp.stack([wr_hi, wr_mid]), b_router=b_router,
        w_routerT=jnp.stack([wr_hi.T, wr_mid.T]), b_routerT=b_router.reshape(-1, 1),
        w_gate_e=w_gate_e, w_up_e=w_up_e, w_down_e=w_down_e,
        g_ple=row(g_ple), w_ple_gate=w_ple_gate.astype(BF16), w_ple=w_ple.astype(BF16), g_final=row(g_final),
    )


def kernel(x_prompt, x_sample, p_prompt, p_sample, state_conf_conv, state_ssm_conv, state_ssm, g_mix, w_in, conf_dw_w, conf_dw_b, conf_ln_g, conf_ln_b, ssm_conv_w, ssm_conv_b, dt_bias, a_log, d_skip, ssm_norm_g, w_out, g_ffn, w_rg, b_rg, w_re, b_re, w_gate_e, w_up_e, w_down_e, g_ple, w_ple_gate, w_ple, g_final):
    depth = g_mix.shape[0]
    bsz, seq, _ = x_prompt.shape
    nb = x_sample.shape[0]
    hp = x_prompt
    hs = x_sample.reshape(nb, D_MODEL)
    cp_l, mp_l, sp_l, cs_l, ms_l, ss_l = [], [], [], [], [], []
    for i in range(depth):
        w = _prep_weights(g_mix[i], w_in[i], conf_dw_w[i], conf_dw_b[i], conf_ln_g[i], conf_ln_b[i],
                          ssm_conv_w[i], ssm_conv_b[i], dt_bias[i], a_log[i], d_skip[i], ssm_norm_g[i], w_out[i],
                          g_ffn[i], w_rg[i], b_rg[i], w_re[i], b_re[i], w_gate_e[i], w_up_e[i], w_down_e[i],
                          g_ple[i], w_ple_gate[i], w_ple[i], g_final)
        h1p, c, m, s = _mixer_prompt(hp, w)
        cp_l.append(c)
        mp_l.append(m)
        sp_l.append(s)
        h1s, c, m, s = _mixer_sample(hs, jnp.transpose(state_conf_conv[i], (1, 0, 2)), state_ssm_conv[i],
                                     state_ssm[i], w)
        cs_l.append(jnp.transpose(c, (1, 0, 2)))
        ms_l.append(m)
        ss_l.append(s)
        assert depth == 1
        hp = _ffn_sparse(h1p.reshape(bsz * seq, D_MODEL), p_prompt[i].reshape(bsz * seq, PLE_DIM), w)
        hp = hp.reshape(bsz, seq, D_MODEL)
        hs = _ffn(h1s, p_sample[i].reshape(nb, PLE_DIM), w, tb=nb)
    return (hp, hs.reshape(nb, 1, D_MODEL), jnp.stack(cp_l), jnp.stack(mp_l), jnp.stack(sp_l),
            jnp.stack(cs_l), jnp.stack(ms_l), jnp.stack(ss_l))
```

```python
import functools

import jax
import jax.numpy as jnp
from jax import lax
from jax.experimental import pallas as pl
from jax.experimental.pallas import tpu as pltpu
from jax.experimental.pallas import tpu_sc as plsc

F32 = jnp.float32
BF16 = jnp.bfloat16

D_MODEL = 1024
C_CONF = 1024
K_CONF = 31
D_INNER = 1024
HEAD_DIM = 64
N_HEADS = 16
N_GROUPS = 4
HEADS_PER_GROUP = N_HEADS // N_GROUPS
GROUP_W = HEADS_PER_GROUP * HEAD_DIM
D_STATE = 128
K_SSM = 4
CHUNK = 128
CONV_DIM = D_INNER + 2 * N_GROUPS * D_STATE
IN_COLS = 2 * C_CONF + D_INNER + CONV_DIM + N_HEADS
O_GATE = C_CONF
O_Z = 2 * C_CONF
O_XBC = O_Z + D_INNER
O_DT = O_XBC + CONV_DIM
N_EXPERT_GROUPS = 4
EXPERTS_PER_GROUP = 4
N_EXPERTS = 16
D_FF = 512
PLE_DIM = 256
EPS = 1e-6

LANES = 128
SUBLANES = 8
N_LANE_TILES = C_CONF // LANES
N_XBC_TILES = CONV_DIM // LANES
CONF_PAD = 32
SSM_PAD = 8
VMEM_LIMIT = 56 * 1024 * 1024


def _dot(a, b):
    return jnp.dot(a, b, preferred_element_type=F32)


def _dot_nt(a, b):
    return lax.dot_general(a, b, (((1,), (1,)), ((), ())), preferred_element_type=F32)


def _dot_tn(a, b):
    return lax.dot_general(a, b, (((0,), (0,)), ((), ())), preferred_element_type=F32)


def _split3(v):
    hi = v.astype(BF16)
    r = v - hi.astype(F32)
    mid = r.astype(BF16)
    lo = (r - mid.astype(F32)).astype(BF16)
    return hi, mid, lo


def _rms(x, g):
    return x * lax.rsqrt(jnp.mean(x * x, axis=-1, keepdims=True) + EPS) * g


def _sigmoid(x):
    return jax.nn.sigmoid(x)


def _silu(x):
    return x * jax.nn.sigmoid(x)


def _softplus(x):
    return jax.nn.softplus(x)


def _mixer_prompt_kernel(x_ref, gmix_ref, win_ref, wdtT_ref, cw_ref, cb_ref, lng_ref, lnb_ref,
                         sw_ref, sb_ref, dtb_ref, dtbT_ref, alog_ref, alogT_ref, dskip_ref, sng_ref,
                         wout_ref, hexp_ref,
                         h1_ref, ncc_ref, nsc_ref, nss_ref,
                         cscr, cout, mscr, xbc_scr, st_scr, *, tl):
    t = pl.program_id(1)
    nt = pl.num_programs(1)

    @pl.when(t == 0)
    def _():
        cscr[:, 0:CONF_PAD, :] = jnp.zeros((N_LANE_TILES, CONF_PAD, LANES), F32)
        mscr[:, 0:SSM_PAD, :] = jnp.zeros((N_XBC_TILES, SSM_PAD, LANES), F32)
        st_scr[...] = jnp.zeros(st_scr.shape, F32)

    x = x_ref[0]
    u = _rms(x, gmix_ref[...]).astype(BF16)

    glu = _dot(u, win_ref[:, 0:O_GATE]) * _sigmoid(_dot(u, win_ref[:, O_GATE:O_Z]))
    for lc in range(N_LANE_TILES):
        cscr[lc, CONF_PAD:CONF_PAD + tl, :] = glu[:, lc * LANES:(lc + 1) * LANES]

    rc = 64

    def conv_lane_tile(lc, carry):
        bias = cb_ref[lc]
        for r0 in range(0, tl, rc):
            acc = jnp.broadcast_to(bias, (rc, LANES))
            for k in range(K_CONF):
                acc = acc + cw_ref[lc, pl.ds(k, 1), :] * cscr[lc, pl.ds(r0 + k + CONF_PAD - (K_CONF - 1), rc), :]
            cout[lc, pl.ds(r0, rc), :] = acc
        return carry

    for lc in range(N_LANE_TILES):
        conv_lane_tile(lc, 0)

    for lc in range(N_LANE_TILES):
        cscr[lc, 0:CONF_PAD, :] = cscr[lc, tl:tl + CONF_PAD, :]

    cc = [cout[lc] for lc in range(N_LANE_TILES)]
    tot = cc[0]
    for lc in range(1, N_LANE_TILES):
        tot = tot + cc[lc]
    mean = jnp.sum(tot, axis=-1, keepdims=True) * (1.0 / C_CONF)
    xc = [c - mean for c in cc]
    sq = xc[0] * xc[0]
    for lc in range(1, N_LANE_TILES):
        sq = sq + xc[lc] * xc[lc]
    rstd = lax.rsqrt(jnp.sum(sq, axis=-1, keepdims=True) * (1.0 / C_CONF) + EPS)
    a_out = jnp.concatenate(
        [_silu(xc[lc] * rstd * lng_ref[:, lc * LANES:(lc + 1) * LANES] + lnb_ref[:, lc * LANES:(lc + 1) * LANES])
         for lc in range(N_LANE_TILES)], axis=-1).astype(BF16)

    z = _dot(u, win_ref[:, O_Z:O_XBC])
    xbc_raw = _dot(u, win_ref[:, O_XBC:O_DT])
    for lt in range(N_XBC_TILES):
        mscr[lt, SSM_PAD:SSM_PAD + tl, :] = xbc_raw[:, lt * LANES:(lt + 1) * LANES]

    def ssm_conv_lane_tile(lt, carry):
        bias = sb_ref[lt]
        for r0 in range(0, tl, rc):
            acc = jnp.broadcast_to(bias, (rc, LANES))
            for k in range(K_SSM):
                acc = acc + sw_ref[lt, pl.ds(k, 1), :] * mscr[lt, pl.ds(r0 + k + SSM_PAD - (K_SSM - 1), rc), :]
            xbc_scr[lt, pl.ds(r0, rc), :] = _silu(acc)
        mscr[lt, 0:SSM_PAD, :] = mscr[lt, tl:tl + SSM_PAD, :]
        return carry

    for lt in range(N_XBC_TILES):
        ssm_conv_lane_tile(lt, 0)

    n_x = D_INNER // LANES
    n_b = N_GROUPS * D_STATE // LANES
    xs = jnp.concatenate([xbc_scr[lt] for lt in range(n_x)], axis=-1)
    bm = jnp.concatenate([xbc_scr[lt] for lt in range(n_x, n_x + n_b)], axis=-1)
    cm = jnp.concatenate([xbc_scr[lt] for lt in range(n_x + n_b, N_XBC_TILES)], axis=-1)

    dt = _softplus(_dot(u, win_ref[:, O_DT:IN_COLS]) + dtb_ref[...])
    dtT = _softplus(_dot_nt(wdtT_ref[...], u) + dtbT_ref[...])
    a = dt * (-jnp.exp(alog_ref[...]))
    aT = dtT * (-jnp.exp(alogT_ref[...]))
    hexp = hexp_ref[...]
    d_h, d_m, d_l = _split3(dt)
    xdt_all = xs * (_dot(d_h, hexp) + _dot(d_m, hexp) + _dot(d_l, hexp))

    row = lax.broadcasted_iota(jnp.int32, (CHUNK, CHUNK), 0)
    col = lax.broadcasted_iota(jnp.int32, (CHUNK, CHUNK), 1)
    lower = row >= col
    tri = lower.astype(BF16)
    triT = (row <= col).astype(BF16)

    y_chunks = []
    for c in range(tl // CHUNK):
        r0 = c * CHUNK
        a_c = a[r0:r0 + CHUNK]
        aT_c = aT[:, r0:r0 + CHUNK]
        ah, am, al = _split3(a_c)
        cs = _dot(tri, ah) + _dot(tri, am) + _dot(tri, al)
        th, tm, tlo = _split3(aT_c)
        csT = _dot(th, triT) + _dot(tm, triT) + _dot(tlo, triT)
        cs_last = cs[CHUNK - 1:CHUNK, :]
        cdec = jnp.exp(cs_last)
        xdt_c = xdt_all[r0:r0 + CHUNK]
        y_heads = []
        for g in range(N_GROUPS):
            cg = cm[r0:r0 + CHUNK, g * D_STATE:(g + 1) * D_STATE].astype(BF16)
            bg = bm[r0:r0 + CHUNK, g * D_STATE:(g + 1) * D_STATE].astype(BF16)
            cb = _dot_nt(cg, bg)
            y_off = _dot(cg, st_scr[g].astype(BF16))
            xdd = []
            dec_row = []
            m_parts_g = []
            x_bd = []
            e_parts = []
            xdt_g = xdt_c[:, g * GROUP_W:(g + 1) * GROUP_W]
            lane_head = lax.broadcasted_iota(jnp.int32, (CHUNK, GROUP_W), 1) // HEAD_DIM
            for hh in range(HEADS_PER_GROUP):
                h = g * HEADS_PER_GROUP + hh
                xdt = xdt_g[:, hh * HEAD_DIM:(hh + 1) * HEAD_DIM]
                cs_b = jnp.broadcast_to(cs[:, h:h + 1], (CHUNK, CHUNK))
                lmat = jnp.where(lower, jnp.exp(cs_b - csT[h:h + 1, :]), 0.0)
                m_parts_g.append((cb * lmat).astype(BF16))
                x_bd.append(jnp.where(lane_head == hh, xdt_g, 0.0).astype(BF16))
                cs_bh = cs_b[:, 0:HEAD_DIM]
                e_parts.append(jnp.exp(cs_bh))
                xdd.append((xdt * jnp.exp(csT[h:h + 1, CHUNK - 1:CHUNK] - cs_bh)).astype(BF16))
                dec_row.append(jnp.broadcast_to(cdec[:, h:h + 1], (1, HEAD_DIM)))
            y_diag = _dot(jnp.concatenate(m_parts_g, axis=1), jnp.concatenate(x_bd, axis=0))
            y_heads.append(y_diag + y_off * jnp.concatenate(e_parts, axis=-1))
            contrib = _dot_tn(bg, jnp.concatenate(xdd, axis=-1))
            st_scr[g] = st_scr[g] * jnp.concatenate(dec_row, axis=-1) + contrib
        y_chunks.append(jnp.concatenate(y_heads, axis=-1))
    y = y_chunks[0] if len(y_chunks) == 1 else jnp.concatenate(y_chunks, axis=0)
    y = (y + dskip_ref[...] * xs) * _silu(z)
    m_parts = []
    for g in range(N_GROUPS):
        yg = y[:, g * GROUP_W:(g + 1) * GROUP_W]
        m_parts.append(_rms(yg, sng_ref[:, g * GROUP_W:(g + 1) * GROUP_W]))
    m_out = jnp.concatenate(m_parts, axis=-1).astype(BF16)

    h1_ref[0] = x + _dot(a_out, wout_ref[0:C_CONF, :]) + _dot(m_out, wout_ref[C_CONF:, :])

    @pl.when(t == nt - 1)
    def _():
        for lc in range(N_LANE_TILES):
            ncc_ref[0, :, lc * LANES:(lc + 1) * LANES] = cscr[lc, pl.ds(CONF_PAD - (K_CONF - 1), K_CONF - 1), :]
        for lt in range(N_XBC_TILES):
            nsc_ref[0, :, lt * LANES:(lt + 1) * LANES] = mscr[lt, pl.ds(SSM_PAD - (K_SSM - 1), K_SSM - 1), :]
        for g in range(N_GROUPS):
            nss_ref[0, g * HEADS_PER_GROUP:(g + 1) * HEADS_PER_GROUP] = (
                st_scr[g].T.reshape(HEADS_PER_GROUP, HEAD_DIM, D_STATE))


def _const_spec(shape):
    nd = len(shape)
    return pl.BlockSpec(shape, lambda *_: (0,) * nd, pipeline_mode=pl.Buffered(1))


def _mixer_prompt(x, w, tl=512):
    bsz, seq, _ = x.shape
    kern = functools.partial(_mixer_prompt_kernel, tl=tl)
    consts = [w['g_mix'], w['w_in'], w['w_dtT'], w['conf_w'], w['conf_b'], w['ln_g'], w['ln_b'],
              w['ssm_w3'], w['ssm_b3'], w['dt_bias'], w['dt_biasT'], w['a_log'], w['a_logT'], w['d_skip'],
              w['ssm_norm_g'], w['w_out'], w['head_expand']]
    return pl.pallas_call(
        kern,
        grid=(bsz, seq // tl),
        in_specs=[pl.BlockSpec((1, tl, D_MODEL), lambda b, t: (b, t, 0))] + [_const_spec(c.shape) for c in consts],
        out_specs=[
            pl.BlockSpec((1, tl, D_MODEL), lambda b, t: (b, t, 0)),
            pl.BlockSpec((1, K_CONF - 1, C_CONF), lambda b, t: (b, 0, 0)),
            pl.BlockSpec((1, K_SSM - 1, CONV_DIM), lambda b, t: (b, 0, 0)),
            pl.BlockSpec((1, N_HEADS, HEAD_DIM, D_STATE), lambda b, t: (b, 0, 0, 0)),
        ],
        out_shape=[
            jax.ShapeDtypeStruct((bsz, seq, D_MODEL), F32),
            jax.ShapeDtypeStruct((bsz, K_CONF - 1, C_CONF), F32),
            jax.ShapeDtypeStruct((bsz, K_SSM - 1, CONV_DIM), F32),
            jax.ShapeDtypeStruct((bsz, N_HEADS, HEAD_DIM, D_STATE), F32),
        ],
        scratch_shapes=[
            pltpu.VMEM((N_LANE_TILES, CONF_PAD + tl, LANES), F32),
            pltpu.VMEM((N_LANE_TILES, tl, LANES), F32),
            pltpu.VMEM((N_XBC_TILES, SSM_PAD + tl, LANES), F32),
            pltpu.VMEM((N_XBC_TILES, tl, LANES), F32),
            pltpu.VMEM((N_GROUPS, D_STATE, GROUP_W), F32),
        ],
        compiler_params=pltpu.CompilerParams(
            dimension_semantics=("arbitrary", "arbitrary"), vmem_limit_bytes=VMEM_LIMIT),
        name="mixer_prompt",
    )(x, *consts)


def _mixer_sample_kernel(x_ref, cst_ref, mst_ref, sst_ref, gmix_ref, win_ref, cw_ref, cb_ref, lng_ref, lnb_ref,
                         sw_ref, sb_ref, dtb_ref, alog_ref, dskip_ref, sng_ref, wout_ref, hexp_ref,
                         h1_ref, ncc_ref, nsc_ref, nss_ref,
                         proj_scr, mix_scr, *, bb):
    i = pl.program_id(0)
    n = pl.num_programs(0)

    @pl.when(i == 0)
    def _():
        u = _rms(x_ref[...], gmix_ref[...]).astype(BF16)
        proj_scr[...] = _dot(u, win_ref[...])

    r0 = pl.multiple_of(i * bb, bb)
    proj = proj_scr[pl.ds(r0, bb), :]

    glu = proj[:, 0:O_GATE] * _sigmoid(proj[:, O_GATE:O_Z])
    acc = cb_ref[...] + cw_ref[pl.ds(K_CONF - 1, 1), :] * glu
    for k in range(K_CONF - 1):
        row_k = cst_ref[k]
        acc = acc + cw_ref[pl.ds(k, 1), :] * row_k
        if k >= 1:
            ncc_ref[k - 1] = row_k
    ncc_ref[K_CONF - 2] = glu
    mean = jnp.mean(acc, axis=-1, keepdims=True)
    xc = acc - mean
    rstd = lax.rsqrt(jnp.mean(xc * xc, axis=-1, keepdims=True) + EPS)
    a_out = _silu(xc * rstd * lng_ref[...] + lnb_ref[...])

    z = proj[:, O_Z:O_XBC]
    xbc_raw = proj[:, O_XBC:O_DT]
    acc = sb_ref[...] + sw_ref[pl.ds(K_SSM - 1, 1), :] * xbc_raw
    for k in range(K_SSM - 1):
        row_k = mst_ref[:, k, :]
        acc = acc + sw_ref[pl.ds(k, 1), :] * row_k
        if k >= 1:
            nsc_ref[:, k - 1, :] = row_k
    nsc_ref[:, K_SSM - 2, :] = xbc_raw
    xbc = _silu(acc)
    xs = xbc[:, 0:D_INNER]
    bm = xbc[:, D_INNER:D_INNER + N_GROUPS * D_STATE]
    cm = xbc[:, D_INNER + N_GROUPS * D_STATE:]
    dt = _softplus(proj[:, O_DT:IN_COLS] + dtb_ref[...])
    dec = jnp.exp(dt * (-jnp.exp(alog_ref[...])))
    hexp = hexp_ref[...]
    t_h, t_m, t_l = _split3(dt)
    dt_e = _dot(t_h, hexp) + _dot(t_m, hexp) + _dot(t_l, hexp)
    xdt = xs * dt_e

    lane = lax.broadcasted_iota(jnp.int32, (SUBLANES, D_INNER), 1)
    sub = lax.broadcasted_iota(jnp.int32, (SUBLANES, D_INNER), 0)
    gmask = (lane // GROUP_W) == sub
    eye_h = (lax.broadcasted_iota(jnp.int32, (N_HEADS, N_HEADS), 0)
             == lax.broadcasted_iota(jnp.int32, (N_HEADS, N_HEADS), 1))
    ones_hn = jnp.ones((3 * N_HEADS, D_STATE), BF16)

    y_rows = []
    for b in range(bb):
        xrow = jnp.where(gmask, jnp.broadcast_to(xdt[b:b + 1, :], (SUBLANES, D_INNER)), 0.0)
        x_h = xrow.astype(BF16).astype(F32)
        x_m = (xrow - x_h).astype(BF16).astype(F32)
        bmat = jnp.concatenate([bm[b:b + 1, g * D_STATE:(g + 1) * D_STATE] for g in range(N_GROUPS)]
                               + [jnp.zeros((SUBLANES - N_GROUPS, D_STATE), F32)], axis=0)
        b_h = bmat.astype(BF16).astype(F32)
        b_m = (bmat - b_h).astype(BF16).astype(F32)
        lhs = jnp.concatenate([x_h, x_h, x_m], axis=0).astype(BF16)
        rhs = jnp.concatenate([b_h, b_m, b_h], axis=0).astype(BF16)
        upd = _dot_tn(lhs, rhs)
        dg = jnp.where(eye_h, jnp.broadcast_to(dec[b:b + 1, :], (N_HEADS, N_HEADS)), 0.0)
        g_h, g_m, g_l = [v.astype(F32) for v in _split3(dg)]
        drow = _dot(jnp.concatenate([g_h, g_m, g_l], axis=1).astype(BF16), ones_hn)
        h_parts = []
        for h in range(N_HEADS):
            h_new = drow[h:h + 1, :] * sst_ref[b, h] + upd[h * HEAD_DIM:(h + 1) * HEAD_DIM, :]
            nss_ref[b, h] = h_new
            h_parts.append(h_new)
        hnew = jnp.concatenate(h_parts, axis=0)
        cmat = jnp.concatenate([cm[b:b + 1, g * D_STATE:(g + 1) * D_STATE] for g in range(N_GROUPS)]
                               + [jnp.zeros((SUBLANES - N_GROUPS, D_STATE), F32)], axis=0)
        yg = _dot_nt(cmat.astype(BF16), hnew.astype(BF16))
        y_rows.append(jnp.sum(jnp.where(gmask, yg, 0.0), axis=0, keepdims=True))
    y = jnp.concatenate(y_rows, axis=0)
    y = (y + dskip_ref[...] * xs) * _silu(z)
    m_parts = []
    for g in range(N_GROUPS):
        m_parts.append(_rms(y[:, g * GROUP_W:(g + 1) * GROUP_W], sng_ref[:, g * GROUP_W:(g + 1) * GROUP_W]))
    mix_scr[pl.ds(r0, bb), :] = jnp.concatenate([a_out] + m_parts, axis=-1)

    @pl.when(i == n - 1)
    def _():
        h1_ref[...] = x_ref[...] + _dot(mix_scr[...].astype(BF16), wout_ref[...])


def _mixer_sample(x, cst, mst, sst, w, bb=8):
    nb = x.shape[0]
    kern = functools.partial(_mixer_sample_kernel, bb=bb)
    consts = [w['g_mix'], w['w_in'], w['conf_w2'], w['conf_b2'], w['ln_g'], w['ln_b'],
              w['ssm_w'], w['ssm_b'], w['dt_bias'], w['a_log'], w['d_skip'], w['ssm_norm_g'], w['w_out'],
              w['head_expand']]
    return pl.pallas_call(
        kern,
        grid=(nb // bb,),
        in_specs=[
            _const_spec((nb, D_MODEL)),
            pl.BlockSpec((K_CONF - 1, bb, C_CONF), lambda i: (0, i, 0)),
            pl.BlockSpec((bb, K_SSM - 1, CONV_DIM), lambda i: (i, 0, 0)),
            pl.BlockSpec((bb, N_HEADS, HEAD_DIM, D_STATE), lambda i: (i, 0, 0, 0)),
        ] + [_const_spec(c.shape) for c in consts],
        out_specs=[
            pl.BlockSpec((nb, D_MODEL), lambda i: (0, 0)),
            pl.BlockSpec((K_CONF - 1, bb, C_CONF), lambda i: (0, i, 0)),
            pl.BlockSpec((bb, K_SSM - 1, CONV_DIM), lambda i: (i, 0, 0)),
            pl.BlockSpec((bb, N_HEADS, HEAD_DIM, D_STATE), lambda i: (i, 0, 0, 0)),
        ],
        out_shape=[
            jax.ShapeDtypeStruct((nb, D_MODEL), F32),
            jax.ShapeDtypeStruct((K_CONF - 1, nb, C_CONF), F32),
            jax.ShapeDtypeStruct((nb, K_SSM - 1, CONV_DIM), F32),
            jax.ShapeDtypeStruct((nb, N_HEADS, HEAD_DIM, D_STATE), F32),
        ],
        scratch_shapes=[
            pltpu.VMEM((nb, IN_COLS), F32),
            pltpu.VMEM((nb, D_MODEL + D_INNER), F32),
        ],
        compiler_params=pltpu.CompilerParams(dimension_semantics=("arbitrary",), vmem_limit_bytes=VMEM_LIMIT),
        name="mixer_sample",
    )(x, cst, mst, sst, *consts)


def _first_argmax(v, width):
    lane = lax.broadcasted_iota(jnp.int32, v.shape, 1)
    m = jnp.max(v, axis=-1, keepdims=True)
    idx = jnp.min(jnp.where(v == m, lane, width), axis=-1, keepdims=True)
    return m, idx


def _ffn_kernel(h_ref, p_ref, gffn_ref, wr_ref, br_ref, wg_ref, wu_ref, wd_ref, gple_ref, wpg_ref, wp_ref,
                gfin_ref, y_ref, t_scr, comb_scr, acc_scr):
    e = pl.program_id(1)
    ne = pl.num_programs(1)

    @pl.when(e == 0)
    def _():
        tf = _rms(h_ref[...], gffn_ref[...])
        t_scr[...] = tf.astype(BF16)
        t_h, t_m, t_l = _split3(tf)
        w_h = wr_ref[0]
        w_m = wr_ref[1]
        logits = (_dot(t_h, w_h) + _dot(t_m, w_h) + _dot(t_h, w_m) + _dot(t_l, w_h) + _dot(t_m, w_m)) + br_ref[...]
        lg = logits[:, 0:N_EXPERT_GROUPS]
        le = logits[:, N_EXPERTS:2 * N_EXPERTS]
        eg = jnp.exp(lg - jnp.max(lg, axis=-1, keepdims=True))
        pg = eg / jnp.sum(eg, axis=-1, keepdims=True)
        g_val, g_idx = _first_argmax(pg, N_EXPERT_GROUPS)
        lane16 = lax.broadcasted_iota(jnp.int32, le.shape, 1)
        in_grp = (lane16 // EXPERTS_PER_GROUP) == g_idx
        neg = jnp.float32(-jnp.inf)
        le_m = jnp.where(in_grp, le, neg)
        ee = jnp.where(in_grp, jnp.exp(le - jnp.max(le_m, axis=-1, keepdims=True)), 0.0)
        pe = ee / jnp.sum(ee, axis=-1, keepdims=True)
        pe_m = jnp.where(in_grp, pe, -1.0)
        v1, i1 = _first_argmax(pe_m, N_EXPERTS)
        pe_m2 = jnp.where(lane16 == i1, -1.0, pe_m)
        v2, i2 = _first_argmax(pe_m2, N_EXPERTS)
        den = v1 + v2
        comb = jnp.where(lane16 == i1, g_val * v1 / den, 0.0) + jnp.where(lane16 == i2, g_val * v2 / den, 0.0)
        comb_scr[...] = comb
        acc_scr[...] = jnp.zeros(acc_scr.shape, F32)

    t = t_scr[...]
    hdn = _silu(_dot(t, wg_ref[0].astype(BF16))) * _dot(t, wu_ref[0].astype(BF16))
    out_e = _dot(hdn.astype(BF16), wd_ref[0].astype(BF16))
    lane16 = lax.broadcasted_iota(jnp.int32, comb_scr.shape, 1)
    c_e = jnp.sum(jnp.where(lane16 == e, comb_scr[...], 0.0), axis=-1, keepdims=True)
    acc_scr[...] += c_e * out_e

    @pl.when(e == ne - 1)
    def _():
        h2 = h_ref[...] + acc_scr[...]
        gate = _sigmoid(_dot(_rms(h2, gple_ref[...]).astype(BF16), wpg_ref[...]))
        h3 = h2 + _dot(p_ref[...].astype(BF16), wp_ref[...]) * gate
        y_ref[...] = _rms(h3, gfin_ref[...])


def _ffn(h, p, w, tb):
    nt = h.shape[0]
    return pl.pallas_call(
        _ffn_kernel,
        grid=(nt // tb, N_EXPERTS),
        in_specs=[
            pl.BlockSpec((tb, D_MODEL), lambda i, e: (i, 0)),
            pl.BlockSpec((tb, PLE_DIM), lambda i, e: (i, 0)),
            _const_spec((1, D_MODEL)),
            _const_spec((2, D_MODEL, 2 * N_EXPERTS)),
            _const_spec((1, 2 * N_EXPERTS)),
            pl.BlockSpec((1, D_MODEL, D_FF), lambda i, e: (e, 0, 0)),
            pl.BlockSpec((1, D_MODEL, D_FF), lambda i, e: (e, 0, 0)),
            pl.BlockSpec((1, D_FF, D_MODEL), lambda i, e: (e, 0, 0)),
            _const_spec((1, D_MODEL)),
            _const_spec((D_MODEL, D_MODEL)),
            _const_spec((PLE_DIM, D_MODEL)),
            _const_spec((1, D_MODEL)),
        ],
        out_specs=pl.BlockSpec((tb, D_MODEL), lambda i, e: (i, 0)),
        out_shape=jax.ShapeDtypeStruct((nt, D_MODEL), F32),
        scratch_shapes=[
            pltpu.VMEM((tb, D_MODEL), BF16),
            pltpu.VMEM((tb, N_EXPERTS), F32),
            pltpu.VMEM((tb, D_MODEL), F32),
        ],
        compiler_params=pltpu.CompilerParams(
            dimension_semantics=("arbitrary", "arbitrary"), vmem_limit_bytes=VMEM_LIMIT),
        name="ffn",
    )(h, p, w['g_ffn'], w['w_router'], w['b_router'], w['w_gate_e'], w['w_up_e'], w['w_down_e'],
      w['g_ple'], w['w_ple_gate'], w['w_ple'], w['g_final'])


ROUTE_TB = 1024
EXPERT_TM = 512
SC_WINDOW = 128
SC_CORES = 2
SC_SUBCORES = 16
SC_WORKERS = SC_CORES * SC_SUBCORES
HALF = D_MODEL // 2


def _pack_bf16_pairs(v):
    bits = pltpu.bitcast(v.astype(BF16).astype(F32), jnp.uint32)
    packed = bits[:, HALF:] | (bits[:, :HALF] >> 16)
    return pltpu.bitcast(packed, jnp.int32)


def _unpack_bf16_pairs(p):
    u = pltpu.bitcast(p, jnp.uint32)
    lo = pltpu.bitcast(u << 16, F32)
    hi = pltpu.bitcast(u & jnp.uint32(0xFFFF0000), F32)
    return jnp.concatenate([lo, hi], axis=-1)


def _route_kernel(h_ref, gffn_ref, wrT_ref, brT_ref, tp_ref, meta_ref, wts_ref, cnt_ref, upper_scr, carry_scr):
    i = pl.program_id(0)
    tb = h_ref.shape[0]

    @pl.when(i == 0)
    def _():
        r = lax.broadcasted_iota(jnp.int32, (tb, tb), 0)
        c = lax.broadcasted_iota(jnp.int32, (tb, tb), 1)
        upper_scr[...] = (r < c).astype(BF16)
        carry_scr[...] = jnp.zeros(carry_scr.shape, F32)

    tf = _rms(h_ref[...], gffn_ref[...])
    tp_ref[...] = _pack_bf16_pairs(tf)
    t_h = tf.astype(BF16)
    t_m = (tf - t_h.astype(F32)).astype(BF16)
    w_h = wrT_ref[0]
    w_m = wrT_ref[1]
    logits = (_dot_nt(w_h, t_h) + _dot_nt(w_h, t_m) + _dot_nt(w_m, t_h)) + brT_ref[...]
    neg = -jnp.inf
    row8 = lax.broadcasted_iota(jnp.int32, (SUBLANES, tb), 0)
    row16 = lax.broadcasted_iota(jnp.int32, (N_EXPERTS, tb), 0)
    lg = jnp.where(row8 < N_EXPERT_GROUPS, logits[0:SUBLANES], neg)
    le = logits[N_EXPERTS:2 * N_EXPERTS]
    eg = jnp.exp(lg - jnp.max(lg, axis=0, keepdims=True))
    pg = eg / jnp.sum(eg, axis=0, keepdims=True)
    g_val = jnp.max(pg, axis=0, keepdims=True)
    g_idx = jnp.min(jnp.where(pg == g_val, row8, SUBLANES), axis=0, keepdims=True)
    in_grp = (row16 // EXPERTS_PER_GROUP) == g_idx
    le_m = jnp.where(in_grp, le, neg)
    ee = jnp.where(in_grp, jnp.exp(le - jnp.max(le_m, axis=0, keepdims=True)), 0.0)
    pe = ee / jnp.sum(ee, axis=0, keepdims=True)
    pe_m = jnp.where(in_grp, pe, -1.0)
    v1 = jnp.max(pe_m, axis=0, keepdims=True)
    i1 = jnp.min(jnp.where(pe_m == v1, row16, N_EXPERTS), axis=0, keepdims=True)
    pe_m2 = jnp.where(row16 == i1, -1.0, pe_m)
    v2 = jnp.max(pe_m2, axis=0, keepdims=True)
    i2 = jnp.min(jnp.where(pe_m2 == v2, row16, N_EXPERTS), axis=0, keepdims=True)
    den = v1 + v2
    w0 = g_val * v1 / den
    w1 = g_val * v2 / den

    sel0 = row16 == i1
    sel1 = row16 == i2
    hot = jnp.where(sel0 | sel1, 1.0, 0.0)
    rank = _dot(hot.astype(BF16), upper_scr[...]) + carry_scr[...]
    r0 = jnp.sum(jnp.where(sel0, rank, 0.0), axis=0, keepdims=True)
    r1 = jnp.sum(jnp.where(sel1, rank, 0.0), axis=0, keepdims=True)
    carry_scr[...] += jnp.sum(hot, axis=1, keepdims=True)
    cnt_ref[...] = carry_scr[...]

    meta = jnp.where(row8 == 0, i1, jnp.where(row8 == 1, i2, jnp.where(
        row8 == 2, r0.astype(jnp.int32), jnp.where(row8 == 3, r1.astype(jnp.int32), 0))))
    meta_ref[...] = meta
    row128 = lax.broadcasted_iota(jnp.int32, (LANES, tb), 0)
    wts_ref[...] = jnp.where(row128 == 0, w0, jnp.where(row128 == 1, w1, 0.0)).T


def _route(h, w, tb=ROUTE_TB):
    nt = h.shape[0]
    return pl.pallas_call(
        _route_kernel,
        grid=(nt // tb,),
        in_specs=[
            pl.BlockSpec((tb, D_MODEL), lambda i: (i, 0)),
            _const_spec((1, D_MODEL)),
            _const_spec((2, 2 * N_EXPERTS, D_MODEL)),
            _const_spec((2 * N_EXPERTS, 1)),
        ],
        out_specs=[
            pl.BlockSpec((tb, HALF), lambda i: (i, 0)),
            pl.BlockSpec((SUBLANES, tb), lambda i: (0, i)),
            pl.BlockSpec((tb, LANES), lambda i: (i, 0)),
            pl.BlockSpec((N_EXPERTS, 1), lambda i: (0, 0)),
        ],
        out_shape=[
            jax.ShapeDtypeStruct((nt, HALF), jnp.int32),
            jax.ShapeDtypeStruct((SUBLANES, nt), jnp.int32),
            jax.ShapeDtypeStruct((nt, LANES), F32),
            jax.ShapeDtypeStruct((N_EXPERTS, 1), F32),
        ],
        scratch_shapes=[pltpu.VMEM((tb, tb), BF16), pltpu.VMEM((N_EXPERTS, 1), F32)],
        compiler_params=pltpu.CompilerParams(dimension_semantics=("arbitrary",), vmem_limit_bytes=VMEM_LIMIT),
        name="moe_route",
    )(h, w['g_ffn'], w['w_routerT'], w['b_routerT'])


def _sc_mesh():
    return plsc.VectorSubcoreMesh(core_axis_name="c", subcore_axis_name="s")


def _sc_dispatch(tp, slot0, slot1, n_slots):
    nt = tp.shape[0]

    per_worker = nt // (SC_WINDOW * SC_WORKERS)

    @pl.kernel(out_type=jax.ShapeDtypeStruct((n_slots, HALF), tp.dtype), mesh=_sc_mesh(), name="moe_dispatch",
               scratch_types=[pltpu.VMEM((1, SC_WINDOW), jnp.int32), pltpu.VMEM((1, SC_WINDOW), jnp.int32),
                              pltpu.VMEM((SC_WINDOW, HALF), tp.dtype)])
    def run(x_hbm, i0_hbm, i1_hbm, o_hbm, i0_v, i1_v, buf_v):
        worker = lax.axis_index("c") * SC_SUBCORES + lax.axis_index("s")

        @pl.loop(0, per_worker)
        def _(k):
            base = (worker * per_worker + k) * SC_WINDOW
            pltpu.sync_copy(i0_hbm.at[:, pl.ds(base, SC_WINDOW)], i0_v)
            pltpu.sync_copy(i1_hbm.at[:, pl.ds(base, SC_WINDOW)], i1_v)
            pltpu.sync_copy(x_hbm.at[pl.ds(base, SC_WINDOW)], buf_v)
            pltpu.sync_copy(buf_v, o_hbm.at[i0_v.at[0]])
            pltpu.sync_copy(buf_v, o_hbm.at[i1_v.at[0]])

    return run(tp, slot0, slot1)


def _sc_combine(ys, slot0, slot1):
    nt = slot0.shape[1]
    out = jax.ShapeDtypeStruct((nt, HALF), ys.dtype)

    per_worker = nt // (SC_WINDOW * SC_WORKERS)

    @pl.kernel(out_type=(out, out), mesh=_sc_mesh(), name="moe_combine",
               scratch_types=[pltpu.VMEM((1, SC_WINDOW), jnp.int32), pltpu.VMEM((1, SC_WINDOW), jnp.int32),
                              pltpu.VMEM((SC_WINDOW, HALF), ys.dtype)])
    def run(y_hbm, i0_hbm, i1_hbm, o0_hbm, o1_hbm, i0_v, i1_v, buf_v):
        worker = lax.axis_index("c") * SC_SUBCORES + lax.axis_index("s")

        @pl.loop(0, per_worker)
        def _(k):
            base = (worker * per_worker + k) * SC_WINDOW
            pltpu.sync_copy(i0_hbm.at[:, pl.ds(base, SC_WINDOW)], i0_v)
            pltpu.sync_copy(i1_hbm.at[:, pl.ds(base, SC_WINDOW)], i1_v)
            pltpu.sync_copy(y_hbm.at[i0_v.at[0]], buf_v)
            pltpu.sync_copy(buf_v, o0_hbm.at[pl.ds(base, SC_WINDOW)])
            pltpu.sync_copy(y_hbm.at[i1_v.at[0]], buf_v)
            pltpu.sync_copy(buf_v, o1_hbm.at[pl.ds(base, SC_WINDOW)])

    return run(ys, slot0, slot1)


def _experts_kernel(te_ref, nv_ref, x_ref, wg_ref, wu_ref, wd_ref, y_ref, wg_s, wu_s, wd_s):
    j = pl.program_id(0)

    @pl.when((j == 0) | (te_ref[j] != te_ref[jnp.maximum(j - 1, 0)]))
    def _():
        wg_s[...] = wg_ref[0].astype(BF16)
        wu_s[...] = wu_ref[0].astype(BF16)
        wd_s[...] = wd_ref[0].astype(BF16)

    @pl.when(j < nv_ref[0])
    def _():
        x = _unpack_bf16_pairs(x_ref[...]).astype(BF16)
        hdn = _silu(_dot(x, wg_s[...])) * _dot(x, wu_s[...])
        y_ref[...] = _pack_bf16_pairs(_dot(hdn.astype(BF16), wd_s[...]))


def _experts(xs, tile_expert, n_valid, w, tm=EXPERT_TM):
    n_slots = xs.shape[0]
    grid_spec = pltpu.PrefetchScalarGridSpec(
        num_scalar_prefetch=2,
        grid=(n_slots // tm,),
        in_specs=[
            pl.BlockSpec((tm, HALF), lambda j, te, nv: (j, 0)),
            pl.BlockSpec((1, D_MODEL, D_FF), lambda j, te, nv: (te[j], 0, 0)),
            pl.BlockSpec((1, D_MODEL, D_FF), lambda j, te, nv: (te[j], 0, 0)),
            pl.BlockSpec((1, D_FF, D_MODEL), lambda j, te, nv: (te[j], 0, 0)),
        ],
        out_specs=pl.BlockSpec((tm, HALF), lambda j, te, nv: (j, 0)),
        scratch_shapes=[pltpu.VMEM((D_MODEL, D_FF), BF16), pltpu.VMEM((D_MODEL, D_FF), BF16),
                        pltpu.VMEM((D_FF, D_MODEL), BF16)],
    )
    return pl.pallas_call(
        _experts_kernel,
        grid_spec=grid_spec,
        out_shape=jax.ShapeDtypeStruct((n_slots, HALF), jnp.int32),
        compiler_params=pltpu.CompilerParams(dimension_semantics=("arbitrary",), vmem_limit_bytes=VMEM_LIMIT),
        name="moe_experts",
    )(tile_expert, n_valid, xs, w['w_gate_e'], w['w_up_e'], w['w_down_e'])


def _finish_kernel(h_ref, y0_ref, y1_ref, wts_ref, p_ref, gple_ref, wpg_ref, wp_ref, gfin_ref, o_ref):
    wts = wts_ref[...]
    moe = wts[:, 0:1] * _unpack_bf16_pairs(y0_ref[...]) + wts[:, 1:2] * _unpack_bf16_pairs(y1_ref[...])
    h2 = h_ref[...] + moe
    gate = _sigmoid(_dot(_rms(h2, gple_ref[...]).astype(BF16), wpg_ref[...]))
    h3 = h2 + _dot(p_ref[...].astype(BF16), wp_ref[...]) * gate
    o_ref[...] = _rms(h3, gfin_ref[...])


def _finish(h, y0, y1, wts, p, w, tb=1024):
    nt = h.shape[0]
    return pl.pallas_call(
        _finish_kernel,
        grid=(nt // tb,),
        in_specs=[
            pl.BlockSpec((tb, D_MODEL), lambda i: (i, 0)),
            pl.BlockSpec((tb, HALF), lambda i: (i, 0)),
            pl.BlockSpec((tb, HALF), lambda i: (i, 0)),
            pl.BlockSpec((tb, LANES), lambda i: (i, 0)),
            pl.BlockSpec((tb, PLE_DIM), lambda i: (i, 0)),
            _const_spec((1, D_MODEL)),
            _const_spec((D_MODEL, D_MODEL)),
            _const_spec((PLE_DIM, D_MODEL)),
            _const_spec((1, D_MODEL)),
        ],
        out_specs=pl.BlockSpec((tb, D_MODEL), lambda i: (i, 0)),
        out_shape=jax.ShapeDtypeStruct((nt, D_MODEL), F32),
        compiler_params=pltpu.CompilerParams(dimension_semantics=("arbitrary",), vmem_limit_bytes=VMEM_LIMIT),
        name="moe_finish",
    )(h, y0, y1, wts, p, w['g_ple'], w['w_ple_gate'], w['w_ple'], w['g_final'])


def _ffn_sparse(h, p, w):
    nt = h.shape[0]
    tm = EXPERT_TM
    n_tiles = (nt * 2) // tm + N_EXPERTS
    tp, meta, wts, counts = _route(h, w)
    cnt = counts.reshape(N_EXPERTS).astype(jnp.int32)
    tiles_e = (cnt + tm - 1) // tm
    tile_end = jnp.cumsum(tiles_e)
    off = (tile_end - tiles_e) * tm
    n_valid = tile_end[-1:]
    tile_ids = jnp.arange(n_tiles, dtype=jnp.int32)
    tile_expert = jnp.sum((tile_ids[:, None] >= tile_end[None, :]).astype(jnp.int32), axis=1)
    last_expert = jnp.sum((n_valid - 1 >= tile_end).astype(jnp.int32))
    tile_expert = jnp.minimum(tile_expert, last_expert).astype(jnp.int32)
    eid = meta[0:2]
    slot = meta[2:4] + jnp.sum(jnp.where(eid[..., None] == jnp.arange(N_EXPERTS), off, 0), axis=-1)
    slot0 = slot[0:1]
    slot1 = slot[1:2]
    xs = _sc_dispatch(tp, slot0, slot1, n_tiles * tm)
    ys = _experts(xs, tile_expert, n_valid.astype(jnp.int32), w)
    y0, y1 = _sc_combine(ys, slot0, slot1)
    return _finish(h, y0, y1, wts, p, w)


def _prep_weights(g_mix, w_in, conf_dw_w, conf_dw_b, conf_ln_g, conf_ln_b, ssm_conv_w, ssm_conv_b,
                  dt_bias, a_log, d_skip, ssm_norm_g, w_out, g_ffn, w_rg, b_rg, w_re, b_re,
                  w_gate_e, w_up_e, w_down_e, g_ple, w_ple_gate, w_ple, g_final):
    row = lambda v: v.reshape(1, -1)
    w_in_b = w_in.astype(BF16)
    conf_w_pad = jnp.concatenate([conf_dw_w, jnp.zeros((CONF_PAD - K_CONF, C_CONF), F32)], axis=0)
    w_router = jnp.concatenate(
        [w_rg, jnp.zeros((D_MODEL, N_EXPERTS - N_EXPERT_GROUPS), F32), w_re], axis=1)
    wr_hi = w_router.astype(BF16)
    wr_mid = (w_router - wr_hi.astype(F32)).astype(BF16)
    b_router = jnp.concatenate([b_rg, jnp.zeros((N_EXPERTS - N_EXPERT_GROUPS,), F32), b_re]).reshape(1, -1)
    head_of_lane = jnp.arange(D_INNER) // HEAD_DIM
    head_expand = (head_of_lane[None, :] == jnp.arange(N_HEADS)[:, None]).astype(BF16)
    return dict(
        g_mix=row(g_mix), w_in=w_in_b, w_dtT=w_in_b[:, O_DT:].T,
        conf_w=conf_w_pad.reshape(CONF_PAD, N_LANE_TILES, LANES).transpose(1, 0, 2),
        conf_b=conf_dw_b.reshape(N_LANE_TILES, 1, LANES),
        conf_w2=conf_dw_w, conf_b2=row(conf_dw_b),
        ssm_w3=ssm_conv_w.reshape(K_SSM, N_XBC_TILES, LANES).transpose(1, 0, 2),
        ssm_b3=ssm_conv_b.reshape(N_XBC_TILES, 1, LANES),
        ln_g=row(conf_ln_g), ln_b=row(conf_ln_b), ssm_w=ssm_conv_w, ssm_b=row(ssm_conv_b),
        dt_bias=row(dt_bias), dt_biasT=dt_bias.reshape(-1, 1), a_log=row(a_log), a_logT=a_log.reshape(-1, 1),
        d_skip=row(jnp.repeat(d_skip, HEAD_DIM)), ssm_norm_g=row(ssm_norm_g), w_out=w_out.astype(BF16),
        head_expand=head_expand,
        g_ffn=row(g_ffn), w_router=jnp.stack([wr_hi, wr_mid]), b_router=b_router,
        w_routerT=jnp.stack([wr_hi.T, wr_mid.T]), b_routerT=b_router.reshape(-1, 1),
        w_gate_e=w_gate_e, w_up_e=w_up_e, w_down_e=w_down_e,
        g_ple=row(g_ple), w_ple_gate=w_ple_gate.astype(BF16), w_ple=w_ple.astype(BF16), g_final=row(g_final),
    )


def kernel(x_prompt, x_sample, p_prompt, p_sample, state_conf_conv, state_ssm_conv, state_ssm, g_mix, w_in, conf_dw_w, conf_dw_b, conf_ln_g, conf_ln_b, ssm_conv_w, ssm_conv_b, dt_bias, a_log, d_skip, ssm_norm_g, w_out, g_ffn, w_rg, b_rg, w_re, b_re, w_gate_e, w_up_e, w_down_e, g_ple, w_ple_gate, w_ple, g_final):
    depth = g_mix.shape[0]
    bsz, seq, _ = x_prompt.shape
    nb = x_sample.shape[0]
    hp = x_prompt
    hs = x_sample.reshape(nb, D_MODEL)
    cp_l, mp_l, sp_l, cs_l, ms_l, ss_l = [], [], [], [], [], []
    for i in range(depth):
        w = _prep_weights(g_mix[i], w_in[i], conf_dw_w[i], conf_dw_b[i], conf_ln_g[i], conf_ln_b[i],
                          ssm_conv_w[i], ssm_conv_b[i], dt_bias[i], a_log[i], d_skip[i], ssm_norm_g[i], w_out[i],
                          g_ffn[i], w_rg[i], b_rg[i], w_re[i], b_re[i], w_gate_e[i], w_up_e[i], w_down_e[i],
                          g_ple[i], w_ple_gate[i], w_ple[i], g_final)
        h1p, c, m, s = _mixer_prompt(hp, w)
        cp_l.append(c)
        mp_l.append(m)
        sp_l.append(s)
        h1s, c, m, s = _mixer_sample(hs, jnp.transpose(state_conf_conv[i], (1, 0, 2)), state_ssm_conv[i],
                                     state_ssm[i], w)
        cs_l.append(jnp.transpose(c, (1, 0, 2)))
        ms_l.append(m)
        ss_l.append(s)
        assert depth == 1
        hp = _ffn_sparse(h1p.reshape(bsz * seq, D_MODEL), p_prompt[i].reshape(bsz * seq, PLE_DIM), w)
        hp = hp.reshape(bsz, seq, D_MODEL)
        hs = _ffn(h1s, p_sample[i].reshape(nb, PLE_DIM), w, tb=nb)
    return (hp, hs.reshape(nb, 1, D_MODEL), jnp.stack(cp_l), jnp.stack(mp_l), jnp.stack(sp_l),
            jnp.stack(cs_l), jnp.stack(ms_l), jnp.stack(ss_l))
```

```python
import functools

import jax
import jax.numpy as jnp
from jax import lax
from jax.experimental import pallas as pl
from jax.experimental.pallas import tpu as pltpu
from jax.experimental.pallas import tpu_sc as plsc

F32 = jnp.float32
BF16 = jnp.bfloat16

D_MODEL = 1024
C_CONF = 1024
K_CONF = 31
D_INNER = 1024
HEAD_DIM = 64
N_HEADS = 16
N_GROUPS = 4
HEADS_PER_GROUP = N_HEADS // N_GROUPS
GROUP_W = HEADS_PER_GROUP * HEAD_DIM
D_STATE = 128
K_SSM = 4
CHUNK = 128
CONV_DIM = D_INNER + 2 * N_GROUPS * D_STATE
IN_COLS = 2 * C_CONF + D_INNER + CONV_DIM + N_HEADS
O_GATE = C_CONF
O_Z = 2 * C_CONF
O_XBC = O_Z + D_INNER
O_DT = O_XBC + CONV_DIM
N_EXPERT_GROUPS = 4
EXPERTS_PER_GROUP = 4
N_EXPERTS = 16
D_FF = 512
PLE_DIM = 256
EPS = 1e-6

LANES = 128
SUBLANES = 8
N_LANE_TILES = C_CONF // LANES
N_XBC_TILES = CONV_DIM // LANES
CONF_PAD = 32
SSM_PAD = 8
VMEM_LIMIT = 56 * 1024 * 1024


def _dot(a, b):
    return jnp.dot(a, b, preferred_element_type=F32)


def _dot_nt(a, b):
    return lax.dot_general(a, b, (((1,), (1,)), ((), ())), preferred_element_type=F32)


def _dot_tn(a, b):
    return lax.dot_general(a, b, (((0,), (0,)), ((), ())), preferred_element_type=F32)


def _split3(v):
    hi = v.astype(BF16)
    r = v - hi.astype(F32)
    mid = r.astype(BF16)
    lo = (r - mid.astype(F32)).astype(BF16)
    return hi, mid, lo


def _rms(x, g):
    return x * lax.rsqrt(jnp.mean(x * x, axis=-1, keepdims=True) + EPS) * g


def _sigmoid(x):
    return jax.nn.sigmoid(x)


def _silu(x):
    return x * jax.nn.sigmoid(x)


def _softplus(x):
    return jax.nn.softplus(x)


def _mixer_prompt_kernel(x_ref, gmix_ref, win_ref, wdtT_ref, cw_ref, cb_ref, lng_ref, lnb_ref,
                         sw_ref, sb_ref, dtb_ref, dtbT_ref, alog_ref, alogT_ref, dskip_ref, sng_ref,
                         wout_ref, hexp_ref,
                         h1_ref, ncc_ref, nsc_ref, nss_ref,
                         cscr, cout, mscr, xbc_scr, st_scr, *, tl):
    t = pl.program_id(1)
    nt = pl.num_programs(1)

    @pl.when(t == 0)
    def _():
        cscr[:, 0:CONF_PAD, :] = jnp.zeros((N_LANE_TILES, CONF_PAD, LANES), F32)
        mscr[:, 0:SSM_PAD, :] = jnp.zeros((N_XBC_TILES, SSM_PAD, LANES), F32)
        st_scr[...] = jnp.zeros(st_scr.shape, F32)

    x = x_ref[0]
    u = _rms(x, gmix_ref[...]).astype(BF16)

    glu = _dot(u, win_ref[:, 0:O_GATE]) * _sigmoid(_dot(u, win_ref[:, O_GATE:O_Z]))
    for lc in range(N_LANE_TILES):
        cscr[lc, CONF_PAD:CONF_PAD + tl, :] = glu[:, lc * LANES:(lc + 1) * LANES]

    rc = 64

    def conv_lane_tile(lc, carry):
        bias = cb_ref[lc]
        for r0 in range(0, tl, rc):
            acc = jnp.broadcast_to(bias, (rc, LANES))
            for k in range(K_CONF):
                acc = acc + cw_ref[lc, pl.ds(k, 1), :] * cscr[lc, pl.ds(r0 + k + CONF_PAD - (K_CONF - 1), rc), :]
            cout[lc, pl.ds(r0, rc), :] = acc
        return carry

    for lc in range(N_LANE_TILES):
        conv_lane_tile(lc, 0)

    for lc in range(N_LANE_TILES):
        cscr[lc, 0:CONF_PAD, :] = cscr[lc, tl:tl + CONF_PAD, :]

    cc = [cout[lc] for lc in range(N_LANE_TILES)]
    tot = cc[0]
    for lc in range(1, N_LANE_TILES):
        tot = tot + cc[lc]
    mean = jnp.sum(tot, axis=-1, keepdims=True) * (1.0 / C_CONF)
    xc = [c - mean for c in cc]
    sq = xc[0] * xc[0]
    for lc in range(1, N_LANE_TILES):
        sq = sq + xc[lc] * xc[lc]
    rstd = lax.rsqrt(jnp.sum(sq, axis=-1, keepdims=True) * (1.0 / C_CONF) + EPS)
    a_out = jnp.concatenate(
        [_silu(xc[lc] * rstd * lng_ref[:, lc * LANES:(lc + 1) * LANES] + lnb_ref[:, lc * LANES:(lc + 1) * LANES])
         for lc in range(N_LANE_TILES)], axis=-1).astype(BF16)

    z = _dot(u, win_ref[:, O_Z:O_XBC])
    xbc_raw = _dot(u, win_ref[:, O_XBC:O_DT])
    for lt in range(N_XBC_TILES):
        mscr[lt, SSM_PAD:SSM_PAD + tl, :] = xbc_raw[:, lt * LANES:(lt + 1) * LANES]

    def ssm_conv_lane_tile(lt, carry):
        bias = sb_ref[lt]
        for r0 in range(0, tl, rc):
            acc = jnp.broadcast_to(bias, (rc, LANES))
            for k in range(K_SSM):
                acc = acc + sw_ref[lt, pl.ds(k, 1), :] * mscr[lt, pl.ds(r0 + k + SSM_PAD - (K_SSM - 1), rc), :]
            xbc_scr[lt, pl.ds(r0, rc), :] = _silu(acc)
        mscr[lt, 0:SSM_PAD, :] = mscr[lt, tl:tl + SSM_PAD, :]
        return carry

    for lt in range(N_XBC_TILES):
        ssm_conv_lane_tile(lt, 0)

    n_x = D_INNER // LANES
    n_b = N_GROUPS * D_STATE // LANES
    xs = jnp.concatenate([xbc_scr[lt] for lt in range(n_x)], axis=-1)
    bm = jnp.concatenate([xbc_scr[lt] for lt in range(n_x, n_x + n_b)], axis=-1)
    cm = jnp.concatenate([xbc_scr[lt] for lt in range(n_x + n_b, N_XBC_TILES)], axis=-1)

    dt = _softplus(_dot(u, win_ref[:, O_DT:IN_COLS]) + dtb_ref[...])
    dtT = _softplus(_dot_nt(wdtT_ref[...], u) + dtbT_ref[...])
    a = dt * (-jnp.exp(alog_ref[...]))
    aT = dtT * (-jnp.exp(alogT_ref[...]))
    hexp = hexp_ref[...]
    d_h, d_m, d_l = _split3(dt)
    xdt_all = xs * (_dot(d_h, hexp) + _dot(d_m, hexp) + _dot(d_l, hexp))

    row = lax.broadcasted_iota(jnp.int32, (CHUNK, CHUNK), 0)
    col = lax.broadcasted_iota(jnp.int32, (CHUNK, CHUNK), 1)
    lower = row >= col
    tri = lower.astype(BF16)
    triT = (row <= col).astype(BF16)

    y_chunks = []
    for c in range(tl // CHUNK):
        r0 = c * CHUNK
        a_c = a[r0:r0 + CHUNK]
        aT_c = aT[:, r0:r0 + CHUNK]
        ah, am, al = _split3(a_c)
        cs = _dot(tri, ah) + _dot(tri, am) + _dot(tri, al)
        th, tm, tlo = _split3(aT_c)
        csT = _dot(th, triT) + _dot(tm, triT) + _dot(tlo, triT)
        cs_last = cs[CHUNK - 1:CHUNK, :]
        cdec = jnp.exp(cs_last)
        xdt_c = xdt_all[r0:r0 + CHUNK]
        y_heads = []
        for g in range(N_GROUPS):
            cg = cm[r0:r0 + CHUNK, g * D_STATE:(g + 1) * D_STATE].astype(BF16)
            bg = bm[r0:r0 + CHUNK, g * D_STATE:(g + 1) * D_STATE].astype(BF16)
            cb = _dot_nt(cg, bg)
            y_off = _dot(cg, st_scr[g].astype(BF16))
            xdd = []
            dec_row = []
            m_parts_g = []
            x_bd = []
            e_parts = []
            xdt_g = xdt_c[:, g * GROUP_W:(g + 1) * GROUP_W]
            lane_head = lax.broadcasted_iota(jnp.int32, (CHUNK, GROUP_W), 1) // HEAD_DIM
            for hh in range(HEADS_PER_GROUP):
                h = g * HEADS_PER_GROUP + hh
                xdt = xdt_g[:, hh * HEAD_DIM:(hh + 1) * HEAD_DIM]
                cs_b = jnp.broadcast_to(cs[:, h:h + 1], (CHUNK, CHUNK))
                lmat = jnp.where(lower, jnp.exp(cs_b - csT[h:h + 1, :]), 0.0)
                m_parts_g.append((cb * lmat).astype(BF16))
                x_bd.append(jnp.where(lane_head == hh, xdt_g, 0.0).astype(BF16))
                cs_bh = cs_b[:, 0:HEAD_DIM]
                e_parts.append(jnp.exp(cs_bh))
                xdd.append((xdt * jnp.exp(csT[h:h + 1, CHUNK - 1:CHUNK] - cs_bh)).astype(BF16))
                dec_row.append(jnp.broadcast_to(cdec[:, h:h + 1], (1, HEAD_DIM)))
            y_diag = _dot(jnp.concatenate(m_parts_g, axis=1), jnp.concatenate(x_bd, axis=0))
            y_heads.append(y_diag + y_off * jnp.concatenate(e_parts, axis=-1))
            contrib = _dot_tn(bg, jnp.concatenate(xdd, axis=-1))
            st_scr[g] = st_scr[g] * jnp.concatenate(dec_row, axis=-1) + contrib
        y_chunks.append(jnp.concatenate(y_heads, axis=-1))
    y = y_chunks[0] if len(y_chunks) == 1 else jnp.concatenate(y_chunks, axis=0)
    y = (y + dskip_ref[...] * xs) * _silu(z)
    m_parts = []
    for g in range(N_GROUPS):
        yg = y[:, g * GROUP_W:(g + 1) * GROUP_W]
        m_parts.append(_rms(yg, sng_ref[:, g * GROUP_W:(g + 1) * GROUP_W]))
    m_out = jnp.concatenate(m_parts, axis=-1).astype(BF16)

    h1_ref[0] = x + _dot(a_out, wout_ref[0:C_CONF, :]) + _dot(m_out, wout_ref[C_CONF:, :])

    @pl.when(t == nt - 1)
    def _():
        for lc in range(N_LANE_TILES):
            ncc_ref[0, :, lc * LANES:(lc + 1) * LANES] = cscr[lc, pl.ds(CONF_PAD - (K_CONF - 1), K_CONF - 1), :]
        for lt in range(N_XBC_TILES):
            nsc_ref[0, :, lt * LANES:(lt + 1) * LANES] = mscr[lt, pl.ds(SSM_PAD - (K_SSM - 1), K_SSM - 1), :]
        for g in range(N_GROUPS):
            nss_ref[0, g * HEADS_PER_GROUP:(g + 1) * HEADS_PER_GROUP] = (
                st_scr[g].T.reshape(HEADS_PER_GROUP, HEAD_DIM, D_STATE))


def _const_spec(shape):
    nd = len(shape)
    return pl.BlockSpec(shape, lambda *_: (0,) * nd, pipeline_mode=pl.Buffered(1))


def _mixer_prompt(x, w, tl=256):
    bsz, seq, _ = x.shape
    kern = functools.partial(_mixer_prompt_kernel, tl=tl)
    consts = [w['g_mix'], w['w_in'], w['w_dtT'], w['conf_w'], w['conf_b'], w['ln_g'], w['ln_b'],
              w['ssm_w3'], w['ssm_b3'], w['dt_bias'], w['dt_biasT'], w['a_log'], w['a_logT'], w['d_skip'],
              w['ssm_norm_g'], w['w_out'], w['head_expand']]
    return pl.pallas_call(
        kern,
        grid=(bsz, seq // tl),
        in_specs=[pl.BlockSpec((1, tl, D_MODEL), lambda b, t: (b, t, 0))] + [_const_spec(c.shape) for c in consts],
        out_specs=[
            pl.BlockSpec((1, tl, D_MODEL), lambda b, t: (b, t, 0)),
            pl.BlockSpec((1, K_CONF - 1, C_CONF), lambda b, t: (b, 0, 0)),
            pl.BlockSpec((1, K_SSM - 1, CONV_DIM), lambda b, t: (b, 0, 0)),
            pl.BlockSpec((1, N_HEADS, HEAD_DIM, D_STATE), lambda b, t: (b, 0, 0, 0)),
        ],
        out_shape=[
            jax.ShapeDtypeStruct((bsz, seq, D_MODEL), F32),
            jax.ShapeDtypeStruct((bsz, K_CONF - 1, C_CONF), F32),
            jax.ShapeDtypeStruct((bsz, K_SSM - 1, CONV_DIM), F32),
            jax.ShapeDtypeStruct((bsz, N_HEADS, HEAD_DIM, D_STATE), F32),
        ],
        scratch_shapes=[
            pltpu.VMEM((N_LANE_TILES, CONF_PAD + tl, LANES), F32),
            pltpu.VMEM((N_LANE_TILES, tl, LANES), F32),
            pltpu.VMEM((N_XBC_TILES, SSM_PAD + tl, LANES), F32),
            pltpu.VMEM((N_XBC_TILES, tl, LANES), F32),
            pltpu.VMEM((N_GROUPS, D_STATE, GROUP_W), F32),
        ],
        compiler_params=pltpu.CompilerParams(
            dimension_semantics=("arbitrary", "arbitrary"), vmem_limit_bytes=VMEM_LIMIT),
        name="mixer_prompt",
    )(x, *consts)


def _mixer_sample_kernel(x_ref, cst_ref, mst_ref, sst_ref, gmix_ref, win_ref, cw_ref, cb_ref, lng_ref, lnb_ref,
                         sw_ref, sb_ref, dtb_ref, alog_ref, dskip_ref, sng_ref, wout_ref, hexp_ref,
                         h1_ref, ncc_ref, nsc_ref, nss_ref,
                         proj_scr, mix_scr, *, bb):
    i = pl.program_id(0)
    n = pl.num_programs(0)

    @pl.when(i == 0)
    def _():
        u = _rms(x_ref[...], gmix_ref[...]).astype(BF16)
        proj_scr[...] = _dot(u, win_ref[...])

    r0 = pl.multiple_of(i * bb, bb)
    proj = proj_scr[pl.ds(r0, bb), :]

    glu = proj[:, 0:O_GATE] * _sigmoid(proj[:, O_GATE:O_Z])
    acc = cb_ref[...] + cw_ref[pl.ds(K_CONF - 1, 1), :] * glu
    for k in range(K_CONF - 1):
        row_k = cst_ref[k]
        acc = acc + cw_ref[pl.ds(k, 1), :] * row_k
        if k >= 1:
            ncc_ref[k - 1] = row_k
    ncc_ref[K_CONF - 2] = glu
    mean = jnp.mean(acc, axis=-1, keepdims=True)
    xc = acc - mean
    rstd = lax.rsqrt(jnp.mean(xc * xc, axis=-1, keepdims=True) + EPS)
    a_out = _silu(xc * rstd * lng_ref[...] + lnb_ref[...])

    z = proj[:, O_Z:O_XBC]
    xbc_raw = proj[:, O_XBC:O_DT]
    acc = sb_ref[...] + sw_ref[pl.ds(K_SSM - 1, 1), :] * xbc_raw
    for k in range(K_SSM - 1):
        row_k = mst_ref[:, k, :]
        acc = acc + sw_ref[pl.ds(k, 1), :] * row_k
        if k >= 1:
            nsc_ref[:, k - 1, :] = row_k
    nsc_ref[:, K_SSM - 2, :] = xbc_raw
    xbc = _silu(acc)
    xs = xbc[:, 0:D_INNER]
    bm = xbc[:, D_INNER:D_INNER + N_GROUPS * D_STATE]
    cm = xbc[:, D_INNER + N_GROUPS * D_STATE:]
    dt = _softplus(proj[:, O_DT:IN_COLS] + dtb_ref[...])
    dec = jnp.exp(dt * (-jnp.exp(alog_ref[...])))
    hexp = hexp_ref[...]
    t_h, t_m, t_l = _split3(dt)
    dt_e = _dot(t_h, hexp) + _dot(t_m, hexp) + _dot(t_l, hexp)
    xdt = xs * dt_e

    lane = lax.broadcasted_iota(jnp.int32, (SUBLANES, D_INNER), 1)
    sub = lax.broadcasted_iota(jnp.int32, (SUBLANES, D_INNER), 0)
    gmask = (lane // GROUP_W) == sub
    eye_h = (lax.broadcasted_iota(jnp.int32, (N_HEADS, N_HEADS), 0)
             == lax.broadcasted_iota(jnp.int32, (N_HEADS, N_HEADS), 1))
    ones_hn = jnp.ones((3 * N_HEADS, D_STATE), BF16)

    y_rows = []
    for b in range(bb):
        xrow = jnp.where(gmask, jnp.broadcast_to(xdt[b:b + 1, :], (SUBLANES, D_INNER)), 0.0)
        x_h = xrow.astype(BF16).astype(F32)
        x_m = (xrow - x_h).astype(BF16).astype(F32)
        bmat = jnp.concatenate([bm[b:b + 1, g * D_STATE:(g + 1) * D_STATE] for g in range(N_GROUPS)]
                               + [jnp.zeros((SUBLANES - N_GROUPS, D_STATE), F32)], axis=0)
        b_h = bmat.astype(BF16).astype(F32)
        b_m = (bmat - b_h).astype(BF16).astype(F32)
        lhs = jnp.concatenate([x_h, x_h, x_m], axis=0).astype(BF16)
        rhs = jnp.concatenate([b_h, b_m, b_h], axis=0).astype(BF16)
        upd = _dot_tn(lhs, rhs)
        dg = jnp.where(eye_h, jnp.broadcast_to(dec[b:b + 1, :], (N_HEADS, N_HEADS)), 0.0)
        g_h, g_m, g_l = [v.astype(F32) for v in _split3(dg)]
        drow = _dot(jnp.concatenate([g_h, g_m, g_l], axis=1).astype(BF16), ones_hn)
        h_parts = []
        for h in range(N_HEADS):
            h_new = drow[h:h + 1, :] * sst_ref[b, h] + upd[h * HEAD_DIM:(h + 1) * HEAD_DIM, :]
            nss_ref[b, h] = h_new
            h_parts.append(h_new)
        hnew = jnp.concatenate(h_parts, axis=0)
        cmat = jnp.concatenate([cm[b:b + 1, g * D_STATE:(g + 1) * D_STATE] for g in range(N_GROUPS)]
                               + [jnp.zeros((SUBLANES - N_GROUPS, D_STATE), F32)], axis=0)
        yg = _dot_nt(cmat.astype(BF16), hnew.astype(BF16))
        y_rows.append(jnp.sum(jnp.where(gmask, yg, 0.0), axis=0, keepdims=True))
    y = jnp.concatenate(y_rows, axis=0)
    y = (y + dskip_ref[...] * xs) * _silu(z)
    m_parts = []
    for g in range(N_GROUPS):
        m_parts.append(_rms(y[:, g * GROUP_W:(g + 1) * GROUP_W], sng_ref[:, g * GROUP_W:(g + 1) * GROUP_W]))
    mix_scr[pl.ds(r0, bb), :] = jnp.concatenate([a_out] + m_parts, axis=-1)

    @pl.when(i == n - 1)
    def _():
        h1_ref[...] = x_ref[...] + _dot(mix_scr[...].astype(BF16), wout_ref[...])


def _mixer_sample(x, cst, mst, sst, w, bb=8):
    nb = x.shape[0]
    kern = functools.partial(_mixer_sample_kernel, bb=bb)
    consts = [w['g_mix'], w['w_in'], w['conf_w2'], w['conf_b2'], w['ln_g'], w['ln_b'],
              w['ssm_w'], w['ssm_b'], w['dt_bias'], w['a_log'], w['d_skip'], w['ssm_norm_g'], w['w_out'],
              w['head_expand']]
    return pl.pallas_call(
        kern,
        grid=(nb // bb,),
        in_specs=[
            _const_spec((nb, D_MODEL)),
            pl.BlockSpec((K_CONF - 1, bb, C_CONF), lambda i: (0, i, 0)),
            pl.BlockSpec((bb, K_SSM - 1, CONV_DIM), lambda i: (i, 0, 0)),
            pl.BlockSpec((bb, N_HEADS, HEAD_DIM, D_STATE), lambda i: (i, 0, 0, 0)),
        ] + [_const_spec(c.shape) for c in consts],
        out_specs=[
            pl.BlockSpec((nb, D_MODEL), lambda i: (0, 0)),
            pl.BlockSpec((K_CONF - 1, bb, C_CONF), lambda i: (0, i, 0)),
            pl.BlockSpec((bb, K_SSM - 1, CONV_DIM), lambda i: (i, 0, 0)),
            pl.BlockSpec((bb, N_HEADS, HEAD_DIM, D_STATE), lambda i: (i, 0, 0, 0)),
        ],
        out_shape=[
            jax.ShapeDtypeStruct((nb, D_MODEL), F32),
            jax.ShapeDtypeStruct((K_CONF - 1, nb, C_CONF), F32),
            jax.ShapeDtypeStruct((nb, K_SSM - 1, CONV_DIM), F32),
            jax.ShapeDtypeStruct((nb, N_HEADS, HEAD_DIM, D_STATE), F32),
        ],
        scratch_shapes=[
            pltpu.VMEM((nb, IN_COLS), F32),
            pltpu.VMEM((nb, D_MODEL + D_INNER), F32),
        ],
        compiler_params=pltpu.CompilerParams(dimension_semantics=("arbitrary",), vmem_limit_bytes=VMEM_LIMIT),
        name="mixer_sample",
    )(x, cst, mst, sst, *consts)


def _first_argmax(v, width):
    lane = lax.broadcasted_iota(jnp.int32, v.shape, 1)
    m = jnp.max(v, axis=-1, keepdims=True)
    idx = jnp.min(jnp.where(v == m, lane, width), axis=-1, keepdims=True)
    return m, idx


def _ffn_kernel(h_ref, p_ref, gffn_ref, wr_ref, br_ref, wg_ref, wu_ref, wd_ref, gple_ref, wpg_ref, wp_ref,
                gfin_ref, y_ref, t_scr, comb_scr, acc_scr):
    e = pl.program_id(1)
    ne = pl.num_programs(1)

    @pl.when(e == 0)
    def _():
        tf = _rms(h_ref[...], gffn_ref[...])
        t_scr[...] = tf.astype(BF16)
        t_h, t_m, t_l = _split3(tf)
        w_h = wr_ref[0]
        w_m = wr_ref[1]
        logits = (_dot(t_h, w_h) + _dot(t_m, w_h) + _dot(t_h, w_m) + _dot(t_l, w_h) + _dot(t_m, w_m)) + br_ref[...]
        lg = logits[:, 0:N_EXPERT_GROUPS]
        le = logits[:, N_EXPERTS:2 * N_EXPERTS]
        eg = jnp.exp(lg - jnp.max(lg, axis=-1, keepdims=True))
        pg = eg / jnp.sum(eg, axis=-1, keepdims=True)
        g_val, g_idx = _first_argmax(pg, N_EXPERT_GROUPS)
        lane16 = lax.broadcasted_iota(jnp.int32, le.shape, 1)
        in_grp = (lane16 // EXPERTS_PER_GROUP) == g_idx
        neg = jnp.float32(-jnp.inf)
        le_m = jnp.where(in_grp, le, neg)
        ee = jnp.where(in_grp, jnp.exp(le - jnp.max(le_m, axis=-1, keepdims=True)), 0.0)
        pe = ee / jnp.sum(ee, axis=-1, keepdims=True)
        pe_m = jnp.where(in_grp, pe, -1.0)
        v1, i1 = _first_argmax(pe_m, N_EXPERTS)
        pe_m2 = jnp.where(lane16 == i1, -1.0, pe_m)
        v2, i2 = _first_argmax(pe_m2, N_EXPERTS)
        den = v1 + v2
        comb = jnp.where(lane16 == i1, g_val * v1 / den, 0.0) + jnp.where(lane16 == i2, g_val * v2 / den, 0.0)
        comb_scr[...] = comb
        acc_scr[...] = jnp.zeros(acc_scr.shape, F32)

    t = t_scr[...]
    hdn = _silu(_dot(t, wg_ref[0].astype(BF16))) * _dot(t, wu_ref[0].astype(BF16))
    out_e = _dot(hdn.astype(BF16), wd_ref[0].astype(BF16))
    lane16 = lax.broadcasted_iota(jnp.int32, comb_scr.shape, 1)
    c_e = jnp.sum(jnp.where(lane16 == e, comb_scr[...], 0.0), axis=-1, keepdims=True)
    acc_scr[...] += c_e * out_e

    @pl.when(e == ne - 1)
    def _():
        h2 = h_ref[...] + acc_scr[...]
        gate = _sigmoid(_dot(_rms(h2, gple_ref[...]).astype(BF16), wpg_ref[...]))
        h3 = h2 + _dot(p_ref[...].astype(BF16), wp_ref[...]) * gate
        y_ref[...] = _rms(h3, gfin_ref[...])


def _ffn(h, p, w, tb):
    nt = h.shape[0]
    return pl.pallas_call(
        _ffn_kernel,
        grid=(nt // tb, N_EXPERTS),
        in_specs=[
            pl.BlockSpec((tb, D_MODEL), lambda i, e: (i, 0)),
            pl.BlockSpec((tb, PLE_DIM), lambda i, e: (i, 0)),
            _const_spec((1, D_MODEL)),
            _const_spec((2, D_MODEL, 2 * N_EXPERTS)),
            _const_spec((1, 2 * N_EXPERTS)),
            pl.BlockSpec((1, D_MODEL, D_FF), lambda i, e: (e, 0, 0)),
            pl.BlockSpec((1, D_MODEL, D_FF), lambda i, e: (e, 0, 0)),
            pl.BlockSpec((1, D_FF, D_MODEL), lambda i, e: (e, 0, 0)),
            _const_spec((1, D_MODEL)),
            _const_spec((D_MODEL, D_MODEL)),
            _const_spec((PLE_DIM, D_MODEL)),
            _const_spec((1, D_MODEL)),
        ],
        out_specs=pl.BlockSpec((tb, D_MODEL), lambda i, e: (i, 0)),
        out_shape=jax.ShapeDtypeStruct((nt, D_MODEL), F32),
        scratch_shapes=[
            pltpu.VMEM((tb, D_MODEL), BF16),
            pltpu.VMEM((tb, N_EXPERTS), F32),
            pltpu.VMEM((tb, D_MODEL), F32),
        ],
        compiler_params=pltpu.CompilerParams(
            dimension_semantics=("arbitrary", "arbitrary"), vmem_limit_bytes=VMEM_LIMIT),
        name="ffn",
    )(h, p, w['g_ffn'], w['w_router'], w['b_router'], w['w_gate_e'], w['w_up_e'], w['w_down_e'],
      w['g_ple'], w['w_ple_gate'], w['w_ple'], w['g_final'])


ROUTE_TB = 1024
EXPERT_TM = 512
SC_WINDOW = 128
SC_CORES = 2
SC_SUBCORES = 16
SC_WORKERS = SC_CORES * SC_SUBCORES
HALF = D_MODEL // 2


def _pack_bf16_pairs(v):
    bits = pltpu.bitcast(v.astype(BF16).astype(F32), jnp.uint32)
    packed = bits[:, HALF:] | (bits[:, :HALF] >> 16)
    return pltpu.bitcast(packed, jnp.int32)


def _unpack_bf16_pairs(p):
    u = pltpu.bitcast(p, jnp.uint32)
    lo = pltpu.bitcast(u << 16, F32)
    hi = pltpu.bitcast(u & jnp.uint32(0xFFFF0000), F32)
    return jnp.concatenate([lo, hi], axis=-1)


def _route_kernel(h_ref, gffn_ref, wrT_ref, brT_ref, tp_ref, meta_ref, wts_ref, cnt_ref, upper_scr, carry_scr):
    i = pl.program_id(0)
    tb = h_ref.shape[0]

    @pl.when(i == 0)
    def _():
        r = lax.broadcasted_iota(jnp.int32, (tb, tb), 0)
        c = lax.broadcasted_iota(jnp.int32, (tb, tb), 1)
        upper_scr[...] = (r < c).astype(BF16)
        carry_scr[...] = jnp.zeros(carry_scr.shape, F32)

    tf = _rms(h_ref[...], gffn_ref[...])
    tp_ref[...] = _pack_bf16_pairs(tf)
    t_h = tf.astype(BF16)
    t_m = (tf - t_h.astype(F32)).astype(BF16)
    w_h = wrT_ref[0]
    w_m = wrT_ref[1]
    logits = (_dot_nt(w_h, t_h) + _dot_nt(w_h, t_m) + _dot_nt(w_m, t_h)) + brT_ref[...]
    neg = -jnp.inf
    row8 = lax.broadcasted_iota(jnp.int32, (SUBLANES, tb), 0)
    row16 = lax.broadcasted_iota(jnp.int32, (N_EXPERTS, tb), 0)
    lg = jnp.where(row8 < N_EXPERT_GROUPS, logits[0:SUBLANES], neg)
    le = logits[N_EXPERTS:2 * N_EXPERTS]
    eg = jnp.exp(lg - jnp.max(lg, axis=0, keepdims=True))
    pg = eg / jnp.sum(eg, axis=0, keepdims=True)
    g_val = jnp.max(pg, axis=0, keepdims=True)
    g_idx = jnp.min(jnp.where(pg == g_val, row8, SUBLANES), axis=0, keepdims=True)
    in_grp = (row16 // EXPERTS_PER_GROUP) == g_idx
    le_m = jnp.where(in_grp, le, neg)
    ee = jnp.where(in_grp, jnp.exp(le - jnp.max(le_m, axis=0, keepdims=True)), 0.0)
    pe = ee / jnp.sum(ee, axis=0, keepdims=True)
    pe_m = jnp.where(in_grp, pe, -1.0)
    v1 = jnp.max(pe_m, axis=0, keepdims=True)
    i1 = jnp.min(jnp.where(pe_m == v1, row16, N_EXPERTS), axis=0, keepdims=True)
    pe_m2 = jnp.where(row16 == i1, -1.0, pe_m)
    v2 = jnp.max(pe_m2, axis=0, keepdims=True)
    i2 = jnp.min(jnp.where(pe_m2 == v2, row16, N_EXPERTS), axis=0, keepdims=True)
    den = v1 + v2
    w0 = g_val * v1 / den
    w1 = g_val * v2 / den

    sel0 = row16 == i1
    sel1 = row16 == i2
    hot = jnp.where(sel0 | sel1, 1.0, 0.0)
    rank = _dot(hot.astype(BF16), upper_scr[...]) + carry_scr[...]
    r0 = jnp.sum(jnp.where(sel0, rank, 0.0), axis=0, keepdims=True)
    r1 = jnp.sum(jnp.where(sel1, rank, 0.0), axis=0, keepdims=True)
    carry_scr[...] += jnp.sum(hot, axis=1, keepdims=True)
    cnt_ref[...] = carry_scr[...]

    meta = jnp.where(row8 == 0, i1, jnp.where(row8 == 1, i2, jnp.where(
        row8 == 2, r0.astype(jnp.int32), jnp.where(row8 == 3, r1.astype(jnp.int32), 0))))
    meta_ref[...] = meta
    row128 = lax.broadcasted_iota(jnp.int32, (LANES, tb), 0)
    wts_ref[...] = jnp.where(row128 == 0, w0, jnp.where(row128 == 1, w1, 0.0)).T


def _route(h, w, tb=ROUTE_TB):
    nt = h.shape[0]
    return pl.pallas_call(
        _route_kernel,
        grid=(nt // tb,),
        in_specs=[
            pl.BlockSpec((tb, D_MODEL), lambda i: (i, 0)),
            _const_spec((1, D_MODEL)),
            _const_spec((2, 2 * N_EXPERTS, D_MODEL)),
            _const_spec((2 * N_EXPERTS, 1)),
        ],
        out_specs=[
            pl.BlockSpec((tb, HALF), lambda i: (i, 0)),
            pl.BlockSpec((SUBLANES, tb), lambda i: (0, i)),
            pl.BlockSpec((tb, LANES), lambda i: (i, 0)),
            pl.BlockSpec((N_EXPERTS, 1), lambda i: (0, 0)),
        ],
        out_shape=[
            jax.ShapeDtypeStruct((nt, HALF), jnp.int32),
            jax.ShapeDtypeStruct((SUBLANES, nt), jnp.int32),
            jax.ShapeDtypeStruct((nt, LANES), F32),
            jax.ShapeDtypeStruct((N_EXPERTS, 1), F32),
        ],
        scratch_shapes=[pltpu.VMEM((tb, tb), BF16), pltpu.VMEM((N_EXPERTS, 1), F32)],
        compiler_params=pltpu.CompilerParams(dimension_semantics=("arbitrary",), vmem_limit_bytes=VMEM_LIMIT),
        name="moe_route",
    )(h, w['g_ffn'], w['w_routerT'], w['b_routerT'])


def _sc_mesh():
    return plsc.VectorSubcoreMesh(core_axis_name="c", subcore_axis_name="s")


def _sc_dispatch(tp, slot0, slot1, n_slots):
    nt = tp.shape[0]

    per_worker = nt // (SC_WINDOW * SC_WORKERS)

    @pl.kernel(out_type=jax.ShapeDtypeStruct((n_slots, HALF), tp.dtype), mesh=_sc_mesh(), name="moe_dispatch",
               scratch_types=[pltpu.VMEM((1, SC_WINDOW), jnp.int32), pltpu.VMEM((1, SC_WINDOW), jnp.int32),
                              pltpu.VMEM((SC_WINDOW, HALF), tp.dtype),
                              pltpu.SemaphoreType.DMA, pltpu.SemaphoreType.DMA, pltpu.SemaphoreType.DMA])
    def run(x_hbm, i0_hbm, i1_hbm, o_hbm, i0_v, i1_v, buf_v, sem_a, sem_b, sem_c):
        worker = lax.axis_index("c") * SC_SUBCORES + lax.axis_index("s")

        @pl.loop(0, per_worker)
        def _(k):
            base = (worker * per_worker + k) * SC_WINDOW
            ld0 = pltpu.async_copy(i0_hbm.at[:, pl.ds(base, SC_WINDOW)], i0_v, sem_a)
            ld1 = pltpu.async_copy(i1_hbm.at[:, pl.ds(base, SC_WINDOW)], i1_v, sem_b)
            ldx = pltpu.async_copy(x_hbm.at[pl.ds(base, SC_WINDOW)], buf_v, sem_c)
            ld0.wait()
            ld1.wait()
            ldx.wait()
            st0 = pltpu.async_copy(buf_v, o_hbm.at[i0_v.at[0]], sem_a)
            st1 = pltpu.async_copy(buf_v, o_hbm.at[i1_v.at[0]], sem_b)
            st0.wait()
            st1.wait()

    return run(tp, slot0, slot1)


def _sc_combine(ys, slot0, slot1):
    nt = slot0.shape[1]
    out = jax.ShapeDtypeStruct((nt, HALF), ys.dtype)

    per_worker = nt // (SC_WINDOW * SC_WORKERS)

    @pl.kernel(out_type=(out, out), mesh=_sc_mesh(), name="moe_combine",
               scratch_types=[pltpu.VMEM((1, SC_WINDOW), jnp.int32), pltpu.VMEM((1, SC_WINDOW), jnp.int32),
                              pltpu.VMEM((SC_WINDOW, HALF), ys.dtype)])
    def run(y_hbm, i0_hbm, i1_hbm, o0_hbm, o1_hbm, i0_v, i1_v, buf_v):
        worker = lax.axis_index("c") * SC_SUBCORES + lax.axis_index("s")

        @pl.loop(0, per_worker)
        def _(k):
            base = (worker * per_worker + k) * SC_WINDOW
            pltpu.sync_copy(i0_hbm.at[:, pl.ds(base, SC_WINDOW)], i0_v)
            pltpu.sync_copy(i1_hbm.at[:, pl.ds(base, SC_WINDOW)], i1_v)
            pltpu.sync_copy(y_hbm.at[i0_v.at[0]], buf_v)
            pltpu.sync_copy(buf_v, o0_hbm.at[pl.ds(base, SC_WINDOW)])
            pltpu.sync_copy(y_hbm.at[i1_v.at[0]], buf_v)
            pltpu.sync_copy(buf_v, o1_hbm.at[pl.ds(base, SC_WINDOW)])

    return run(ys, slot0, slot1)


def _experts_kernel(te_ref, nv_ref, x_ref, wg_ref, wu_ref, wd_ref, y_ref, wg_s, wu_s, wd_s):
    j = pl.program_id(0)

    @pl.when((j == 0) | (te_ref[j] != te_ref[jnp.maximum(j - 1, 0)]))
    def _():
        wg_s[...] = wg_ref[0].astype(BF16)
        wu_s[...] = wu_ref[0].astype(BF16)
        wd_s[...] = wd_ref[0].astype(BF16)

    @pl.when(j < nv_ref[0])
    def _():
        x = _unpack_bf16_pairs(x_ref[...]).astype(BF16)
        hdn = _silu(_dot(x, wg_s[...])) * _dot(x, wu_s[...])
        y_ref[...] = _pack_bf16_pairs(_dot(hdn.astype(BF16), wd_s[...]))


def _experts(xs, tile_expert, n_valid, w, tm=EXPERT_TM):
    n_slots = xs.shape[0]
    grid_spec = pltpu.PrefetchScalarGridSpec(
        num_scalar_prefetch=2,
        grid=(n_slots // tm,),
        in_specs=[
            pl.BlockSpec((tm, HALF), lambda j, te, nv: (j, 0)),
            pl.BlockSpec((1, D_MODEL, D_FF), lambda j, te, nv: (te[j], 0, 0)),
            pl.BlockSpec((1, D_MODEL, D_FF), lambda j, te, nv: (te[j], 0, 0)),
            pl.BlockSpec((1, D_FF, D_MODEL), lambda j, te, nv: (te[j], 0, 0)),
        ],
        out_specs=pl.BlockSpec((tm, HALF), lambda j, te, nv: (j, 0)),
        scratch_shapes=[pltpu.VMEM((D_MODEL, D_FF), BF16), pltpu.VMEM((D_MODEL, D_FF), BF16),
                        pltpu.VMEM((D_FF, D_MODEL), BF16)],
    )
    return pl.pallas_call(
        _experts_kernel,
        grid_spec=grid_spec,
        out_shape=jax.ShapeDtypeStruct((n_slots, HALF), jnp.int32),
        compiler_params=pltpu.CompilerParams(dimension_semantics=("arbitrary",), vmem_limit_bytes=VMEM_LIMIT),
        name="moe_experts",
    )(tile_expert, n_valid, xs, w['w_gate_e'], w['w_up_e'], w['w_down_e'])


def _finish_kernel(h_ref, y0_ref, y1_ref, wts_ref, p_ref, gple_ref, wpg_ref, wp_ref, gfin_ref, o_ref):
    wts = wts_ref[...]
    moe = wts[:, 0:1] * _unpack_bf16_pairs(y0_ref[...]) + wts[:, 1:2] * _unpack_bf16_pairs(y1_ref[...])
    h2 = h_ref[...] + moe
    gate = _sigmoid(_dot(_rms(h2, gple_ref[...]).astype(BF16), wpg_ref[...]))
    h3 = h2 + _dot(p_ref[...].astype(BF16), wp_ref[...]) * gate
    o_ref[...] = _rms(h3, gfin_ref[...])


def _finish(h, y0, y1, wts, p, w, tb=1024):
    nt = h.shape[0]
    return pl.pallas_call(
        _finish_kernel,
        grid=(nt // tb,),
        in_specs=[
            pl.BlockSpec((tb, D_MODEL), lambda i: (i, 0)),
            pl.BlockSpec((tb, HALF), lambda i: (i, 0)),
            pl.BlockSpec((tb, HALF), lambda i: (i, 0)),
            pl.BlockSpec((tb, LANES), lambda i: (i, 0)),
            pl.BlockSpec((tb, PLE_DIM), lambda i: (i, 0)),
            _const_spec((1, D_MODEL)),
            _const_spec((D_MODEL, D_MODEL)),
            _const_spec((PLE_DIM, D_MODEL)),
            _const_spec((1, D_MODEL)),
        ],
        out_specs=pl.BlockSpec((tb, D_MODEL), lambda i: (i, 0)),
        out_shape=jax.ShapeDtypeStruct((nt, D_MODEL), F32),
        compiler_params=pltpu.CompilerParams(dimension_semantics=("arbitrary",), vmem_limit_bytes=VMEM_LIMIT),
        name="moe_finish",
    )(h, y0, y1, wts, p, w['g_ple'], w['w_ple_gate'], w['w_ple'], w['g_final'])


def _ffn_sparse(h, p, w):
    nt = h.shape[0]
    tm = EXPERT_TM
    n_tiles = (nt * 2) // tm + N_EXPERTS
    tp, meta, wts, counts = _route(h, w)
    cnt = counts.reshape(N_EXPERTS).astype(jnp.int32)
    tiles_e = (cnt + tm - 1) // tm
    tile_end = jnp.cumsum(tiles_e)
    off = (tile_end - tiles_e) * tm
    n_valid = tile_end[-1:]
    tile_ids = jnp.arange(n_tiles, dtype=jnp.int32)
    tile_expert = jnp.sum((tile_ids[:, None] >= tile_end[None, :]).astype(jnp.int32), axis=1)
    last_expert = jnp.sum((n_valid - 1 >= tile_end).astype(jnp.int32))
    tile_expert = jnp.minimum(tile_expert, last_expert).astype(jnp.int32)
    eid = meta[0:2]
    slot = meta[2:4] + jnp.sum(jnp.where(eid[..., None] == jnp.arange(N_EXPERTS), off, 0), axis=-1)
    slot0 = slot[0:1]
    slot1 = slot[1:2]
    xs = _sc_dispatch(tp, slot0, slot1, n_tiles * tm)
    ys = _experts(xs, tile_expert, n_valid.astype(jnp.int32), w)
    y0, y1 = _sc_combine(ys, slot0, slot1)
    return _finish(h, y0, y1, wts, p, w)


def _prep_weights(g_mix, w_in, conf_dw_w, conf_dw_b, conf_ln_g, conf_ln_b, ssm_conv_w, ssm_conv_b,
                  dt_bias, a_log, d_skip, ssm_norm_g, w_out, g_ffn, w_rg, b_rg, w_re, b_re,
                  w_gate_e, w_up_e, w_down_e, g_ple, w_ple_gate, w_ple, g_final):
    row = lambda v: v.reshape(1, -1)
    w_in_b = w_in.astype(BF16)
    conf_w_pad = jnp.concatenate([conf_dw_w, jnp.zeros((CONF_PAD - K_CONF, C_CONF), F32)], axis=0)
    w_router = jnp.concatenate(
        [w_rg, jnp.zeros((D_MODEL, N_EXPERTS - N_EXPERT_GROUPS), F32), w_re], axis=1)
    wr_hi = w_router.astype(BF16)
    wr_mid = (w_router - wr_hi.astype(F32)).astype(BF16)
    b_router = jnp.concatenate([b_rg, jnp.zeros((N_EXPERTS - N_EXPERT_GROUPS,), F32), b_re]).reshape(1, -1)
    head_of_lane = jnp.arange(D_INNER) // HEAD_DIM
    head_expand = (head_of_lane[None, :] == jnp.arange(N_HEADS)[:, None]).astype(BF16)
    return dict(
        g_mix=row(g_mix), w_in=w_in_b, w_dtT=w_in_b[:, O_DT:].T,
        conf_w=conf_w_pad.reshape(CONF_PAD, N_LANE_TILES, LANES).transpose(1, 0, 2),
        conf_b=conf_dw_b.reshape(N_LANE_TILES, 1, LANES),
        conf_w2=conf_dw_w, conf_b2=row(conf_dw_b),
        ssm_w3=ssm_conv_w.reshape(K_SSM, N_XBC_TILES, LANES).transpose(1, 0, 2),
        ssm_b3=ssm_conv_b.reshape(N_XBC_TILES, 1, LANES),
        ln_g=row(conf_ln_g), ln_b=row(conf_ln_b), ssm_w=ssm_conv_w, ssm_b=row(ssm_conv_b),
        dt_bias=row(dt_bias), dt_biasT=dt_bias.reshape(-1, 1), a_log=row(a_log), a_logT=a_log.reshape(-1, 1),
        d_skip=row(jnp.repeat(d_skip, HEAD_DIM)), ssm_norm_g=row(ssm_norm_g), w_out=w_out.astype(BF16),
        head_expand=head_expand,
        g_ffn=row(g_ffn), w_router=jnp.stack([wr_hi, wr_mid]), b_router=b_router,
        w_routerT=jnp.stack([wr_hi.T, wr_mid.T]), b_routerT=b_router.reshape(-1, 1),
        w_gate_e=w_gate_e, w_up_e=w_up_e, w_down_e=w_down_e,
        g_ple=row(g_ple), w_ple_gate=w_ple_gate.astype(BF16), w_ple=w_ple.astype(BF16), g_final=row(g_final),
    )


def kernel(x_prompt, x_sample, p_prompt, p_sample, state_conf_conv, state_ssm_conv, state_ssm, g_mix, w_in, conf_dw_w, conf_dw_b, conf_ln_g, conf_ln_b, ssm_conv_w, ssm_conv_b, dt_bias, a_log, d_skip, ssm_norm_g, w_out, g_ffn, w_rg, b_rg, w_re, b_re, w_gate_e, w_up_e, w_down_e, g_ple, w_ple_gate, w_ple, g_final):
    depth = g_mix.shape[0]
    bsz, seq, _ = x_prompt.shape
    nb = x_sample.shape[0]
    hp = x_prompt
    hs = x_sample.reshape(nb, D_MODEL)
    cp_l, mp_l, sp_l, cs_l, ms_l, ss_l = [], [], [], [], [], []
    for i in range(depth):
        w = _prep_weights(g_mix[i], w_in[i], conf_dw_w[i], conf_dw_b[i], conf_ln_g[i], conf_ln_b[i],
                          ssm_conv_w[i], ssm_conv_b[i], dt_bias[i], a_log[i], d_skip[i], ssm_norm_g[i], w_out[i],
                          g_ffn[i], w_rg[i], b_rg[i], w_re[i], b_re[i], w_gate_e[i], w_up_e[i], w_down_e[i],
                          g_ple[i], w_ple_gate[i], w_ple[i], g_final)
        h1p, c, m, s = _mixer_prompt(hp, w)
        cp_l.append(c)
        mp_l.append(m)
        sp_l.append(s)
        h1s, c, m, s = _mixer_sample(hs, jnp.transpose(state_conf_conv[i], (1, 0, 2)), state_ssm_conv[i],
                                     state_ssm[i], w)
        cs_l.append(jnp.transpose(c, (1, 0, 2)))
        ms_l.append(m)
        ss_l.append(s)
        assert depth == 1
        hp = _ffn_sparse(h1p.reshape(bsz * seq, D_MODEL), p_prompt[i].reshape(bsz * seq, PLE_DIM), w)
        hp = hp.reshape(bsz, seq, D_MODEL)
        hs = _ffn(h1s, p_sample[i].reshape(nb, PLE_DIM), w, tb=nb)
    return (hp, hs.reshape(nb, 1, D_MODEL), jnp.stack(cp_l), jnp.stack(mp_l), jnp.stack(sp_l),
            jnp.stack(cs_l), jnp.stack(ms_l), jnp.stack(ss_l))
```
